```python
import math
import jax
import jax.numpy as jnp
from jax import lax
import numpy as np

D_MODEL = 4096
BATCH = 4
SEQ = 4096
DEPTH = 2

MIX = D_MODEL // 2
HEAD_DIM = 128
N_HEADS = MIX // HEAD_DIM
IN_COLS = 5 * MIX
CONV_K = 31
DSW_PATTERNS = ((128, 1), (512, 4), (2048, 16))
NUM_BUCKETS = 32
MAX_DISTANCE = 2048
HGRN_CHUNK = 64
POOL_WINDOWS = (2, 4, 8, 16)
POOL_GROUP = MIX // len(POOL_WINDOWS)
D_FF = 4 * D_MODEL
ALPHA = (2.0 * DEPTH) ** 0.25
BETA = (8.0 * DEPTH) ** -0.25
N_EVEN = (DEPTH + 1) // 2
N_ODD = DEPTH // 2
LN_EPS = 1e-5

kernel_name = 'hybrid_conv_dilattn_hgrn2_pool_block'


def layer_norm(x, g, b):
    xf = x.astype(jnp.float32)
    mu = jnp.mean(xf, axis=-1, keepdims=True)
    var = jnp.mean(jnp.square(xf - mu), axis=-1, keepdims=True)
    y = (xf - mu) * lax.rsqrt(var + LN_EPS)
    return (y * g.astype(jnp.float32) + b.astype(jnp.float32)).astype(x.dtype)


def t5_bucket(dist):
    max_exact = NUM_BUCKETS // 2
    nf = jnp.maximum(dist, 1).astype(jnp.float32)
    large = max_exact + (jnp.log(nf / max_exact) / math.log(MAX_DISTANCE / max_exact)
                         * (NUM_BUCKETS - max_exact)).astype(jnp.int32)
    large = jnp.minimum(large, NUM_BUCKETS - 1)
    return jnp.where(dist < max_exact, dist, large)


def conformer_conv(a_val, a_gate, conv_w, conv_b, ln_g, ln_b):
    u = a_val * jax.nn.sigmoid(a_gate)
    u = lax.conv_general_dilated(
        u, conv_w[:, None, :].astype(u.dtype), (1,), [(CONV_K - 1, 0)],
        dimension_numbers=('NWC', 'WIO', 'NWC'), feature_group_count=u.shape[-1])
    u = u + conv_b.astype(u.dtype)
    return jax.nn.silu(layer_norm(u, ln_g, ln_b))


def dilated_group(q, k, v, rel_bias, window, dil):
    bsz, seq, heads, hd = q.shape
    steps = window // dil
    sub_len = seq // dil
    nb = -(-sub_len // steps)
    pad_len = nb * steps

    def strided(t):
        t = t.reshape(bsz, sub_len, dil, heads, hd)
        return jnp.pad(t, ((0, 0), (0, pad_len - sub_len), (0, 0), (0, 0), (0, 0)))

    def band(t):
        t = jnp.pad(strided(t), ((0, 0), (steps, 0), (0, 0), (0, 0), (0, 0)))
        t = t.reshape(bsz, nb + 1, steps, dil, heads, hd)
        return jnp.concatenate([t[:, :-1], t[:, 1:]], axis=2)

    qb = strided(q).reshape(bsz, nb, steps, dil, heads, hd)
    kb, vb = band(k), band(v)
    qi = jnp.arange(steps)[:, None]
    ki = jnp.arange(2 * steps)[None, :]
    step = qi + steps - ki
    bias = rel_bias.astype(jnp.float32)[t5_bucket(jnp.clip(step, 0, steps) * dil)]
    bias = bias.transpose(2, 0, 1)
    key_pos = jnp.arange(nb)[:, None, None] * steps + ki[None] - steps
    valid = (step >= 0) & (step <= steps) & (key_pos >= 0)
    s = jnp.einsum('bnqrhe,bnkrhe->bnrhqk', qb, kb) * (hd ** -0.5) + bias
    s = jnp.where(valid[None, :, None, None], s, -jnp.inf)
    m = jnp.max(s, axis=-1, keepdims=True)
    p = jnp.exp(s - m)
    l = jnp.sum(p, axis=-1)
    o = jnp.einsum('bnrhqk,bnkrhe->bnqrhe', p, vb) / l.transpose(0, 1, 4, 2, 3)[..., None]
    lse = (m[..., 0] + jnp.log(l)).transpose(0, 1, 4, 2, 3)
    o = o.reshape(bsz, pad_len, dil, heads, hd)[:, :sub_len].reshape(bsz, seq, heads, hd)
    lse = lse.reshape(bsz, pad_len, dil, heads)[:, :sub_len].reshape(bsz, seq, heads)
    return o, lse


def dilated_attention(q, k, v, rel_bias):
    q, k, v = (t.astype(jnp.float32) for t in (q, k, v))
    res = [dilated_group(q, k, v, rel_bias, w, d) for (w, d) in DSW_PATTERNS]
    o = jnp.stack([r[0] for r in res])
    lse = jnp.stack([r[1] for r in res])
    wts = jax.nn.softmax(lse, axis=0)
    return jnp.sum(wts[..., None] * o, axis=0)


def hgrn2(q, f_pre, i, g_out, lb, norm_g):
    bsz, seq, _ = q.shape
    dt = q.dtype
    q = jax.nn.silu(q.astype(jnp.float32)).reshape(bsz, seq, N_HEADS, HEAD_DIM)
    f = lb + (1.0 - lb) * jax.nn.sigmoid(f_pre.astype(jnp.float32))
    logf = jnp.log(f).reshape(bsz, seq, N_HEADS, HEAD_DIM)
    kk = (1.0 - f).reshape(bsz, seq, N_HEADS, HEAD_DIM)
    v = i.astype(jnp.float32).reshape(bsz, seq, N_HEADS, HEAD_DIM)
    n_chunks = seq // HGRN_CHUNK

    def to_chunks(t):
        return t.reshape(bsz, n_chunks, HGRN_CHUNK, N_HEADS, HEAD_DIM).transpose(1, 0, 3, 2, 4)

    causal = jnp.tril(jnp.ones((HGRN_CHUNK, HGRN_CHUNK), dtype=bool))

    def step(state, inp):
        qc, kc, vc, gc = inp
        b = jnp.cumsum(gc, axis=2)
        inter = jnp.einsum('bhtk,bhkv->bhtv', qc * jnp.exp(b), state)
        diff = b[:, :, :, None, :] - b[:, :, None, :, :]
        decay = jnp.exp(jnp.where(causal[None, None, :, :, None], diff, -jnp.inf))
        attn = jnp.einsum('bhtk,bhtsk,bhsk->bhts', qc, decay, kc)
        o = inter + jnp.einsum('bhts,bhsv->bhtv', attn, vc)
        b_last = b[:, :, -1:, :]
        state = (jnp.exp(b_last[:, :, 0, :, None]) * state
                 + jnp.einsum('bhsk,bhsv->bhkv', kc * jnp.exp(b_last - b), vc))
        return state, o

    s0 = jnp.zeros((bsz, N_HEADS, HEAD_DIM, HEAD_DIM), jnp.float32)
    _, o = lax.scan(step, s0, (to_chunks(q), to_chunks(kk), to_chunks(v), to_chunks(logf)))
    o = o.transpose(1, 0, 3, 2, 4).reshape(bsz, seq, N_HEADS, HEAD_DIM)
    o = o * lax.rsqrt(jnp.mean(jnp.square(o), axis=-1, keepdims=True) + LN_EPS)
    o = o * norm_g.astype(jnp.float32).reshape(N_HEADS, HEAD_DIM)
    o = o.reshape(bsz, seq, MIX) * jax.nn.silu(g_out.astype(jnp.float32))
    return o.astype(dt)


def multiscale_pool(p, pool_w, pool_scale):
    bsz, seq, _ = p.shape
    pg = p.astype(jnp.float32).reshape(bsz, seq, len(POOL_WINDOWS), POOL_GROUP)
    cs0 = jnp.pad(jnp.cumsum(pg, axis=1), ((0, 0), (1, 0), (0, 0), (0, 0)))
    t = jnp.arange(seq)
    outs = []
    for gi, w in enumerate(POOL_WINDOWS):
        cg = cs0[:, :, gi]
        lag = jnp.pad(cg[:, :seq], ((0, 0), (w - 1, 0), (0, 0)))[:, :seq]
        mean = (cg[:, 1:] - lag) / jnp.minimum(t + 1, w).astype(jnp.float32)[None, :, None]
        outs.append(mean - pg[:, :, gi])
    pooled = jnp.stack(outs, axis=2).astype(p.dtype)
    y = jnp.einsum('bsgc,gcd->bsgd', pooled, pool_w).reshape(bsz, seq, MIX)
    return (y * pool_scale).astype(p.dtype)


def even_mixer(h, w_in, w_out, conv_w, conv_b, conv_ln_g, conv_ln_b, rel_bias):
    bsz, seq, _ = h.shape
    u = h @ w_in
    a_val, a_gate, q, k, v = jnp.split(u, 5, axis=-1)
    a_out = conformer_conv(a_val, a_gate, conv_w, conv_b, conv_ln_g, conv_ln_b)
    heads = lambda t: t.reshape(bsz, seq, N_HEADS, HEAD_DIM)
    b_out = dilated_attention(heads(q), heads(k), heads(v), rel_bias).reshape(bsz, seq, MIX)
    return jnp.concatenate([a_out, b_out.astype(h.dtype)], axis=-1) @ w_out


def odd_mixer(h, w_in, w_out, lb, norm_g, pool_w, pool_scale):
    u = h @ w_in
    cq, cf, ci, cg, dp = jnp.split(u, 5, axis=-1)
    c_out = hgrn2(cq, cf, ci, cg, lb, norm_g)
    d_out = multiscale_pool(dp, pool_w, pool_scale)
    return jnp.concatenate([c_out, d_out], axis=-1) @ w_out


def setup_inputs(seed: int = 0) -> dict:
    key = jax.random.key(seed)
    ks = jax.random.split(key, 20)
    nrm = lambda k, shape, s: jax.random.normal(k, shape, jnp.float32) * s
    return {
        'x': nrm(ks[0], (BATCH, SEQ, D_MODEL), 1.0),
        'c': nrm(ks[1], (BATCH, D_MODEL), 1.0),
        'ada_w': nrm(ks[2], (DEPTH, D_MODEL, 6 * D_MODEL), 0.5 * D_MODEL ** -0.5),
        'ada_b': nrm(ks[3], (DEPTH, 6 * D_MODEL), 0.01),
        'w_in': nrm(ks[4], (DEPTH, D_MODEL, IN_COLS), D_MODEL ** -0.5),
        'w_out': nrm(ks[5], (DEPTH, 2 * MIX, D_MODEL), BETA * (2 * MIX) ** -0.5),
        'ln_g': 1.0 + nrm(ks[6], (DEPTH, 2, D_MODEL), 0.05),
        'ln_b': nrm(ks[7], (DEPTH, 2, D_MODEL), 0.02),
        'mlp_w1': nrm(ks[8], (DEPTH, D_MODEL, D_FF), BETA * D_MODEL ** -0.5),
        'mlp_w2': nrm(ks[9], (DEPTH, D_FF, D_MODEL), BETA * D_FF ** -0.5),
        'conv_w': nrm(ks[10], (N_EVEN, CONV_K, MIX), CONV_K ** -0.5),
        'conv_b': nrm(ks[11], (N_EVEN, MIX), 0.02),
        'conv_ln_g': 1.0 + nrm(ks[12], (N_EVEN, MIX), 0.05),
        'conv_ln_b': nrm(ks[13], (N_EVEN, MIX), 0.02),
        'rel_bias': nrm(ks[14], (NUM_BUCKETS, N_HEADS), 0.5),
        'hgrn_lb_logits': nrm(ks[15], (DEPTH, MIX), 0.5),
        'hgrn_norm_g': 1.0 + nrm(ks[16], (N_ODD, MIX), 0.05),
        'pool_w': nrm(ks[17], (N_ODD, len(POOL_WINDOWS), POOL_GROUP, POOL_GROUP), POOL_GROUP ** -0.5),
        'pool_scale': 1.0 + nrm(ks[18], (N_ODD, MIX), 0.1),
    }


def reference(x, c, ada_w, ada_b, w_in, w_out, ln_g, ln_b, mlp_w1, mlp_w2,
              conv_w, conv_b, conv_ln_g, conv_ln_b, rel_bias,
              hgrn_lb_logits, hgrn_norm_g, pool_w, pool_scale):
    lb_all = jnp.cumsum(jax.nn.softmax(hgrn_lb_logits.astype(jnp.float32), axis=0), axis=0)
    lb_all = lb_all - lb_all[0]
    cs = jax.nn.silu(c)
    for l in range(DEPTH):
        mod = cs @ ada_w[l] + ada_b[l]
        sh1, sc1, g1, sh2, sc2, g2 = jnp.split(mod[:, None, :], 6, axis=-1)
        h = x * (1.0 + sc1) + sh1
        if l % 2 == 0:
            e = l // 2
            y = even_mixer(h, w_in[l], w_out[l], conv_w[e], conv_b[e],
                           conv_ln_g[e], conv_ln_b[e], rel_bias)
        else:
            o = l // 2
            y = odd_mixer(h, w_in[l], w_out[l], lb_all[l], hgrn_norm_g[o],
                          pool_w[o], pool_scale[o])
        x = layer_norm(ALPHA * x + g1 * y, ln_g[l, 0], ln_b[l, 0])
        h = x * (1.0 + sc2) + sh2
        y = jnp.square(jax.nn.relu(h @ mlp_w1[l])) @ mlp_w2[l]
        x = layer_norm(ALPHA * x + g2 * y, ln_g[l, 1], ln_b[l, 1])
    return x
```

```python
import functools
import math

import jax
import jax.numpy as jnp
from jax import lax
from jax.experimental import pallas as pl
from jax.experimental.pallas import tpu as pltpu

F32 = jnp.float32
BF16 = jnp.bfloat16

HEAD_DIM = 128
CONV_K = 31
DSW_PATTERNS = ((128, 1), (512, 4), (2048, 16))
NUM_BUCKETS = 32
MAX_DISTANCE = 2048
HGRN_CHUNK = 64
POOL_WINDOWS = (2, 4, 8, 16)
LN_EPS = 1e-5
NEG = -1e30

V7X_VMEM_BYTES = 64 * 1024 * 1024
VMEM_LIMIT = 56 * 1024 * 1024


def _cparams(sem):
    return pltpu.CompilerParams(dimension_semantics=sem, vmem_limit_bytes=VMEM_LIMIT)


def _silu(v):
    return v * jax.nn.sigmoid(v)


def _mod_kernel(c_ref, w_ref, b_ref, o_ref):
    cs = _silu(c_ref[...]).astype(BF16)
    w = w_ref[0].astype(BF16)
    o_ref[0] = jnp.dot(cs, w, preferred_element_type=F32) + b_ref[0]


def adaln_mod(c, ada_w, ada_b, *, tn=512):
    nl, d, n6 = ada_w.shape
    bsz = c.shape[0]
    rows = 8
    c8 = jnp.zeros((rows, d), F32).at[:bsz].set(c)
    out = pl.pallas_call(
        _mod_kernel,
        grid=(nl, n6 // tn),
        in_specs=[
            pl.BlockSpec((rows, d), lambda l, j: (0, 0)),
            pl.BlockSpec((1, d, tn), lambda l, j: (l, 0, j)),
            pl.BlockSpec((1, 1, tn), lambda l, j: (l, 0, j)),
        ],
        out_specs=pl.BlockSpec((1, rows, tn), lambda l, j: (l, 0, j)),
        out_shape=jax.ShapeDtypeStruct((nl, rows, n6), F32),
        compiler_params=_cparams(("arbitrary", "arbitrary")),
        name="adaln_mod",
    )(c8, ada_w, ada_b.reshape(nl, 1, n6))
    return out[:, :bsz].reshape(nl, bsz, 6, d)


def _ln_rows(o_ref, g_ref, b_ref, rc=8):
    def body(i, carry):
        r = pl.ds(pl.multiple_of(i * rc, rc), rc)
        v = o_ref[r, :]
        mu = jnp.mean(v, axis=-1, keepdims=True)
        dv = v - mu
        var = jnp.mean(dv * dv, axis=-1, keepdims=True)
        o_ref[r, :] = dv * lax.rsqrt(var + LN_EPS) * g_ref[...] + b_ref[...]
        return carry
    lax.fori_loop(0, o_ref.shape[0] // rc, body, 0)


def _mm_in_kernel(x_ref, mod_ref, w_ref, o_ref, h_ref, *, n_pro, pc):
    p = pl.program_id(1)
    for c in range(n_pro):
        @pl.when(p == c)
        def _(c=c):
            cols = slice(c * pc, (c + 1) * pc)
            h = x_ref[...] * (1.0 + mod_ref[0, 1:2, cols]) + mod_ref[0, 0:1, cols]
            h_ref[:, cols] = h.astype(BF16)

    @pl.when(p >= n_pro)
    def _():
        o_ref[...] = jnp.dot(h_ref[...], w_ref[...], preferred_element_type=F32).astype(o_ref.dtype)


def mm_in(x2, mod_l, w, seq, *, tm=1024, tn=1024, pc=512):
    m, d = x2.shape
    n = w.shape[1]
    n_pro = d // pc
    return pl.pallas_call(
        functools.partial(_mm_in_kernel, n_pro=n_pro, pc=pc),
        grid=(m // tm, n_pro + n // tn),
        in_specs=[
            pl.BlockSpec((tm, pc), lambda i, p: (i, jnp.minimum(p, n_pro - 1))),
            pl.BlockSpec((1, 6, d), lambda i, p: ((i * tm) // seq, 0, 0)),
            pl.BlockSpec((d, tn), lambda i, p: (0, jnp.maximum(p - n_pro, 0))),
        ],
        out_specs=pl.BlockSpec((tm, tn), lambda i, p: (i, jnp.maximum(p - n_pro, 0))),
        out_shape=jax.ShapeDtypeStruct((m, n), BF16),
        scratch_shapes=[pltpu.VMEM((tm, d), BF16)],
        compiler_params=_cparams(("arbitrary", "arbitrary")),
        name="mm_in",
    )(x2, mod_l, w)


def _mm_out_kernel(a_ref, b_ref, w_ref, x_ref, mod_ref, g_ref, beta_ref, o_ref, *, n_n, tn, k1, alpha):
    n = pl.program_id(1)
    y = jnp.dot(a_ref[...], w_ref[0:k1, :], preferred_element_type=F32)
    y = y + jnp.dot(b_ref[...], w_ref[k1:, :], preferred_element_type=F32)
    for c in range(n_n):
        @pl.when(n == c)
        def _(c=c):
            cols = slice(c * tn, (c + 1) * tn)
            o_ref[:, cols] = alpha * x_ref[...] + mod_ref[0, 2:3, cols] * y

    @pl.when(n == n_n - 1)
    def _():
        _ln_rows(o_ref, g_ref, beta_ref)


def mm_out_ln(za, zb, w, x2, mod_l, ln_g, ln_b, seq, alpha, *, tm=512, tn=512):
    m, k1 = za.shape
    d = w.shape[1]
    n_n = d // tn
    return pl.pallas_call(
        functools.partial(_mm_out_kernel, n_n=n_n, tn=tn, k1=k1, alpha=alpha),
        grid=(m // tm, n_n),
        in_specs=[
            pl.BlockSpec((tm, k1), lambda i, n: (i, 0)),
            pl.BlockSpec((tm, zb.shape[1]), lambda i, n: (i, 0)),
            pl.BlockSpec((w.shape[0], tn), lambda i, n: (0, n)),
            pl.BlockSpec((tm, tn), lambda i, n: (i, n)),
            pl.BlockSpec((1, 6, d), lambda i, n: ((i * tm) // seq, 0, 0)),
            pl.BlockSpec((1, d), lambda i, n: (0, 0)),
            pl.BlockSpec((1, d), lambda i, n: (0, 0)),
        ],
        out_specs=pl.BlockSpec((tm, d), lambda i, n: (i, 0)),
        out_shape=jax.ShapeDtypeStruct((m, d), F32),
        compiler_params=_cparams(("arbitrary", "arbitrary")),
        name="mm_out_ln",
    )(za, zb, w, x2, mod_l, ln_g.reshape(1, d), ln_b.reshape(1, d))


def _mlp_kernel(x_ref, mod_ref, w1_ref, w2_ref, g_ref, beta_ref, o_ref, h_ref, *, n_pro, pc, n_f, tn2, alpha):
    p = pl.program_id(1)
    d = o_ref.shape[1]
    for c in range(n_pro):
        @pl.when(p == c)
        def _(c=c):
            cols = slice(c * pc, (c + 1) * pc)
            xs = x_ref[...]
            h = xs * (1.0 + mod_ref[0, 4:5, cols]) + mod_ref[0, 3:4, cols]
            h_ref[:, cols] = h.astype(BF16)
            o_ref[:, cols] = alpha * xs

    @pl.when(p >= n_pro)
    def _():
        t = jnp.dot(h_ref[...], w1_ref[...], preferred_element_type=F32)
        t = jnp.maximum(t, 0.0)
        t = (t * t).astype(BF16)
        for nb in range(d // tn2):
            cols = slice(nb * tn2, (nb + 1) * tn2)
            y = jnp.dot(t, w2_ref[:, cols], preferred_element_type=F32)
            o_ref[:, cols] += mod_ref[0, 5:6, cols] * y

    @pl.when(p == n_pro + n_f - 1)
    def _():
        _ln_rows(o_ref, g_ref, beta_ref)


def mlp_ln(x2, mod_l, w1, w2, ln_g, ln_b, seq, alpha, *, tm=512, tf=512, pc=512, tn2=1024):
    m, d = x2.shape
    dff = w1.shape[1]
    n_pro = d // pc
    n_f = dff // tf
    return pl.pallas_call(
        functools.partial(_mlp_kernel, n_pro=n_pro, pc=pc, n_f=n_f, tn2=tn2, alpha=alpha),
        grid=(m // tm, n_pro + n_f),
        in_specs=[
            pl.BlockSpec((tm, pc), lambda i, p: (i, jnp.minimum(p, n_pro - 1))),
            pl.BlockSpec((1, 6, d), lambda i, p: ((i * tm) // seq, 0, 0)),
            pl.BlockSpec((d, tf), lambda i, p: (0, jnp.maximum(p - n_pro, 0))),
            pl.BlockSpec((tf, d), lambda i, p: (jnp.maximum(p - n_pro, 0), 0)),
            pl.BlockSpec((1, d), lambda i, p: (0, 0)),
            pl.BlockSpec((1, d), lambda i, p: (0, 0)),
        ],
        out_specs=pl.BlockSpec((tm, d), lambda i, p: (i, 0)),
        out_shape=jax.ShapeDtypeStruct((m, d), F32),
        scratch_shapes=[pltpu.VMEM((tm, d), BF16)],
        compiler_params=_cparams(("arbitrary", "arbitrary")),
        name="mlp_ln",
    )(x2, mod_l, w1, w2, ln_g.reshape(1, d), ln_b.reshape(1, d))


CONV_HALO = 32


def _conv_kernel(av_ref, ag_ref, w_ref, cb_ref, g_ref, beta_ref, o_ref, glu_ref, y_ref, *, ts, rc):
    s = pl.program_id(1)
    ch = o_ref.shape[2]

    @pl.when(s == 0)
    def _():
        glu_ref[:, 0:CONV_HALO, :] = jnp.zeros((ch // 128, CONV_HALO, 128), F32)

    @pl.when(s > 0)
    def _():
        glu_ref[:, 0:CONV_HALO, :] = glu_ref[:, ts:ts + CONV_HALO, :]

    def glu_body(i, carry):
        r0 = pl.multiple_of(i * rc, rc)
        a = av_ref[0, pl.ds(r0, rc), :].astype(F32)
        gt = ag_ref[0, pl.ds(r0, rc), :].astype(F32)
        glu = a * jax.nn.sigmoid(gt)
        for cc in range(ch // 128):
            glu_ref[cc, pl.ds(CONV_HALO + r0, rc), :] = glu[:, cc * 128:(cc + 1) * 128]
        return carry
    lax.fori_loop(0, ts // rc, glu_body, 0)

    off = CONV_HALO - (CONV_K - 1)

    def conv_body(i, carry):
        r0 = pl.multiple_of(i * rc, rc)
        for cc in range(ch // 128):
            lanes = slice(cc * 128, (cc + 1) * 128)
            acc = jnp.zeros((rc, 128), F32)
            for j in range(CONV_K):
                acc = acc + w_ref[j:j + 1, lanes] * glu_ref[cc, pl.ds(r0 + off + j, rc), :]
            y_ref[pl.ds(r0, rc), lanes] = acc + cb_ref[0:1, lanes]
        return carry
    lax.fori_loop(0, ts // rc, conv_body, 0)

    def ln_body(i, carry):
        r0 = pl.multiple_of(i * 16, 16)
        v = y_ref[pl.ds(r0, 16), :]
        mu = jnp.mean(v, axis=-1, keepdims=True)
        dv = v - mu
        var = jnp.mean(dv * dv, axis=-1, keepdims=True)
        z = dv * lax.rsqrt(var + LN_EPS) * g_ref[...] + beta_ref[...]
        o_ref[0, pl.ds(r0, 16), :] = _silu(z).astype(o_ref.dtype)
        return carry
    lax.fori_loop(0, ts // 16, ln_body, 0)


def conv_branch(u3, conv_w, conv_b, ln_g, ln_b, mix, *, ts=512, rc=32):
    bsz, seq, _ = u3.shape
    return pl.pallas_call(
        functools.partial(_conv_kernel, ts=ts, rc=rc),
        grid=(bsz, seq // ts),
        in_specs=[
            pl.BlockSpec((1, ts, mix), lambda b, s: (b, s, 0)),
            pl.BlockSpec((1, ts, mix), lambda b, s: (b, s, 1)),
            pl.BlockSpec((CONV_K, mix), lambda b, s: (0, 0)),
            pl.BlockSpec((1, mix), lambda b, s: (0, 0)),
            pl.BlockSpec((1, mix), lambda b, s: (0, 0)),
            pl.BlockSpec((1, mix), lambda b, s: (0, 0)),
        ],
        out_specs=pl.BlockSpec((1, ts, mix), lambda b, s: (b, s, 0)),
        out_shape=jax.ShapeDtypeStruct((bsz, seq, mix), BF16),
        scratch_shapes=[pltpu.VMEM((mix // 128, ts + CONV_HALO, 128), F32), pltpu.VMEM((ts, mix), F32)],
        compiler_params=_cparams(("arbitrary", "arbitrary")),
        name="conv_branch",
    )(u3, u3, conv_w, conv_b.reshape(1, mix), ln_g.reshape(1, mix), ln_b.reshape(1, mix))


def _t5_bucket(dist):
    max_exact = NUM_BUCKETS // 2
    nf = jnp.maximum(dist, 1).astype(F32)
    large = max_exact + (jnp.log(nf / max_exact) / math.log(MAX_DISTANCE / max_exact)
                         * (NUM_BUCKETS - max_exact)).astype(jnp.int32)
    large = jnp.minimum(large, NUM_BUCKETS - 1)
    return jnp.where(dist < max_exact, dist, large)


def _bucket_tiles():
    tiles = []
    for window, dil in DSW_PATTERNS:
        steps = window // dil
        qi = jnp.arange(steps)[:, None]
        ki = jnp.arange(2 * steps)[None, :]
        step = qi + steps - ki
        bucket = _t5_bucket(jnp.clip(step, 0, steps) * dil)
        valid = (step >= 0) & (step <= steps)
        tiles.append(jnp.where(valid, bucket, -1).astype(jnp.int32))
    return jnp.stack(tiles)


def _bias_kernel(rb_ref, idx_ref, o_ref, *, heads):
    idx = idx_ref[0]
    for h in range(heads):
        acc = jnp.full(idx.shape, NEG, F32)
        for bk in range(NUM_BUCKETS):
            acc = jnp.where(idx == bk, rb_ref[bk, h], acc)
        o_ref[0, h] = acc


def attn_bias(rel_bias):
    heads = rel_bias.shape[1]
    idx = _bucket_tiles()
    g, st, st2 = idx.shape
    return pl.pallas_call(
        functools.partial(_bias_kernel, heads=heads),
        grid=(g,),
        in_specs=[
            pl.BlockSpec(memory_space=pltpu.SMEM),
            pl.BlockSpec((1, st, st2), lambda i: (i, 0, 0)),
        ],
        out_specs=pl.BlockSpec((1, heads, st, st2), lambda i: (i, 0, 0, 0)),
        out_shape=jax.ShapeDtypeStruct((g, heads, st, st2), F32),
        compiler_params=_cparams(("arbitrary",)),
        name="attn_bias",
    )(rel_bias.astype(F32), idx)


def _attn_kernel(*refs, hb, qb, has_prev, emit_lse, scale):
    if has_prev:
        q_ref, k_ref, v_ref, kp_ref, vp_ref, bias_ref, op_ref, lp_ref = refs[:8]
        rest = refs[8:]
    else:
        q_ref, k_ref, v_ref, kp_ref, vp_ref, bias_ref = refs[:6]
        op_ref = lp_ref = None
        rest = refs[6:]
    o_ref = rest[0]
    l_ref = rest[1] if emit_lse else None
    t = pl.program_id(3)
    st = HEAD_DIM
    nt = (((1,), (1,)), ((), ()))
    for h in range(hb):
        lanes = slice(h * HEAD_DIM, (h + 1) * HEAD_DIM)
        for j in range(qb):
            rows = slice(j * st, (j + 1) * st)
            q = q_ref[0, rows, lanes]
            if j == 0:
                kp, vp = kp_ref[0, :, lanes], vp_ref[0, :, lanes]
            else:
                prow = slice((j - 1) * st, j * st)
                kp, vp = k_ref[0, prow, lanes], v_ref[0, prow, lanes]
            kc, vc = k_ref[0, rows, lanes], v_ref[0, rows, lanes]
            bias_p = bias_ref[0, h, :, 0:st]
            bias_c = bias_ref[0, h, :, st:2 * st]
            if j == 0:
                bias_p = jnp.where(t == 0, NEG, bias_p)
            s_p = lax.dot_general(q, kp, nt, preferred_element_type=F32) * scale + bias_p
            s_c = lax.dot_general(q, kc, nt, preferred_element_type=F32) * scale + bias_c
            m = jnp.max(jnp.maximum(s_p, s_c), axis=-1, keepdims=True)
            p_p = jnp.exp(s_p - m)
            p_c = jnp.exp(s_c - m)
            l = jnp.sum(p_p + p_c, axis=-1, keepdims=True)
            acc = jnp.dot(p_p.astype(BF16), vp, preferred_element_type=F32)
            acc = acc + jnp.dot(p_c.astype(BF16), vc, preferred_element_type=F32)
            o = acc / l
            lse = m + jnp.log(l)
            if has_prev:
                lse0 = lp_ref[0, rows, lanes]
                o0 = op_ref[0, rows, lanes]
                mx = jnp.maximum(lse0, lse)
                w0 = jnp.exp(lse0 - mx)
                w1 = jnp.exp(lse - mx)
                den = w0 + w1
                o = (w0 * o0 + w1 * o) / den
                lse = mx + jnp.log(den)
            o_ref[0, rows, lanes] = o.astype(o_ref.dtype)
            if emit_lse:
                l_ref[0, rows, lanes] = jnp.broadcast_to(lse, (st, HEAD_DIM))


def attn_pattern(u3, bias_g, dil, prev, *, mix, last, hb=4, qb=2):
    bsz, seq, cols = u3.shape
    heads = mix // HEAD_DIM
    sub = seq // dil
    tq = min(qb * HEAD_DIM, sub)
    qb = tq // HEAD_DIM
    nt_ = sub // tq
    wb = hb * HEAD_DIM
    ub = cols // wb
    ob = mix // wb
    q0, k0, v0 = (2 * mix) // wb, (3 * mix) // wb, (4 * mix) // wb
    uv = u3.reshape(bsz, sub, dil * cols)
    pb = tq // HEAD_DIM

    def cur(c0):
        return pl.BlockSpec((1, tq, wb), lambda g, b, r, t: (b, t, r * ub + c0 + g))

    def prv(c0):
        return pl.BlockSpec((1, HEAD_DIM, wb), lambda g, b, r, t: (b, jnp.maximum(t * pb - 1, 0), r * ub + c0 + g))

    ospec = pl.BlockSpec((1, tq, wb), lambda g, b, r, t: (b, t, r * ob + g))
    in_specs = [cur(q0), cur(k0), cur(v0), prv(k0), prv(v0),
                pl.BlockSpec((1, hb, HEAD_DIM, 2 * HEAD_DIM), lambda g, b, r, t: (0, g, 0, 0))]
    args = [uv, uv, uv, uv, uv, bias_g]
    if prev is not None:
        in_specs += [ospec, ospec]
        args += [prev[0].reshape(bsz, sub, dil * mix), prev[1].reshape(bsz, sub, dil * mix)]
    oshape = jax.ShapeDtypeStruct((bsz, sub, dil * mix), BF16 if last else F32)
    if last:
        out_specs, out_shape = ospec, oshape
    else:
        out_specs = [ospec, ospec]
        out_shape = [oshape, jax.ShapeDtypeStruct((bsz, sub, dil * mix), F32)]
    res = pl.pallas_call(
        functools.partial(_attn_kernel, hb=hb, qb=qb, has_prev=prev is not None, emit_lse=not last,
                          scale=HEAD_DIM ** -0.5),
        grid=(heads // hb, bsz, dil, nt_),
        in_specs=in_specs,
        out_specs=out_specs,
        out_shape=out_shape,
        compiler_params=_cparams(("arbitrary",) * 4),
        name=f"attn_d{dil}",
    )(*args)
    if last:
        return res.reshape(bsz, seq, mix)
    return res[0].reshape(bsz, seq, mix), res[1].reshape(bsz, seq, mix)


def dilated_attention(u3, rel_bias, mix):
    bias = attn_bias(rel_bias)
    prev = None
    n = len(DSW_PATTERNS)
    for gi, (window, dil) in enumerate(DSW_PATTERNS):
        assert window // dil == HEAD_DIM
        prev = attn_pattern(u3, bias[gi:gi + 1], dil, prev, mix=mix, last=gi == n - 1)
    return prev


def _split3(v):
    hi = v.astype(BF16)
    r1 = v - hi.astype(F32)
    mid = r1.astype(BF16)
    lo = (r1 - mid.astype(F32)).astype(BF16)
    return hi, mid, lo


def _scaled(v, expo, mask):
    return jnp.where(mask, v * jnp.exp(jnp.where(mask, expo, 0.0)), 0.0)


def _hgrn_kernel(cq_ref, cf_ref, ci_ref, cg_ref, lb_ref, ng_ref, o_ref, st_ref, q_s, kk_s, b_s, *, hb, ts, layer):
    s = pl.program_id(2)
    c_len = HGRN_CHUNK
    sub = 16
    nt = (((1,), (1,)), ((), ()))
    tn_ = (((0,), (0,)), ((), ()))

    @pl.when(s == 0)
    def _():
        st_ref[...] = jnp.zeros(st_ref.shape, F32)

    row = lax.broadcasted_iota(jnp.int32, (c_len, c_len), 0)
    col = lax.broadcasted_iota(jnp.int32, (c_len, c_len), 1)
    tri = jnp.where(row >= col, 1.0, 0.0).astype(BF16)
    r64 = lax.broadcasted_iota(jnp.int32, (c_len, HEAD_DIM), 0)
    half = c_len // 2
    mask_b = (((row >= sub) & (row < half) & (col < sub))
              | ((row >= half + sub) & (col >= half) & (col < half + sub)))
    r16 = lax.broadcasted_iota(jnp.int32, (sub, HEAD_DIM), 0)
    c16 = lax.broadcasted_iota(jnp.int32, (sub, c_len), 1)
    lrow = lax.broadcasted_iota(jnp.int32, (lb_ref.shape[0], HEAD_DIM), 0)

    def chunk_body(ci, carry):
        r0 = pl.multiple_of(ci * c_len, c_len)
        rows = pl.ds(r0, c_len)
        for h in range(hb):
            lanes = slice(h * HEAD_DIM, (h + 1) * HEAD_DIM)
            lg = lb_ref[:, lanes]
            pe = jnp.exp(lg - jnp.max(lg, axis=0, keepdims=True))
            lb = jnp.sum(jnp.where((lrow >= 1) & (lrow <= layer), pe, 0.0), axis=0, keepdims=True) / jnp.sum(pe, axis=0, keepdims=True)
            q = _silu(cq_ref[0, rows, lanes].astype(F32))
            f = lb + (1.0 - lb) * jax.nn.sigmoid(cf_ref[0, rows, lanes].astype(F32))
            logf = jnp.log(f)
            kk = 1.0 - f
            v = ci_ref[0, rows, lanes]
            hi, mid, lo = _split3(logf)
            b = (jnp.dot(tri, hi, preferred_element_type=F32)
                 + jnp.dot(tri, mid, preferred_element_type=F32)
                 + jnp.dot(tri, lo, preferred_element_type=F32))
            q_s[h] = q
            kk_s[h] = kk
            b_s[h] = b
            b_last = b_s[h, c_len - 1:c_len, :]
            st_t = st_ref[h]
            inter = lax.dot_general((q * jnp.exp(b)).astype(BF16), st_t.astype(BF16), nt,
                                    preferred_element_type=F32)
            b_a = b_s[h, half - 1:half, :]
            qa = _scaled(q, b - b_a, r64 >= half)
            ka = _scaled(kk, b_a - b, r64 < half)
            attn = lax.dot_general(qa.astype(BF16), ka.astype(BF16), nt, preferred_element_type=F32)
            b_r = jnp.where(r64 < half, b_s[h, sub - 1:sub, :], b_s[h, half + sub - 1:half + sub, :])
            qsel = ((r64 >= sub) & (r64 < half)) | (r64 >= half + sub)
            ksel = (r64 < sub) | ((r64 >= half) & (r64 < half + sub))
            qbm = _scaled(q, b - b_r, qsel)
            kbm = _scaled(kk, b_r - b, ksel)
            attn_b = lax.dot_general(qbm.astype(BF16), kbm.astype(BF16), nt, preferred_element_type=F32)
            attn = attn + jnp.where(mask_b, attn_b, 0.0)
            diag_rows = []
            for jb in range(c_len // sub):
                blk = slice(jb * sub, (jb + 1) * sub)
                qt = q_s[h, blk, :]
                bt = b_s[h, blk, :]
                dblk = jnp.zeros((sub, c_len), F32)
                for si in range(sub):
                    r = jb * sub + si
                    e = jnp.exp(jnp.where(r16 >= si, bt - b_s[h, r:r + 1, :], NEG))
                    cvec = jnp.sum(qt * kk_s[h, r:r + 1, :] * e, axis=-1, keepdims=True)
                    dblk = jnp.where(c16 == r, cvec, dblk)
                diag_rows.append(dblk)
            attn = attn + jnp.concatenate(diag_rows, axis=0)
            o = inter + jnp.dot(attn.astype(BF16), v, preferred_element_type=F32)
            kd = (kk * jnp.exp(b_last - b)).astype(BF16)
            st_ref[h] = st_t * jnp.exp(b_last) + lax.dot_general(v, kd, tn_, preferred_element_type=F32)
            ms = jnp.mean(o * o, axis=-1, keepdims=True)
            o = o * lax.rsqrt(ms + LN_EPS) * ng_ref[0:1, lanes]
            o = o * _silu(cg_ref[0, rows, lanes].astype(F32))
            o_ref[0, rows, lanes] = o.astype(o_ref.dtype)
        return carry
    lax.fori_loop(0, ts // c_len, chunk_body, 0)


def hgrn_branch(u3, lb_logits, layer, norm_g, mix, *, hb=4, ts=256):
    bsz, seq, _ = u3.shape
    heads = mix // HEAD_DIM
    wb = hb * HEAD_DIM
    nb = mix // wb

    def col(k):
        return pl.BlockSpec((1, ts, wb), lambda b, g, s: (b, s, k * nb + g))

    vec = pl.BlockSpec((1, wb), lambda b, g, s: (0, g))
    return pl.pallas_call(
        functools.partial(_hgrn_kernel, hb=hb, ts=ts, layer=layer),
        grid=(bsz, heads // hb, seq // ts),
        in_specs=[col(0), col(1), col(2), col(3),
                  pl.BlockSpec((lb_logits.shape[0], wb), lambda b, g, s: (0, g)), vec],
        out_specs=pl.BlockSpec((1, ts, wb), lambda b, g, s: (b, s, g)),
        out_shape=jax.ShapeDtypeStruct((bsz, seq, mix), BF16),
        scratch_shapes=[pltpu.VMEM((hb, HEAD_DIM, HEAD_DIM), F32)]
        + [pltpu.VMEM((hb, HGRN_CHUNK, HEAD_DIM), F32)] * 3,
        compiler_params=_cparams(("arbitrary",) * 3),
        name="hgrn",
    )(u3, u3, u3, u3, lb_logits.astype(F32), norm_g.reshape(1, mix))


POOL_HALO = 16


def _pool_kernel(dp_ref, pw_ref, ps_ref, o_ref, x_ref, p_ref, *, ts, rc):
    s = pl.program_id(1)
    ch = o_ref.shape[2]
    grp = ch // len(POOL_WINDOWS)

    @pl.when(s == 0)
    def _():
        x_ref[:, 0:POOL_HALO, :] = jnp.zeros((ch // 128, POOL_HALO, 128), F32)

    @pl.when(s > 0)
    def _():
        x_ref[:, 0:POOL_HALO, :] = x_ref[:, ts:ts + POOL_HALO, :]

    def load_body(i, carry):
        r0 = pl.multiple_of(i * rc, rc)
        xv = dp_ref[0, pl.ds(r0, rc), :].astype(F32)
        for cc in range(ch // 128):
            x_ref[cc, pl.ds(POOL_HALO + r0, rc), :] = xv[:, cc * 128:(cc + 1) * 128]
        return carry
    lax.fori_loop(0, ts // rc, load_body, 0)

    def pool_body(i, carry):
        r0 = pl.multiple_of(i * rc, rc)
        pos = s * ts + r0 + lax.broadcasted_iota(jnp.int32, (rc, 1), 0)
        for cc in range(ch // 128):
            w = POOL_WINDOWS[(cc * 128) // grp]
            cur = x_ref[cc, pl.ds(POOL_HALO + r0, rc), :]
            tot = cur
            for j in range(1, w):
                tot = tot + x_ref[cc, pl.ds(POOL_HALO + r0 - j, rc), :]
            cnt = jnp.minimum(pos + 1, w).astype(F32)
            p_ref[pl.ds(r0, rc), cc * 128:(cc + 1) * 128] = (tot / cnt - cur).astype(BF16)
        return carry
    lax.fori_loop(0, ts // rc, pool_body, 0)

    for gi in range(len(POOL_WINDOWS)):
        lanes = slice(gi * grp, (gi + 1) * grp)
        y = jnp.dot(p_ref[:, lanes], pw_ref[gi], preferred_element_type=F32)
        o_ref[0, :, lanes] = (y * ps_ref[0:1, lanes]).astype(o_ref.dtype)


def pool_branch(u3, pool_w, pool_scale, mix, *, ts=512, rc=32):
    bsz, seq, _ = u3.shape
    ng, grp, _ = pool_w.shape
    return pl.pallas_call(
        functools.partial(_pool_kernel, ts=ts, rc=rc),
        grid=(bsz, seq // ts),
        in_specs=[
            pl.BlockSpec((1, ts, mix), lambda b, s: (b, s, 4)),
            pl.BlockSpec((ng, grp, grp), lambda b, s: (0, 0, 0)),
            pl.BlockSpec((1, mix), lambda b, s: (0, 0)),
        ],
        out_specs=pl.BlockSpec((1, ts, mix), lambda b, s: (b, s, 0)),
        out_shape=jax.ShapeDtypeStruct((bsz, seq, mix), BF16),
        scratch_shapes=[pltpu.VMEM((mix // 128, ts + POOL_HALO, 128), F32), pltpu.VMEM((ts, mix), BF16)],
        compiler_params=_cparams(("arbitrary", "arbitrary")),
        name="pool_branch",
    )(u3, pool_w.astype(BF16), pool_scale.reshape(1, mix))


def kernel(x, c, ada_w, ada_b, w_in, w_out, ln_g, ln_b, mlp_w1, mlp_w2, conv_w, conv_b, conv_ln_g, conv_ln_b,
           rel_bias, hgrn_lb_logits, hgrn_norm_g, pool_w, pool_scale):
    bsz, seq, d = x.shape
    depth = ada_w.shape[0]
    mix = d // 2
    alpha = (2.0 * depth) ** 0.25
    m = bsz * seq

    mod = adaln_mod(c, ada_w, ada_b)
    x2 = x.reshape(m, d)
    for l in range(depth):
        u = mm_in(x2, mod[l], w_in[l].astype(BF16), seq)
        u3 = u.reshape(bsz, seq, 5 * mix)
        if l % 2 == 0:
            e = l // 2
            za = conv_branch(u3, conv_w[e], conv_b[e], conv_ln_g[e], conv_ln_b[e], mix)
            zb = dilated_attention(u3, rel_bias, mix)
        else:
            o = l // 2
            za = hgrn_branch(u3, hgrn_lb_logits, l, hgrn_norm_g[o], mix)
            zb = pool_branch(u3, pool_w[o], pool_scale[o], mix)
        x2 = mm_out_ln(za.reshape(m, mix), zb.reshape(m, mix), w_out[l].astype(BF16), x2, mod[l],
                       ln_g[l, 0], ln_b[l, 0], seq, alpha)
        x2 = mlp_ln(x2, mod[l], mlp_w1[l].astype(BF16), mlp_w2[l].astype(BF16), ln_g[l, 1], ln_b[l, 1], seq, alpha)
    return x2.reshape(bsz, seq, d)
```

```python
import functools
import math

import jax
import jax.numpy as jnp
from jax import lax
from jax.experimental import pallas as pl
from jax.experimental.pallas import tpu as pltpu

F32 = jnp.float32
BF16 = jnp.bfloat16

HEAD_DIM = 128
CONV_K = 31
DSW_PATTERNS = ((128, 1), (512, 4), (2048, 16))
NUM_BUCKETS = 32
MAX_DISTANCE = 2048
HGRN_CHUNK = 64
POOL_WINDOWS = (2, 4, 8, 16)
LN_EPS = 1e-5
NEG = -1e30

V7X_VMEM_BYTES = 64 * 1024 * 1024
VMEM_LIMIT = 58 * 1024 * 1024


def _cparams(sem):
    return pltpu.CompilerParams(dimension_semantics=sem, vmem_limit_bytes=VMEM_LIMIT)


def _silu(v):
    return v * jax.nn.sigmoid(v)


def _mod_kernel(c_ref, w_ref, b_ref, o_ref):
    cs = _silu(c_ref[...]).astype(BF16)
    w = w_ref[0].astype(BF16)
    o_ref[0] = jnp.dot(cs, w, preferred_element_type=F32) + b_ref[0]


def adaln_mod(c, ada_w, ada_b, *, tn=512):
    nl, d, n6 = ada_w.shape
    bsz = c.shape[0]
    rows = 8
    c8 = jnp.zeros((rows, d), F32).at[:bsz].set(c)
    out = pl.pallas_call(
        _mod_kernel,
        grid=(nl, n6 // tn),
        in_specs=[
            pl.BlockSpec((rows, d), lambda l, j: (0, 0)),
            pl.BlockSpec((1, d, tn), lambda l, j: (l, 0, j)),
            pl.BlockSpec((1, 1, tn), lambda l, j: (l, 0, j)),
        ],
        out_specs=pl.BlockSpec((1, rows, tn), lambda l, j: (l, 0, j)),
        out_shape=jax.ShapeDtypeStruct((nl, rows, n6), F32),
        compiler_params=_cparams(("arbitrary", "arbitrary")),
        name="adaln_mod",
    )(c8, ada_w, ada_b.reshape(nl, 1, n6))
    return out[:, :bsz].reshape(nl, bsz, 6, d)


LN_ROWS = 16


def _ln_mod_rows(o_ref, h_ref, g_ref, b_ref, mod_ref, sh_row, sc_row):
    def body(i, carry):
        r = pl.ds(pl.multiple_of(i * LN_ROWS, LN_ROWS), LN_ROWS)
        v = o_ref[r, :]
        mu = jnp.mean(v, axis=-1, keepdims=True)
        dv = v - mu
        var = jnp.mean(dv * dv, axis=-1, keepdims=True)
        y = dv * lax.rsqrt(var + LN_EPS) * g_ref[...] + b_ref[...]
        o_ref[r, :] = y
        if h_ref is not None:
            h = y * (1.0 + mod_ref[0, sc_row:sc_row + 1, :]) + mod_ref[0, sh_row:sh_row + 1, :]
            h_ref[r, :] = h.astype(BF16)
        return carry
    lax.fori_loop(0, o_ref.shape[0] // LN_ROWS, body, 0, unroll=2)


def _cast_kernel(w_ref, o_ref):
    o_ref[...] = w_ref[0].astype(o_ref.dtype)


def cast_layer_bf16(w_stack, layer, *, tr=512, tc=2048):
    _, rows, cols = w_stack.shape
    tc = min(tc, cols)
    return pl.pallas_call(
        _cast_kernel,
        grid=(rows // tr, cols // tc),
        in_specs=[pl.BlockSpec((1, tr, tc), lambda i, j: (layer, i, j))],
        out_specs=pl.BlockSpec((tr, tc), lambda i, j: (i, j)),
        out_shape=jax.ShapeDtypeStruct((rows, cols), BF16),
        compiler_params=_cparams(("arbitrary", "arbitrary")),
        name="cast_bf16",
    )(w_stack)


def _modulate_kernel(x_ref, mod_ref, o_ref):
    o_ref[...] = (x_ref[...] * (1.0 + mod_ref[0, 1:2, :]) + mod_ref[0, 0:1, :]).astype(o_ref.dtype)


def modulate(x2, mod_l, seq, *, tm=256):
    m, d = x2.shape
    return pl.pallas_call(
        _modulate_kernel,
        grid=(m // tm,),
        in_specs=[
            pl.BlockSpec((tm, d), lambda i: (i, 0)),
            pl.BlockSpec((1, 6, d), lambda i: ((i * tm) // seq, 0, 0)),
        ],
        out_specs=pl.BlockSpec((tm, d), lambda i: (i, 0)),
        out_shape=jax.ShapeDtypeStruct((m, d), BF16),
        compiler_params=_cparams(("arbitrary",)),
        name="modulate",
    )(x2, mod_l)


def _mm_in_kernel(h_ref, w_ref, o_ref):
    o_ref[...] = jnp.dot(h_ref[...], w_ref[...], preferred_element_type=F32).astype(o_ref.dtype)


def mm_in(h, w, *, tm=1024, tn=1024):
    m, d = h.shape
    n = w.shape[1]
    return pl.pallas_call(
        _mm_in_kernel,
        grid=(m // tm, n // tn),
        in_specs=[
            pl.BlockSpec((tm, d), lambda i, j: (i, 0)),
            pl.BlockSpec((d, tn), lambda i, j: (0, j)),
        ],
        out_specs=pl.BlockSpec((tm, tn), lambda i, j: (i, j)),
        out_shape=jax.ShapeDtypeStruct((m, n), BF16),
        compiler_params=_cparams(("arbitrary", "arbitrary")),
        name="mm_in",
    )(h, w)


def _mm_out_kernel(a_ref, b_ref, w_ref, x_ref, mod_ref, g_ref, beta_ref, o_ref, h_ref, *, n_n, tn, k1, alpha):
    n = pl.program_id(1)
    y = jnp.dot(a_ref[...], w_ref[0:k1, :], preferred_element_type=F32)
    y = y + jnp.dot(b_ref[...], w_ref[k1:, :], preferred_element_type=F32)
    cols = pl.ds(pl.multiple_of(n * tn, tn), tn)
    o_ref[:, cols] = alpha * x_ref[...] + mod_ref[0, 2:3, cols] * y

    @pl.when(n == n_n - 1)
    def _():
        _ln_mod_rows(o_ref, h_ref, g_ref, beta_ref, mod_ref, 3, 4)


def mm_out_ln(za, zb, w, x2, mod_l, ln_g, ln_b, seq, alpha, *, tm=512, tn=512):
    m, k1 = za.shape
    d = w.shape[1]
    n_n = d // tn
    row = pl.BlockSpec((tm, d), lambda i, n: (i, 0))
    return pl.pallas_call(
        functools.partial(_mm_out_kernel, n_n=n_n, tn=tn, k1=k1, alpha=alpha),
        grid=(m // tm, n_n),
        in_specs=[
            pl.BlockSpec((tm, k1), lambda i, n: (i, 0)),
            pl.BlockSpec((tm, zb.shape[1]), lambda i, n: (i, 0)),
            pl.BlockSpec((w.shape[0], tn), lambda i, n: (0, n)),
            pl.BlockSpec((tm, tn), lambda i, n: (i, n)),
            pl.BlockSpec((1, 6, d), lambda i, n: ((i * tm) // seq, 0, 0)),
            pl.BlockSpec((1, d), lambda i, n: (0, 0)),
            pl.BlockSpec((1, d), lambda i, n: (0, 0)),
        ],
        out_specs=[row, row],
        out_shape=[jax.ShapeDtypeStruct((m, d), F32), jax.ShapeDtypeStruct((m, d), BF16)],
        compiler_params=_cparams(("arbitrary", "arbitrary")),
        name="mm_out_ln",
    )(za, zb, w, x2, mod_l, ln_g.reshape(1, d), ln_b.reshape(1, d))


def _mlp_kernel(h_ref, x_ref, mod_ref, modn_ref, w1_ref, w2_ref, g_ref, beta_ref, o_ref, *rest,
                n_x, xc, n_f, tn2, alpha):
    ho_ref = rest[0] if rest else None
    f = pl.program_id(1)
    d = o_ref.shape[1]

    @pl.when(f == 0)
    def _():
        o_ref[...] = jnp.zeros(o_ref.shape, F32)

    t = jnp.dot(h_ref[...], w1_ref[...], preferred_element_type=F32)
    t = jnp.maximum(t, 0.0)
    t = (t * t).astype(BF16)
    for nb in range(d // tn2):
        cols = slice(nb * tn2, (nb + 1) * tn2)
        y = jnp.dot(t, w2_ref[:, cols], preferred_element_type=F32)
        o_ref[:, cols] += mod_ref[0, 5:6, cols] * y

    @pl.when(f < n_x)
    def _():
        cols = pl.ds(pl.multiple_of(f * xc, xc), xc)
        o_ref[:, cols] += alpha * x_ref[...]

    @pl.when(f == n_f - 1)
    def _():
        _ln_mod_rows(o_ref, ho_ref, g_ref, beta_ref, modn_ref, 0, 1)


def mlp_ln(h, x2, mod_l, mod_next, w1, w2, ln_g, ln_b, seq, alpha, *, tm=512, tf=512, xc=512, tn2=512):
    m, d = x2.shape
    dff = w1.shape[1]
    n_x = d // xc
    n_f = dff // tf
    assert n_f >= n_x
    emit_h = mod_next is not None
    row = pl.BlockSpec((tm, d), lambda i, f: (i, 0))
    modspec = pl.BlockSpec((1, 6, d), lambda i, f: ((i * tm) // seq, 0, 0))
    res = pl.pallas_call(
        functools.partial(_mlp_kernel, n_x=n_x, xc=xc, n_f=n_f, tn2=tn2, alpha=alpha),
        grid=(m // tm, n_f),
        in_specs=[
            row,
            pl.BlockSpec((tm, xc), lambda i, f: (i, jnp.minimum(f, n_x - 1))),
            modspec,
            modspec,
            pl.BlockSpec((d, tf), lambda i, f: (0, f)),
            pl.BlockSpec((tf, d), lambda i, f: (f, 0)),
            pl.BlockSpec((1, d), lambda i, f: (0, 0)),
            pl.BlockSpec((1, d), lambda i, f: (0, 0)),
        ],
        out_specs=[row, row] if emit_h else row,
        out_shape=([jax.ShapeDtypeStruct((m, d), F32), jax.ShapeDtypeStruct((m, d), BF16)] if emit_h
                   else jax.ShapeDtypeStruct((m, d), F32)),
        compiler_params=_cparams(("arbitrary", "arbitrary")),
        name="mlp_ln",
    )(h, x2, mod_l, mod_next if emit_h else mod_l, w1, w2, ln_g.reshape(1, d), ln_b.reshape(1, d))
    return res if emit_h else (res, None)


CONV_HALO = 32


def _conv_kernel(av_ref, ag_ref, w_ref, cb_ref, g_ref, beta_ref, o_ref, glu_ref, y_ref, *, ts, rc):
    s = pl.program_id(1)
    ch = o_ref.shape[2]

    @pl.when(s == 0)
    def _():
        glu_ref[:, 0:CONV_HALO, :] = jnp.zeros((ch // 128, CONV_HALO, 128), F32)

    @pl.when(s > 0)
    def _():
        glu_ref[:, 0:CONV_HALO, :] = glu_ref[:, ts:ts + CONV_HALO, :]

    def glu_body(i, carry):
        r0 = pl.multiple_of(i * rc, rc)
        a = av_ref[0, pl.ds(r0, rc), :].astype(F32)
        gt = ag_ref[0, pl.ds(r0, rc), :].astype(F32)
        glu = a * jax.nn.sigmoid(gt)
        for cc in range(ch // 128):
            glu_ref[cc, pl.ds(CONV_HALO + r0, rc), :] = glu[:, cc * 128:(cc + 1) * 128]
        return carry
    lax.fori_loop(0, ts // rc, glu_body, 0)

    off = CONV_HALO - (CONV_K - 1)

    def conv_body(i, carry):
        r0 = pl.multiple_of(i * rc, rc)
        for cc in range(ch // 128):
            lanes = slice(cc * 128, (cc + 1) * 128)
            acc = jnp.zeros((rc, 128), F32)
            for j in range(CONV_K):
                acc = acc + w_ref[j:j + 1, lanes] * glu_ref[cc, pl.ds(r0 + off + j, rc), :]
            y_ref[pl.ds(r0, rc), lanes] = acc + cb_ref[0:1, lanes]
        return carry
    lax.fori_loop(0, ts // rc, conv_body, 0)

    def ln_body(i, carry):
        r0 = pl.multiple_of(i * 16, 16)
        v = y_ref[pl.ds(r0, 16), :]
        mu = jnp.mean(v, axis=-1, keepdims=True)
        dv = v - mu
        var = jnp.mean(dv * dv, axis=-1, keepdims=True)
        z = dv * lax.rsqrt(var + LN_EPS) * g_ref[...] + beta_ref[...]
        o_ref[0, pl.ds(r0, 16), :] = _silu(z).astype(o_ref.dtype)
        return carry
    lax.fori_loop(0, ts // 16, ln_body, 0)


def conv_branch(u3, conv_w, conv_b, ln_g, ln_b, mix, *, ts=512, rc=32):
    bsz, seq, _ = u3.shape
    return pl.pallas_call(
        functools.partial(_conv_kernel, ts=ts, rc=rc),
        grid=(bsz, seq // ts),
        in_specs=[
            pl.BlockSpec((1, ts, mix), lambda b, s: (b, s, 0)),
            pl.BlockSpec((1, ts, mix), lambda b, s: (b, s, 1)),
            pl.BlockSpec((CONV_K, mix), lambda b, s: (0, 0)),
            pl.BlockSpec((1, mix), lambda b, s: (0, 0)),
            pl.BlockSpec((1, mix), lambda b, s: (0, 0)),
            pl.BlockSpec((1, mix), lambda b, s: (0, 0)),
        ],
        out_specs=pl.BlockSpec((1, ts, mix), lambda b, s: (b, s, 0)),
        out_shape=jax.ShapeDtypeStruct((bsz, seq, mix), BF16),
        scratch_shapes=[pltpu.VMEM((mix // 128, ts + CONV_HALO, 128), F32), pltpu.VMEM((ts, mix), F32)],
        compiler_params=_cparams(("arbitrary", "arbitrary")),
        name="conv_branch",
    )(u3, u3, conv_w, conv_b.reshape(1, mix), ln_g.reshape(1, mix), ln_b.reshape(1, mix))


def _t5_bucket(dist):
    max_exact = NUM_BUCKETS // 2
    nf = jnp.maximum(dist, 1).astype(F32)
    large = max_exact + (jnp.log(nf / max_exact) / math.log(MAX_DISTANCE / max_exact)
                         * (NUM_BUCKETS - max_exact)).astype(jnp.int32)
    large = jnp.minimum(large, NUM_BUCKETS - 1)
    return jnp.where(dist < max_exact, dist, large)


def _bucket_tiles():
    tiles = []
    for window, dil in DSW_PATTERNS:
        steps = window // dil
        qi = jnp.arange(steps)[:, None]
        ki = jnp.arange(2 * steps)[None, :]
        step = qi + steps - ki
        bucket = _t5_bucket(jnp.clip(step, 0, steps) * dil)
        valid = (step >= 0) & (step <= steps)
        tiles.append(jnp.where(valid, bucket, -1).astype(jnp.int32))
    return jnp.stack(tiles)


def _bias_kernel(rb_ref, idx_ref, o_ref, *, heads):
    idx = idx_ref[0]
    for h in range(heads):
        acc = jnp.full(idx.shape, NEG, F32)
        for bk in range(NUM_BUCKETS):
            acc = jnp.where(idx == bk, rb_ref[bk, h], acc)
        o_ref[0, h] = acc


def attn_bias(rel_bias):
    heads = rel_bias.shape[1]
    idx = _bucket_tiles()
    g, st, st2 = idx.shape
    return pl.pallas_call(
        functools.partial(_bias_kernel, heads=heads),
        grid=(g,),
        in_specs=[
            pl.BlockSpec(memory_space=pltpu.SMEM),
            pl.BlockSpec((1, st, st2), lambda i: (i, 0, 0)),
        ],
        out_specs=pl.BlockSpec((1, heads, st, st2), lambda i: (i, 0, 0, 0)),
        out_shape=jax.ShapeDtypeStruct((g, heads, st, st2), F32),
        compiler_params=_cparams(("arbitrary",)),
        name="attn_bias",
    )(rel_bias.astype(F32), idx)


def _attn_kernel(*refs, hb, qb, has_prev, emit_lse, scale):
    if has_prev:
        q_ref, k_ref, v_ref, kp_ref, vp_ref, bias_ref, op_ref, lp_ref = refs[:8]
        rest = refs[8:]
    else:
        q_ref, k_ref, v_ref, kp_ref, vp_ref, bias_ref = refs[:6]
        op_ref = lp_ref = None
        rest = refs[6:]
    o_ref = rest[0]
    l_ref = rest[1] if emit_lse else None
    t = pl.program_id(3)
    st = HEAD_DIM
    nt = (((1,), (1,)), ((), ()))
    for h in range(hb):
        lanes = slice(h * HEAD_DIM, (h + 1) * HEAD_DIM)
        for j in range(qb):
            rows = slice(j * st, (j + 1) * st)
            q = q_ref[0, rows, lanes]
            if j == 0:
                kp, vp = kp_ref[0, :, lanes], vp_ref[0, :, lanes]
            else:
                prow = slice((j - 1) * st, j * st)
                kp, vp = k_ref[0, prow, lanes], v_ref[0, prow, lanes]
            kc, vc = k_ref[0, rows, lanes], v_ref[0, rows, lanes]
            bias_p = bias_ref[0, h, :, 0:st]
            bias_c = bias_ref[0, h, :, st:2 * st]
            if j == 0:
                bias_p = jnp.where(t == 0, NEG, bias_p)
            s_p = lax.dot_general(q, kp, nt, preferred_element_type=F32) * scale + bias_p
            s_c = lax.dot_general(q, kc, nt, preferred_element_type=F32) * scale + bias_c
            m = jnp.max(jnp.maximum(s_p, s_c), axis=-1, keepdims=True)
            p_p = jnp.exp(s_p - m)
            p_c = jnp.exp(s_c - m)
            l = jnp.sum(p_p + p_c, axis=-1, keepdims=True)
            acc = jnp.dot(p_p.astype(BF16), vp, preferred_element_type=F32)
            acc = acc + jnp.dot(p_c.astype(BF16), vc, preferred_element_type=F32)
            o = acc / l
            lse = m + jnp.log(l)
            if has_prev:
                lse0 = lp_ref[0, rows, lanes]
                o0 = op_ref[0, rows, lanes]
                mx = jnp.maximum(lse0, lse)
                w0 = jnp.exp(lse0 - mx)
                w1 = jnp.exp(lse - mx)
                den = w0 + w1
                o = (w0 * o0 + w1 * o) / den
                lse = mx + jnp.log(den)
            o_ref[0, rows, lanes] = o.astype(o_ref.dtype)
            if emit_lse:
                l_ref[0, rows, lanes] = jnp.broadcast_to(lse, (st, HEAD_DIM))


def attn_pattern(u3, bias_g, dil, prev, *, mix, last, hb=4, qb=2):
    bsz, seq, cols = u3.shape
    heads = mix // HEAD_DIM
    sub = seq // dil
    tq = min(qb * HEAD_DIM, sub)
    qb = tq // HEAD_DIM
    nt_ = sub // tq
    wb = hb * HEAD_DIM
    ub = cols // wb
    ob = mix // wb
    q0, k0, v0 = (2 * mix) // wb, (3 * mix) // wb, (4 * mix) // wb
    uv = u3.reshape(bsz, sub, dil * cols)
    pb = tq // HEAD_DIM

    def cur(c0):
        return pl.BlockSpec((1, tq, wb), lambda g, b, r, t: (b, t, r * ub + c0 + g))

    def prv(c0):
        return pl.BlockSpec((1, HEAD_DIM, wb), lambda g, b, r, t: (b, jnp.maximum(t * pb - 1, 0), r * ub + c0 + g))

    ospec = pl.BlockSpec((1, tq, wb), lambda g, b, r, t: (b, t, r * ob + g))
    in_specs = [cur(q0), cur(k0), cur(v0), prv(k0), prv(v0),
                pl.BlockSpec((1, hb, HEAD_DIM, 2 * HEAD_DIM), lambda g, b, r, t: (0, g, 0, 0))]
    args = [uv, uv, uv, uv, uv, bias_g]
    if prev is not None:
        in_specs += [ospec, ospec]
        args += [prev[0].reshape(bsz, sub, dil * mix), prev[1].reshape(bsz, sub, dil * mix)]
    oshape = jax.ShapeDtypeStruct((bsz, sub, dil * mix), BF16 if last else F32)
    if last:
        out_specs, out_shape = ospec, oshape
    else:
        out_specs = [ospec, ospec]
        out_shape = [oshape, jax.ShapeDtypeStruct((bsz, sub, dil * mix), F32)]
    res = pl.pallas_call(
        functools.partial(_attn_kernel, hb=hb, qb=qb, has_prev=prev is not None, emit_lse=not last,
                          scale=HEAD_DIM ** -0.5),
        grid=(heads // hb, bsz, dil, nt_),
        in_specs=in_specs,
        out_specs=out_specs,
        out_shape=out_shape,
        compiler_params=_cparams(("arbitrary",) * 4),
        name=f"attn_d{dil}",
    )(*args)
    if last:
        return res.reshape(bsz, seq, mix)
    return res[0].reshape(bsz, seq, mix), res[1].reshape(bsz, seq, mix)


def dilated_attention(u3, rel_bias, mix):
    bias = attn_bias(rel_bias)
    prev = None
    n = len(DSW_PATTERNS)
    for gi, (window, dil) in enumerate(DSW_PATTERNS):
        assert window // dil == HEAD_DIM
        prev = attn_pattern(u3, bias[gi:gi + 1], dil, prev, mix=mix, last=gi == n - 1)
    return prev


def _split3(v):
    hi = v.astype(BF16)
    r1 = v - hi.astype(F32)
    mid = r1.astype(BF16)
    lo = (r1 - mid.astype(F32)).astype(BF16)
    return hi, mid, lo


def _scaled(v, expo, mask):
    return jnp.where(mask, v * jnp.exp(jnp.where(mask, expo, 0.0)), 0.0)


def _hgrn_kernel(cq_ref, cf_ref, ci_ref, cg_ref, lb_ref, ng_ref, o_ref, st_ref, q_s, kk_s, b_s, *, hb, ts, layer):
    s = pl.program_id(2)
    c_len = HGRN_CHUNK
    sub = 16
    nt = (((1,), (1,)), ((), ()))
    tn_ = (((0,), (0,)), ((), ()))

    @pl.when(s == 0)
    def _():
        st_ref[...] = jnp.zeros(st_ref.shape, F32)

    row = lax.broadcasted_iota(jnp.int32, (c_len, c_len), 0)
    col = lax.broadcasted_iota(jnp.int32, (c_len, c_len), 1)
    tri = jnp.where(row >= col, 1.0, 0.0).astype(BF16)
    r64 = lax.broadcasted_iota(jnp.int32, (c_len, HEAD_DIM), 0)
    half = c_len // 2
    mask_b = (((row >= sub) & (row < half) & (col < sub))
              | ((row >= half + sub) & (col >= half) & (col < half + sub)))
    r16 = lax.broadcasted_iota(jnp.int32, (sub, HEAD_DIM), 0)
    c16 = lax.broadcasted_iota(jnp.int32, (sub, c_len), 1)
    lrow = lax.broadcasted_iota(jnp.int32, (lb_ref.shape[0], HEAD_DIM), 0)

    def chunk_body(ci, carry):
        r0 = pl.multiple_of(ci * c_len, c_len)
        rows = pl.ds(r0, c_len)
        for h in range(hb):
            lanes = slice(h * HEAD_DIM, (h + 1) * HEAD_DIM)
            lg = lb_ref[:, lanes]
            pe = jnp.exp(lg - jnp.max(lg, axis=0, keepdims=True))
            lb = jnp.sum(jnp.where((lrow >= 1) & (lrow <= layer), pe, 0.0), axis=0, keepdims=True) / jnp.sum(pe, axis=0, keepdims=True)
            q = _silu(cq_ref[0, rows, lanes].astype(F32))
            f = lb + (1.0 - lb) * jax.nn.sigmoid(cf_ref[0, rows, lanes].astype(F32))
            logf = jnp.log(f)
            kk = 1.0 - f
            v = ci_ref[0, rows, lanes]
            hi, mid, lo = _split3(logf)
            b = (jnp.dot(tri, hi, preferred_element_type=F32)
                 + jnp.dot(tri, mid, preferred_element_type=F32)
                 + jnp.dot(tri, lo, preferred_element_type=F32))
            q_s[h] = q
            kk_s[h] = kk
            b_s[h] = b
            b_last = b_s[h, c_len - 1:c_len, :]
            st_t = st_ref[h]
            inter = lax.dot_general((q * jnp.exp(b)).astype(BF16), st_t.astype(BF16), nt,
                                    preferred_element_type=F32)
            b_a = b_s[h, half - 1:half, :]
            qa = _scaled(q, b - b_a, r64 >= half)
            ka = _scaled(kk, b_a - b, r64 < half)
            attn = lax.dot_general(qa.astype(BF16), ka.astype(BF16), nt, preferred_element_type=F32)
            b_r = jnp.where(r64 < half, b_s[h, sub - 1:sub, :], b_s[h, half + sub - 1:half + sub, :])
            qsel = ((r64 >= sub) & (r64 < half)) | (r64 >= half + sub)
            ksel = (r64 < sub) | ((r64 >= half) & (r64 < half + sub))
            qbm = _scaled(q, b - b_r, qsel)
            kbm = _scaled(kk, b_r - b, ksel)
            attn_b = lax.dot_general(qbm.astype(BF16), kbm.astype(BF16), nt, preferred_element_type=F32)
            attn = attn + jnp.where(mask_b, attn_b, 0.0)
            diag_rows = []
            for jb in range(c_len // sub):
                blk = slice(jb * sub, (jb + 1) * sub)
                qt = q_s[h, blk, :]
                bt = b_s[h, blk, :]
                dblk = jnp.zeros((sub, c_len), F32)
                for si in range(sub):
                    r = jb * sub + si
                    e = jnp.exp(jnp.where(r16 >= si, bt - b_s[h, r:r + 1, :], NEG))
                    cvec = jnp.sum(qt * kk_s[h, r:r + 1, :] * e, axis=-1, keepdims=True)
                    dblk = jnp.where(c16 == r, cvec, dblk)
                diag_rows.append(dblk)
            attn = attn + jnp.concatenate(diag_rows, axis=0)
            o = inter + jnp.dot(attn.astype(BF16), v, preferred_element_type=F32)
            kd = (kk * jnp.exp(b_last - b)).astype(BF16)
            st_ref[h] = st_t * jnp.exp(b_last) + lax.dot_general(v, kd, tn_, preferred_element_type=F32)
            ms = jnp.mean(o * o, axis=-1, keepdims=True)
            o = o * lax.rsqrt(ms + LN_EPS) * ng_ref[0:1, lanes]
            o = o * _silu(cg_ref[0, rows, lanes].astype(F32))
            o_ref[0, rows, lanes] = o.astype(o_ref.dtype)
        return carry
    lax.fori_loop(0, ts // c_len, chunk_body, 0)


def hgrn_branch(u3, lb_logits, layer, norm_g, mix, *, hb=4, ts=256):
    bsz, seq, _ = u3.shape
    heads = mix // HEAD_DIM
    wb = hb * HEAD_DIM
    nb = mix // wb

    def col(k):
        return pl.BlockSpec((1, ts, wb), lambda b, g, s: (b, s, k * nb + g))

    vec = pl.BlockSpec((1, wb), lambda b, g, s: (0, g))
    return pl.pallas_call(
        functools.partial(_hgrn_kernel, hb=hb, ts=ts, layer=layer),
        grid=(bsz, heads // hb, seq // ts),
        in_specs=[col(0), col(1), col(2), col(3),
                  pl.BlockSpec((lb_logits.shape[0], wb), lambda b, g, s: (0, g)), vec],
        out_specs=pl.BlockSpec((1, ts, wb), lambda b, g, s: (b, s, g)),
        out_shape=jax.ShapeDtypeStruct((bsz, seq, mix), BF16),
        scratch_shapes=[pltpu.VMEM((hb, HEAD_DIM, HEAD_DIM), F32)]
        + [pltpu.VMEM((hb, HGRN_CHUNK, HEAD_DIM), F32)] * 3,
        compiler_params=_cparams(("arbitrary",) * 3),
        name="hgrn",
    )(u3, u3, u3, u3, lb_logits.astype(F32), norm_g.reshape(1, mix))


POOL_HALO = 16


def _pool_kernel(dp_ref, pw_ref, ps_ref, o_ref, x_ref, p_ref, *, ts, rc):
    s = pl.program_id(1)
    ch = o_ref.shape[2]
    grp = ch // len(POOL_WINDOWS)

    @pl.when(s == 0)
    def _():
        x_ref[:, 0:POOL_HALO, :] = jnp.zeros((ch // 128, POOL_HALO, 128), F32)

    @pl.when(s > 0)
    def _():
        x_ref[:, 0:POOL_HALO, :] = x_ref[:, ts:ts + POOL_HALO, :]

    def load_body(i, carry):
        r0 = pl.multiple_of(i * rc, rc)
        xv = dp_ref[0, pl.ds(r0, rc), :].astype(F32)
        for cc in range(ch // 128):
            x_ref[cc, pl.ds(POOL_HALO + r0, rc), :] = xv[:, cc * 128:(cc + 1) * 128]
        return carry
    lax.fori_loop(0, ts // rc, load_body, 0)

    def pool_body(i, carry):
        r0 = pl.multiple_of(i * rc, rc)
        pos = s * ts + r0 + lax.broadcasted_iota(jnp.int32, (rc, 1), 0)
        for cc in range(ch // 128):
            w = POOL_WINDOWS[(cc * 128) // grp]
            cur = x_ref[cc, pl.ds(POOL_HALO + r0, rc), :]
            tot = cur
            for j in range(1, w):
                tot = tot + x_ref[cc, pl.ds(POOL_HALO + r0 - j, rc), :]
            cnt = jnp.minimum(pos + 1, w).astype(F32)
            p_ref[pl.ds(r0, rc), cc * 128:(cc + 1) * 128] = (tot / cnt - cur).astype(BF16)
        return carry
    lax.fori_loop(0, ts // rc, pool_body, 0)

    for gi in range(len(POOL_WINDOWS)):
        lanes = slice(gi * grp, (gi + 1) * grp)
        y = jnp.dot(p_ref[:, lanes], pw_ref[gi], preferred_element_type=F32)
        o_ref[0, :, lanes] = (y * ps_ref[0:1, lanes]).astype(o_ref.dtype)


def pool_branch(u3, pool_w, pool_scale, mix, *, ts=512, rc=32):
    bsz, seq, _ = u3.shape
    ng, grp, _ = pool_w.shape
    return pl.pallas_call(
        functools.partial(_pool_kernel, ts=ts, rc=rc),
        grid=(bsz, seq // ts),
        in_specs=[
            pl.BlockSpec((1, ts, mix), lambda b, s: (b, s, 4)),
            pl.BlockSpec((ng, grp, grp), lambda b, s: (0, 0, 0)),
            pl.BlockSpec((1, mix), lambda b, s: (0, 0)),
        ],
        out_specs=pl.BlockSpec((1, ts, mix), lambda b, s: (b, s, 0)),
        out_shape=jax.ShapeDtypeStruct((bsz, seq, mix), BF16),
        scratch_shapes=[pltpu.VMEM((mix // 128, ts + POOL_HALO, 128), F32), pltpu.VMEM((ts, mix), BF16)],
        compiler_params=_cparams(("arbitrary", "arbitrary")),
        name="pool_branch",
    )(u3, pool_w.astype(BF16), pool_scale.reshape(1, mix))


def kernel(x, c, ada_w, ada_b, w_in, w_out, ln_g, ln_b, mlp_w1, mlp_w2, conv_w, conv_b, conv_ln_g, conv_ln_b,
           rel_bias, hgrn_lb_logits, hgrn_norm_g, pool_w, pool_scale):
    bsz, seq, d = x.shape
    depth = ada_w.shape[0]
    mix = d // 2
    alpha = (2.0 * depth) ** 0.25
    m = bsz * seq

    mod = adaln_mod(c, ada_w, ada_b)
    x2 = x.reshape(m, d)
    h = modulate(x2, mod[0], seq)
    for l in range(depth):
        u = mm_in(h, cast_layer_bf16(w_in, l))
        u3 = u.reshape(bsz, seq, 5 * mix)
        if l % 2 == 0:
            e = l // 2
            za = conv_branch(u3, conv_w[e], conv_b[e], conv_ln_g[e], conv_ln_b[e], mix)
            zb = dilated_attention(u3, rel_bias, mix)
        else:
            o = l // 2
            za = hgrn_branch(u3, hgrn_lb_logits, l, hgrn_norm_g[o], mix)
            zb = pool_branch(u3, pool_w[o], pool_scale[o], mix)
        x2, h2 = mm_out_ln(za.reshape(m, mix), zb.reshape(m, mix), cast_layer_bf16(w_out, l), x2, mod[l],
                           ln_g[l, 0], ln_b[l, 0], seq, alpha)
        x2, h = mlp_ln(h2, x2, mod[l], mod[l + 1] if l + 1 < depth else None,
                       cast_layer_bf16(mlp_w1, l), cast_layer_bf16(mlp_w2, l), ln_g[l, 1], ln_b[l, 1], seq, alpha)
    return x2.reshape(bsz, seq, d)
```

```python
import functools
import math

import jax
import jax.numpy as jnp
from jax import lax
from jax.experimental import pallas as pl
from jax.experimental.pallas import tpu as pltpu

F32 = jnp.float32
BF16 = jnp.bfloat16

HEAD_DIM = 128
CONV_K = 31
DSW_PATTERNS = ((128, 1), (512, 4), (2048, 16))
NUM_BUCKETS = 32
MAX_DISTANCE = 2048
HGRN_CHUNK = 64
POOL_WINDOWS = (2, 4, 8, 16)
LN_EPS = 1e-5
NEG = -1e30

V7X_VMEM_BYTES = 64 * 1024 * 1024
VMEM_LIMIT = 58 * 1024 * 1024


def _cparams(sem):
    return pltpu.CompilerParams(dimension_semantics=sem, vmem_limit_bytes=VMEM_LIMIT)


def _silu(v):
    return v * jax.nn.sigmoid(v)


def _mod_kernel(c_ref, w_ref, b_ref, o_ref):
    cs = _silu(c_ref[...]).astype(BF16)
    w = w_ref[0].astype(BF16)
    o_ref[0] = jnp.dot(cs, w, preferred_element_type=F32) + b_ref[0]


def adaln_mod(c, ada_w, ada_b, *, tn=512):
    nl, d, n6 = ada_w.shape
    bsz = c.shape[0]
    rows = 8
    c8 = jnp.zeros((rows, d), F32).at[:bsz].set(c)
    out = pl.pallas_call(
        _mod_kernel,
        grid=(nl, n6 // tn),
        in_specs=[
            pl.BlockSpec((rows, d), lambda l, j: (0, 0)),
            pl.BlockSpec((1, d, tn), lambda l, j: (l, 0, j)),
            pl.BlockSpec((1, 1, tn), lambda l, j: (l, 0, j)),
        ],
        out_specs=pl.BlockSpec((1, rows, tn), lambda l, j: (l, 0, j)),
        out_shape=jax.ShapeDtypeStruct((nl, rows, n6), F32),
        compiler_params=_cparams(("arbitrary", "arbitrary")),
        name="adaln_mod",
    )(c8, ada_w, ada_b.reshape(nl, 1, n6))
    return out[:, :bsz].reshape(nl, bsz, 6, d)


LN_ROWS = 16


def _ln_mod_rows(o_ref, h_ref, g_ref, b_ref, mod_ref, sh_row, sc_row):
    def body(i, carry):
        r = pl.ds(pl.multiple_of(i * LN_ROWS, LN_ROWS), LN_ROWS)
        v = o_ref[r, :]
        mu = jnp.mean(v, axis=-1, keepdims=True)
        dv = v - mu
        var = jnp.mean(dv * dv, axis=-1, keepdims=True)
        y = dv * lax.rsqrt(var + LN_EPS) * g_ref[...] + b_ref[...]
        o_ref[r, :] = y
        if h_ref is not None:
            h = y * (1.0 + mod_ref[0, sc_row:sc_row + 1, :]) + mod_ref[0, sh_row:sh_row + 1, :]
            h_ref[r, :] = h.astype(BF16)
        return carry
    lax.fori_loop(0, o_ref.shape[0] // LN_ROWS, body, 0, unroll=2)


def _cast_kernel(w_ref, o_ref):
    o_ref[...] = w_ref[0].astype(o_ref.dtype)


def cast_layer_bf16(w_stack, layer, *, tr=512, tc=2048):
    _, rows, cols = w_stack.shape
    tc = min(tc, cols)
    return pl.pallas_call(
        _cast_kernel,
        grid=(rows // tr, cols // tc),
        in_specs=[pl.BlockSpec((1, tr, tc), lambda i, j: (layer, i, j))],
        out_specs=pl.BlockSpec((tr, tc), lambda i, j: (i, j)),
        out_shape=jax.ShapeDtypeStruct((rows, cols), BF16),
        compiler_params=_cparams(("arbitrary", "arbitrary")),
        name="cast_bf16",
    )(w_stack)


def _modulate_kernel(x_ref, mod_ref, o_ref):
    o_ref[...] = (x_ref[...] * (1.0 + mod_ref[0, 1:2, :]) + mod_ref[0, 0:1, :]).astype(o_ref.dtype)


def modulate(x2, mod_l, seq, *, tm=256):
    m, d = x2.shape
    return pl.pallas_call(
        _modulate_kernel,
        grid=(m // tm,),
        in_specs=[
            pl.BlockSpec((tm, d), lambda i: (i, 0)),
            pl.BlockSpec((1, 6, d), lambda i: ((i * tm) // seq, 0, 0)),
        ],
        out_specs=pl.BlockSpec((tm, d), lambda i: (i, 0)),
        out_shape=jax.ShapeDtypeStruct((m, d), BF16),
        compiler_params=_cparams(("arbitrary",)),
        name="modulate",
    )(x2, mod_l)


def _mm_in_kernel(h_ref, w_ref, o_ref, *scratch, perm_tile):
    if perm_tile is None:
        o_ref[...] = jnp.dot(h_ref[...], w_ref[...], preferred_element_type=F32).astype(o_ref.dtype)
        return
    hp_ref, = scratch
    j = pl.program_id(1)

    @pl.when(j == 0)
    def _():
        perm = _group_perm()
        for g in range(h_ref.shape[0] // ATT_GROUP):
            rows = slice(g * ATT_GROUP, (g + 1) * ATT_GROUP)
            hp_ref[rows, :] = jnp.dot(perm, h_ref[rows, :], preferred_element_type=F32).astype(BF16)

    @pl.when(j < perm_tile)
    def _():
        o_ref[...] = jnp.dot(h_ref[...], w_ref[...], preferred_element_type=F32).astype(o_ref.dtype)

    @pl.when(j >= perm_tile)
    def _():
        o_ref[...] = jnp.dot(hp_ref[...], w_ref[...], preferred_element_type=F32).astype(o_ref.dtype)


def mm_in(h, w, *, perm_from=None, tm=1024, tn=1024):
    m, d = h.shape
    n = w.shape[1]
    perm_tile = None
    scratch = []
    if perm_from is not None:
        assert perm_from % tn == 0 and tm % ATT_GROUP == 0
        perm_tile = perm_from // tn
        scratch = [pltpu.VMEM((tm, d), BF16)]
    return pl.pallas_call(
        functools.partial(_mm_in_kernel, perm_tile=perm_tile),
        grid=(m // tm, n // tn),
        in_specs=[
            pl.BlockSpec((tm, d), lambda i, j: (i, 0)),
            pl.BlockSpec((d, tn), lambda i, j: (0, j)),
        ],
        out_specs=pl.BlockSpec((tm, tn), lambda i, j: (i, j)),
        out_shape=jax.ShapeDtypeStruct((m, n), BF16),
        scratch_shapes=scratch,
        compiler_params=_cparams(("arbitrary", "arbitrary")),
        name="mm_in",
    )(h, w)


def _mm_out_kernel(a_ref, b_ref, w_ref, x_ref, mod_ref, g_ref, beta_ref, o_ref, h_ref, *, n_n, tn, k1, alpha):
    n = pl.program_id(1)
    y = jnp.dot(a_ref[...], w_ref[0:k1, :], preferred_element_type=F32)
    y = y + jnp.dot(b_ref[...], w_ref[k1:, :], preferred_element_type=F32)
    cols = pl.ds(pl.multiple_of(n * tn, tn), tn)
    o_ref[:, cols] = alpha * x_ref[...] + mod_ref[0, 2:3, cols] * y

    @pl.when(n == n_n - 1)
    def _():
        _ln_mod_rows(o_ref, h_ref, g_ref, beta_ref, mod_ref, 3, 4)


def mm_out_ln(za, zb, w, x2, mod_l, ln_g, ln_b, seq, alpha, *, tm=512, tn=512):
    m, k1 = za.shape
    d = w.shape[1]
    n_n = d // tn
    row = pl.BlockSpec((tm, d), lambda i, n: (i, 0))
    return pl.pallas_call(
        functools.partial(_mm_out_kernel, n_n=n_n, tn=tn, k1=k1, alpha=alpha),
        grid=(m // tm, n_n),
        in_specs=[
            pl.BlockSpec((tm, k1), lambda i, n: (i, 0)),
            pl.BlockSpec((tm, zb.shape[1]), lambda i, n: (i, 0)),
            pl.BlockSpec((w.shape[0], tn), lambda i, n: (0, n)),
            pl.BlockSpec((tm, tn), lambda i, n: (i, n)),
            pl.BlockSpec((1, 6, d), lambda i, n: ((i * tm) // seq, 0, 0)),
            pl.BlockSpec((1, d), lambda i, n: (0, 0)),
            pl.BlockSpec((1, d), lambda i, n: (0, 0)),
        ],
        out_specs=[row, row],
        out_shape=[jax.ShapeDtypeStruct((m, d), F32), jax.ShapeDtypeStruct((m, d), BF16)],
        compiler_params=_cparams(("arbitrary", "arbitrary")),
        name="mm_out_ln",
    )(za, zb, w, x2, mod_l, ln_g.reshape(1, d), ln_b.reshape(1, d))


def _mlp_kernel(h_ref, x_ref, mod_ref, modn_ref, w1_ref, w2_ref, g_ref, beta_ref, o_ref, *rest,
                n_x, xc, n_f, tn2, alpha):
    ho_ref = rest[0] if rest else None
    f = pl.program_id(1)
    d = o_ref.shape[1]

    @pl.when(f == 0)
    def _():
        o_ref[...] = jnp.zeros(o_ref.shape, F32)

    t = jnp.dot(h_ref[...], w1_ref[...], preferred_element_type=F32)
    t = jnp.maximum(t, 0.0)
    t = (t * t).astype(BF16)
    for nb in range(d // tn2):
        cols = slice(nb * tn2, (nb + 1) * tn2)
        y = jnp.dot(t, w2_ref[:, cols], preferred_element_type=F32)
        o_ref[:, cols] += mod_ref[0, 5:6, cols] * y

    @pl.when(f < n_x)
    def _():
        cols = pl.ds(pl.multiple_of(f * xc, xc), xc)
        o_ref[:, cols] += alpha * x_ref[...]

    @pl.when(f == n_f - 1)
    def _():
        _ln_mod_rows(o_ref, ho_ref, g_ref, beta_ref, modn_ref, 0, 1)


def mlp_ln(h, x2, mod_l, mod_next, w1, w2, ln_g, ln_b, seq, alpha, *, tm=512, tf=512, xc=512, tn2=512):
    m, d = x2.shape
    dff = w1.shape[1]
    n_x = d // xc
    n_f = dff // tf
    assert n_f >= n_x
    emit_h = mod_next is not None
    row = pl.BlockSpec((tm, d), lambda i, f: (i, 0))
    modspec = pl.BlockSpec((1, 6, d), lambda i, f: ((i * tm) // seq, 0, 0))
    res = pl.pallas_call(
        functools.partial(_mlp_kernel, n_x=n_x, xc=xc, n_f=n_f, tn2=tn2, alpha=alpha),
        grid=(m // tm, n_f),
        in_specs=[
            row,
            pl.BlockSpec((tm, xc), lambda i, f: (i, jnp.minimum(f, n_x - 1))),
            modspec,
            modspec,
            pl.BlockSpec((d, tf), lambda i, f: (0, f)),
            pl.BlockSpec((tf, d), lambda i, f: (f, 0)),
            pl.BlockSpec((1, d), lambda i, f: (0, 0)),
            pl.BlockSpec((1, d), lambda i, f: (0, 0)),
        ],
        out_specs=[row, row] if emit_h else row,
        out_shape=([jax.ShapeDtypeStruct((m, d), F32), jax.ShapeDtypeStruct((m, d), BF16)] if emit_h
                   else jax.ShapeDtypeStruct((m, d), F32)),
        compiler_params=_cparams(("arbitrary", "arbitrary")),
        name="mlp_ln",
    )(h, x2, mod_l, mod_next if emit_h else mod_l, w1, w2, ln_g.reshape(1, d), ln_b.reshape(1, d))
    return res if emit_h else (res, None)


CONV_HALO = 32


def _conv_kernel(av_ref, ag_ref, w_ref, cb_ref, g_ref, beta_ref, o_ref, glu_ref, y_ref, *, ts, rc):
    s = pl.program_id(1)
    ch = o_ref.shape[2]

    @pl.when(s == 0)
    def _():
        glu_ref[:, 0:CONV_HALO, :] = jnp.zeros((ch // 128, CONV_HALO, 128), F32)

    @pl.when(s > 0)
    def _():
        glu_ref[:, 0:CONV_HALO, :] = glu_ref[:, ts:ts + CONV_HALO, :]

    def glu_body(i, carry):
        r0 = pl.multiple_of(i * rc, rc)
        a = av_ref[0, pl.ds(r0, rc), :].astype(F32)
        gt = ag_ref[0, pl.ds(r0, rc), :].astype(F32)
        glu = a * jax.nn.sigmoid(gt)
        for cc in range(ch // 128):
            glu_ref[cc, pl.ds(CONV_HALO + r0, rc), :] = glu[:, cc * 128:(cc + 1) * 128]
        return carry
    lax.fori_loop(0, ts // rc, glu_body, 0)

    off = CONV_HALO - (CONV_K - 1)

    def conv_body(i, carry):
        r0 = pl.multiple_of(i * rc, rc)
        for cc in range(ch // 128):
            lanes = slice(cc * 128, (cc + 1) * 128)
            acc = jnp.zeros((rc, 128), F32)
            for j in range(CONV_K):
                acc = acc + w_ref[j:j + 1, lanes] * glu_ref[cc, pl.ds(r0 + off + j, rc), :]
            y_ref[pl.ds(r0, rc), lanes] = acc + cb_ref[0:1, lanes]
        return carry
    lax.fori_loop(0, ts // rc, conv_body, 0)

    def ln_body(i, carry):
        r0 = pl.multiple_of(i * 16, 16)
        v = y_ref[pl.ds(r0, 16), :]
        mu = jnp.mean(v, axis=-1, keepdims=True)
        dv = v - mu
        var = jnp.mean(dv * dv, axis=-1, keepdims=True)
        z = dv * lax.rsqrt(var + LN_EPS) * g_ref[...] + beta_ref[...]
        o_ref[0, pl.ds(r0, 16), :] = _silu(z).astype(o_ref.dtype)
        return carry
    lax.fori_loop(0, ts // 16, ln_body, 0)


def conv_branch(u3, conv_w, conv_b, ln_g, ln_b, mix, *, ts=512, rc=32):
    bsz, seq, _ = u3.shape
    return pl.pallas_call(
        functools.partial(_conv_kernel, ts=ts, rc=rc),
        grid=(bsz, seq // ts),
        in_specs=[
            pl.BlockSpec((1, ts, mix), lambda b, s: (b, s, 0)),
            pl.BlockSpec((1, ts, mix), lambda b, s: (b, s, 1)),
            pl.BlockSpec((CONV_K, mix), lambda b, s: (0, 0)),
            pl.BlockSpec((1, mix), lambda b, s: (0, 0)),
            pl.BlockSpec((1, mix), lambda b, s: (0, 0)),
            pl.BlockSpec((1, mix), lambda b, s: (0, 0)),
        ],
        out_specs=pl.BlockSpec((1, ts, mix), lambda b, s: (b, s, 0)),
        out_shape=jax.ShapeDtypeStruct((bsz, seq, mix), BF16),
        scratch_shapes=[pltpu.VMEM((mix // 128, ts + CONV_HALO, 128), F32), pltpu.VMEM((ts, mix), F32)],
        compiler_params=_cparams(("arbitrary", "arbitrary")),
        name="conv_branch",
    )(u3, u3, conv_w, conv_b.reshape(1, mix), ln_g.reshape(1, mix), ln_b.reshape(1, mix))


def _t5_bucket(dist):
    max_exact = NUM_BUCKETS // 2
    nf = jnp.maximum(dist, 1).astype(F32)
    large = max_exact + (jnp.log(nf / max_exact) / math.log(MAX_DISTANCE / max_exact)
                         * (NUM_BUCKETS - max_exact)).astype(jnp.int32)
    large = jnp.minimum(large, NUM_BUCKETS - 1)
    return jnp.where(dist < max_exact, dist, large)


ATT_GROUP = 256
ATT_RES = 16


def _natural_index(dil):
    if dil == 1:
        i = jnp.arange(ATT_GROUP)
        return ATT_GROUP, ATT_RES * (i % ATT_RES) + i // ATT_RES
    if dil == 4:
        i = jnp.arange(HEAD_DIM)
        return HEAD_DIM, (i // 64) * 64 + 4 * (i % 16) + (i % 64) // 16
    assert dil == ATT_RES
    return HEAD_DIM, jnp.arange(HEAD_DIM)


def _bucket_tile(window, dil):
    steps = window // dil
    qr, nat = _natural_index(dil)
    qn = nat[:, None] + qr
    kn = jnp.concatenate([nat, nat + qr])[None, :]
    step = qn - kn
    bucket = _t5_bucket(jnp.clip(step, 0, steps) * dil)
    valid = (step >= 0) & (step <= steps)
    return jnp.where(valid, bucket, -1).astype(jnp.int32)


def _bias_kernel(rb_ref, idx_ref, o_ref, *, heads):
    idx = idx_ref[...]
    for h in range(heads):
        acc = jnp.full(idx.shape, NEG, F32)
        for bk in range(NUM_BUCKETS):
            acc = jnp.where(idx == bk, rb_ref[bk, h], acc)
        o_ref[h] = acc


def attn_bias(rel_bias, window, dil):
    heads = rel_bias.shape[1]
    idx = _bucket_tile(window, dil)
    qr, qr2 = idx.shape
    return pl.pallas_call(
        functools.partial(_bias_kernel, heads=heads),
        grid=(1,),
        in_specs=[
            pl.BlockSpec(memory_space=pltpu.SMEM),
            pl.BlockSpec((qr, qr2), lambda i: (0, 0)),
        ],
        out_specs=pl.BlockSpec((heads, qr, qr2), lambda i: (0, 0, 0)),
        out_shape=jax.ShapeDtypeStruct((heads, qr, qr2), F32),
        compiler_params=_cparams(("arbitrary",)),
        name=f"attn_bias_d{dil}",
    )(rel_bias.astype(F32), idx)


def _group_perm():
    shift = ATT_RES.bit_length() - 1
    row = lax.broadcasted_iota(jnp.int32, (ATT_GROUP, ATT_GROUP), 0)
    col = lax.broadcasted_iota(jnp.int32, (ATT_GROUP, ATT_GROUP), 1)
    src = ((row & (ATT_RES - 1)) << shift) | (row >> shift)
    return jnp.where(col == src, 1.0, 0.0).astype(BF16)


def _attn_kernel(*refs, hb, qr, nqb, has_halo, has_prev, last, scale):
    it = iter(refs)
    q_ref, k_ref, v_ref = next(it), next(it), next(it)
    kp_ref, vp_ref = (next(it), next(it)) if has_halo else (None, None)
    bias_ref = next(it)
    op_ref, lp_ref = (next(it), next(it)) if has_prev else (None, None)
    o_ref = next(it)
    l_ref = None if last else next(it)
    kbuf, vbuf, s_scr, p_scr, m_scr, d_scr, obuf, lbuf = (next(it) for _ in range(8))
    t = pl.program_id(3)
    tq = qr * nqb
    wb = hb * HEAD_DIM
    nt = (((1,), (1,)), ((), ()))
    units = [(h, j) for h in range(hb) for j in range(nqb)]

    q = q_ref[...].reshape(tq, wb)
    kbuf[qr:, :] = k_ref[...].reshape(tq, wb)
    vbuf[qr:, :] = v_ref[...].reshape(tq, wb)
    if has_halo:
        kbuf[0:qr, :] = kp_ref[...].reshape(qr, wb)
        vbuf[0:qr, :] = vp_ref[...].reshape(qr, wb)
        first = t == 0
    else:
        kbuf[0:qr, :] = jnp.zeros((qr, wb), BF16)
        vbuf[0:qr, :] = jnp.zeros((qr, wb), BF16)
        first = t >= 0
    prev_cols = lax.broadcasted_iota(jnp.int32, (qr, 2 * qr), 1) < qr

    for u, (h, j) in enumerate(units):
        lanes = slice(h * HEAD_DIM, (h + 1) * HEAD_DIM)
        s = lax.dot_general(q[j * qr:(j + 1) * qr, lanes], kbuf[j * qr:(j + 2) * qr, lanes], nt,
                            preferred_element_type=F32)
        s = s * scale + bias_ref[h]
        if j == 0:
            s = jnp.where(prev_cols & first, NEG, s)
        s_scr[u] = s

    for u in range(len(units)):
        s = s_scr[u]
        m = jnp.max(s, axis=-1, keepdims=True)
        p = jnp.exp(s - m)
        m_scr[u] = m
        d_scr[u] = jnp.sum(p, axis=-1, keepdims=True)
        p_scr[u] = p.astype(BF16)

    if has_prev:
        o_prev = op_ref[...].reshape(tq, wb)
        l_prev = lp_ref[...].reshape(tq, wb)
    for u, (h, j) in enumerate(units):
        lanes = slice(h * HEAD_DIM, (h + 1) * HEAD_DIM)
        rows = slice(j * qr, (j + 1) * qr)
        acc = jnp.dot(p_scr[u], vbuf[j * qr:(j + 2) * qr, lanes], preferred_element_type=F32)
        den = d_scr[u]
        o = acc / den
        lse = m_scr[u] + jnp.log(den)
        if has_prev:
            lse0 = l_prev[rows, lanes]
            mx = jnp.maximum(lse0, lse)
            w0 = jnp.exp(lse0 - mx)
            w1 = jnp.exp(lse - mx)
            tot = w0 + w1
            o = (w0 * o_prev[rows, lanes] + w1 * o) / tot
            lse = mx + jnp.log(tot)
        obuf[rows, lanes] = o
        if not last:
            lbuf[rows, lanes] = jnp.broadcast_to(lse, (qr, HEAD_DIM))

    if last:
        res = jnp.dot(_group_perm(), obuf[...].astype(BF16), preferred_element_type=F32)
        o_ref[...] = res.astype(o_ref.dtype).reshape(o_ref.shape)
    else:
        o_ref[...] = obuf[...].reshape(o_ref.shape)
        l_ref[...] = lbuf[...].reshape(l_ref.shape)


def attn_pattern(u3, bias, dil, prev, *, mix, last, hb=4):
    bsz, seq, _ = u3.shape
    heads = mix // HEAD_DIM
    ng = seq // ATT_GROUP
    wb = hb * HEAD_DIM
    q0, k0, v0 = (2 * mix) // wb, (3 * mix) // wb, (4 * mix) // wb
    qr = bias.shape[1]
    halo = None
    if dil == 1:
        nqb, n_res, n_t = 1, 1, ng
        view = lambda a: a.reshape(bsz, ng, ATT_GROUP, a.shape[-1])
        cur = lambda c0: pl.BlockSpec((1, 1, ATT_GROUP, wb), lambda g, b, r, t: (b, t, 0, c0 + g))
        halo = lambda c0: pl.BlockSpec((1, 1, ATT_GROUP, wb),
                                       lambda g, b, r, t: (b, jnp.maximum(t - 1, 0), 0, c0 + g))
    elif dil == 4:
        nqb, n_res, gt = 2, 4, 4
        n_t = ng // gt
        view = lambda a: a.reshape(bsz, ng, 4, 4, ATT_RES, a.shape[-1])
        cur = lambda c0: pl.BlockSpec((1, gt, 4, 1, ATT_RES, wb), lambda g, b, r, t: (b, t, 0, r, 0, c0 + g))
        halo = lambda c0: pl.BlockSpec((1, gt // 2, 4, 1, ATT_RES, wb),
                                       lambda g, b, r, t: (b, jnp.maximum(2 * t - 1, 0), 0, r, 0, c0 + g))
    else:
        assert dil == ATT_RES and ng * ATT_RES == 2 * HEAD_DIM
        nqb, n_res, n_t = 2, ATT_RES, 1
        view = lambda a: a.reshape(bsz, ng, ATT_RES, ATT_RES, a.shape[-1])
        cur = lambda c0: pl.BlockSpec((1, ng, 1, ATT_RES, wb), lambda g, b, r, t: (b, 0, r, 0, c0 + g))
    has_halo = halo is not None
    tq = qr * nqb
    uv = view(u3)
    in_specs = [cur(q0), cur(k0), cur(v0)]
    args = [uv, uv, uv]
    if has_halo:
        in_specs += [halo(k0), halo(v0)]
        args += [uv, uv]
    in_specs.append(pl.BlockSpec((hb, qr, 2 * qr), lambda g, b, r, t: (g, 0, 0)))
    args.append(bias)
    if prev is not None:
        in_specs += [cur(0), cur(0)]
        args += [view(prev[0]), view(prev[1])]
    if last:
        assert dil == 1
        out_specs = pl.BlockSpec((1, ATT_GROUP, wb), lambda g, b, r, t: (b, t, g))
        out_shape = jax.ShapeDtypeStruct((bsz, seq, mix), BF16)
    else:
        oshape = jax.eval_shape(view, jax.ShapeDtypeStruct((bsz, seq, mix), F32))
        out_specs = [cur(0), cur(0)]
        out_shape = [oshape, oshape]
    n_u = hb * nqb
    res = pl.pallas_call(
        functools.partial(_attn_kernel, hb=hb, qr=qr, nqb=nqb, has_halo=has_halo, has_prev=prev is not None,
                          last=last, scale=HEAD_DIM ** -0.5),
        grid=(heads // hb, bsz, n_res, n_t),
        in_specs=in_specs,
        out_specs=out_specs,
        out_shape=out_shape,
        scratch_shapes=[
            pltpu.VMEM((tq + qr, wb), BF16), pltpu.VMEM((tq + qr, wb), BF16),
            pltpu.VMEM((n_u, qr, 2 * qr), F32), pltpu.VMEM((n_u, qr, 2 * qr), BF16),
            pltpu.VMEM((n_u, qr, 1), F32), pltpu.VMEM((n_u, qr, 1), F32),
            pltpu.VMEM((tq, wb), F32), pltpu.VMEM((tq, wb), F32),
        ],
        compiler_params=_cparams(("arbitrary",) * 4),
        name=f"attn_d{dil}",
    )(*args)
    if last:
        return res
    return res[0].reshape(bsz, seq, mix), res[1].reshape(bsz, seq, mix)


def dilated_attention(u3, rel_bias, mix):
    order = sorted(DSW_PATTERNS, key=lambda wd: wd[1] == 1)
    prev = None
    for gi, (window, dil) in enumerate(order):
        assert window // dil == HEAD_DIM
        prev = attn_pattern(u3, attn_bias(rel_bias, window, dil), dil, prev, mix=mix, last=gi == len(order) - 1)
    return prev


def _split3(v):
    hi = v.astype(BF16)
    r1 = v - hi.astype(F32)
    mid = r1.astype(BF16)
    lo = (r1 - mid.astype(F32)).astype(BF16)
    return hi, mid, lo


def _scaled(v, expo, mask):
    return jnp.where(mask, v * jnp.exp(jnp.where(mask, expo, 0.0)), 0.0)


def _hgrn_kernel(cq_ref, cf_ref, ci_ref, cg_ref, lb_ref, ng_ref, o_ref, st_ref, q_s, kk_s, b_s, *, hb, ts, layer):
    s = pl.program_id(2)
    c_len = HGRN_CHUNK
    sub = 16
    nt = (((1,), (1,)), ((), ()))
    tn_ = (((0,), (0,)), ((), ()))

    @pl.when(s == 0)
    def _():
        st_ref[...] = jnp.zeros(st_ref.shape, F32)

    row = lax.broadcasted_iota(jnp.int32, (c_len, c_len), 0)
    col = lax.broadcasted_iota(jnp.int32, (c_len, c_len), 1)
    tri = jnp.where(row >= col, 1.0, 0.0).astype(BF16)
    r64 = lax.broadcasted_iota(jnp.int32, (c_len, HEAD_DIM), 0)
    half = c_len // 2
    mask_b = (((row >= sub) & (row < half) & (col < sub))
              | ((row >= half + sub) & (col >= half) & (col < half + sub)))
    r16 = lax.broadcasted_iota(jnp.int32, (sub, HEAD_DIM), 0)
    c16 = lax.broadcasted_iota(jnp.int32, (sub, c_len), 1)
    lrow = lax.broadcasted_iota(jnp.int32, (lb_ref.shape[0], HEAD_DIM), 0)

    def chunk_body(ci, carry):
        r0 = pl.multiple_of(ci * c_len, c_len)
        rows = pl.ds(r0, c_len)
        for h in range(hb):
            lanes = slice(h * HEAD_DIM, (h + 1) * HEAD_DIM)
            lg = lb_ref[:, lanes]
            pe = jnp.exp(lg - jnp.max(lg, axis=0, keepdims=True))
            lb = jnp.sum(jnp.where((lrow >= 1) & (lrow <= layer), pe, 0.0), axis=0, keepdims=True) / jnp.sum(pe, axis=0, keepdims=True)
            q = _silu(cq_ref[0, rows, lanes].astype(F32))
            f = lb + (1.0 - lb) * jax.nn.sigmoid(cf_ref[0, rows, lanes].astype(F32))
            logf = jnp.log(f)
            kk = 1.0 - f
            v = ci_ref[0, rows, lanes]
            hi, mid, lo = _split3(logf)
            b = (jnp.dot(tri, hi, preferred_element_type=F32)
                 + jnp.dot(tri, mid, preferred_element_type=F32)
                 + jnp.dot(tri, lo, preferred_element_type=F32))
            q_s[h] = q
            kk_s[h] = kk
            b_s[h] = b
            b_last = b_s[h, c_len - 1:c_len, :]
            st_t = st_ref[h]
            inter = lax.dot_general((q * jnp.exp(b)).astype(BF16), st_t.astype(BF16), nt,
                                    preferred_element_type=F32)
            b_a = b_s[h, half - 1:half, :]
            qa = _scaled(q, b - b_a, r64 >= half)
            ka = _scaled(kk, b_a - b, r64 < half)
            attn = lax.dot_general(qa.astype(BF16), ka.astype(BF16), nt, preferred_element_type=F32)
            b_r = jnp.where(r64 < half, b_s[h, sub - 1:sub, :], b_s[h, half + sub - 1:half + sub, :])
            qsel = ((r64 >= sub) & (r64 < half)) | (r64 >= half + sub)
            ksel = (r64 < sub) | ((r64 >= half) & (r64 < half + sub))
            qbm = _scaled(q, b - b_r, qsel)
            kbm = _scaled(kk, b_r - b, ksel)
            attn_b = lax.dot_general(qbm.astype(BF16), kbm.astype(BF16), nt, preferred_element_type=F32)
            attn = attn + jnp.where(mask_b, attn_b, 0.0)
            diag_rows = []
            for jb in range(c_len // sub):
                blk = slice(jb * sub, (jb + 1) * sub)
                qt = q_s[h, blk, :]
                bt = b_s[h, blk, :]
                dblk = jnp.zeros((sub, c_len), F32)
                for si in range(sub):
                    r = jb * sub + si
                    e = jnp.exp(jnp.where(r16 >= si, bt - b_s[h, r:r + 1, :], NEG))
                    cvec = jnp.sum(qt * kk_s[h, r:r + 1, :] * e, axis=-1, keepdims=True)
                    dblk = jnp.where(c16 == r, cvec, dblk)
                diag_rows.append(dblk)
            attn = attn + jnp.concatenate(diag_rows, axis=0)
            o = inter + jnp.dot(attn.astype(BF16), v, preferred_element_type=F32)
            kd = (kk * jnp.exp(b_last - b)).astype(BF16)
            st_ref[h] = st_t * jnp.exp(b_last) + lax.dot_general(v, kd, tn_, preferred_element_type=F32)
            ms = jnp.mean(o * o, axis=-1, keepdims=True)
            o = o * lax.rsqrt(ms + LN_EPS) * ng_ref[0:1, lanes]
            o = o * _silu(cg_ref[0, rows, lanes].astype(F32))
            o_ref[0, rows, lanes] = o.astype(o_ref.dtype)
        return carry
    lax.fori_loop(0, ts // c_len, chunk_body, 0)


def hgrn_branch(u3, lb_logits, layer, norm_g, mix, *, hb=4, ts=256):
    bsz, seq, _ = u3.shape
    heads = mix // HEAD_DIM
    wb = hb * HEAD_DIM
    nb = mix // wb

    def col(k):
        return pl.BlockSpec((1, ts, wb), lambda b, g, s: (b, s, k * nb + g))

    vec = pl.BlockSpec((1, wb), lambda b, g, s: (0, g))
    return pl.pallas_call(
        functools.partial(_hgrn_kernel, hb=hb, ts=ts, layer=layer),
        grid=(bsz, heads // hb, seq // ts),
        in_specs=[col(0), col(1), col(2), col(3),
                  pl.BlockSpec((lb_logits.shape[0], wb), lambda b, g, s: (0, g)), vec],
        out_specs=pl.BlockSpec((1, ts, wb), lambda b, g, s: (b, s, g)),
        out_shape=jax.ShapeDtypeStruct((bsz, seq, mix), BF16),
        scratch_shapes=[pltpu.VMEM((hb, HEAD_DIM, HEAD_DIM), F32)]
        + [pltpu.VMEM((hb, HGRN_CHUNK, HEAD_DIM), F32)] * 3,
        compiler_params=_cparams(("arbitrary",) * 3),
        name="hgrn",
    )(u3, u3, u3, u3, lb_logits.astype(F32), norm_g.reshape(1, mix))


POOL_HALO = 16


def _pool_kernel(dp_ref, pw_ref, ps_ref, o_ref, x_ref, p_ref, *, ts, rc):
    s = pl.program_id(1)
    ch = o_ref.shape[2]
    grp = ch // len(POOL_WINDOWS)

    @pl.when(s == 0)
    def _():
        x_ref[:, 0:POOL_HALO, :] = jnp.zeros((ch // 128, POOL_HALO, 128), F32)

    @pl.when(s > 0)
    def _():
        x_ref[:, 0:POOL_HALO, :] = x_ref[:, ts:ts + POOL_HALO, :]

    def load_body(i, carry):
        r0 = pl.multiple_of(i * rc, rc)
        xv = dp_ref[0, pl.ds(r0, rc), :].astype(F32)
        for cc in range(ch // 128):
            x_ref[cc, pl.ds(POOL_HALO + r0, rc), :] = xv[:, cc * 128:(cc + 1) * 128]
        return carry
    lax.fori_loop(0, ts // rc, load_body, 0)

    def pool_body(i, carry):
        r0 = pl.multiple_of(i * rc, rc)
        pos = s * ts + r0 + lax.broadcasted_iota(jnp.int32, (rc, 1), 0)
        for cc in range(ch // 128):
            w = POOL_WINDOWS[(cc * 128) // grp]
            cur = x_ref[cc, pl.ds(POOL_HALO + r0, rc), :]
            tot = cur
            for j in range(1, w):
                tot = tot + x_ref[cc, pl.ds(POOL_HALO + r0 - j, rc), :]
            cnt = jnp.minimum(pos + 1, w).astype(F32)
            p_ref[pl.ds(r0, rc), cc * 128:(cc + 1) * 128] = (tot / cnt - cur).astype(BF16)
        return carry
    lax.fori_loop(0, ts // rc, pool_body, 0)

    for gi in range(len(POOL_WINDOWS)):
        lanes = slice(gi * grp, (gi + 1) * grp)
        y = jnp.dot(p_ref[:, lanes], pw_ref[gi], preferred_element_type=F32)
        o_ref[0, :, lanes] = (y * ps_ref[0:1, lanes]).astype(o_ref.dtype)


def pool_branch(u3, pool_w, pool_scale, mix, *, ts=512, rc=32):
    bsz, seq, _ = u3.shape
    ng, grp, _ = pool_w.shape
    return pl.pallas_call(
        functools.partial(_pool_kernel, ts=ts, rc=rc),
        grid=(bsz, seq // ts),
        in_specs=[
            pl.BlockSpec((1, ts, mix), lambda b, s: (b, s, 4)),
            pl.BlockSpec((ng, grp, grp), lambda b, s: (0, 0, 0)),
            pl.BlockSpec((1, mix), lambda b, s: (0, 0)),
        ],
        out_specs=pl.BlockSpec((1, ts, mix), lambda b, s: (b, s, 0)),
        out_shape=jax.ShapeDtypeStruct((bsz, seq, mix), BF16),
        scratch_shapes=[pltpu.VMEM((mix // 128, ts + POOL_HALO, 128), F32), pltpu.VMEM((ts, mix), BF16)],
        compiler_params=_cparams(("arbitrary", "arbitrary")),
        name="pool_branch",
    )(u3, pool_w.astype(BF16), pool_scale.reshape(1, mix))


def kernel(x, c, ada_w, ada_b, w_in, w_out, ln_g, ln_b, mlp_w1, mlp_w2, conv_w, conv_b, conv_ln_g, conv_ln_b,
           rel_bias, hgrn_lb_logits, hgrn_norm_g, pool_w, pool_scale):
    bsz, seq, d = x.shape
    depth = ada_w.shape[0]
    mix = d // 2
    alpha = (2.0 * depth) ** 0.25
    m = bsz * seq

    mod = adaln_mod(c, ada_w, ada_b)
    x2 = x.reshape(m, d)
    h = modulate(x2, mod[0], seq)
    for l in range(depth):
        u = mm_in(h, cast_layer_bf16(w_in, l), perm_from=2 * mix if l % 2 == 0 else None)
        u3 = u.reshape(bsz, seq, 5 * mix)
        if l % 2 == 0:
            e = l // 2
            za = conv_branch(u3, conv_w[e], conv_b[e], conv_ln_g[e], conv_ln_b[e], mix)
            zb = dilated_attention(u3, rel_bias, mix)
        else:
            o = l // 2
            za = hgrn_branch(u3, hgrn_lb_logits, l, hgrn_norm_g[o], mix)
            zb = pool_branch(u3, pool_w[o], pool_scale[o], mix)
        x2, h2 = mm_out_ln(za.reshape(m, mix), zb.reshape(m, mix), cast_layer_bf16(w_out, l), x2, mod[l],
                           ln_g[l, 0], ln_b[l, 0], seq, alpha)
        x2, h = mlp_ln(h2, x2, mod[l], mod[l + 1] if l + 1 < depth else None,
                       cast_layer_bf16(mlp_w1, l), cast_layer_bf16(mlp_w2, l), ln_g[l, 1], ln_b[l, 1], seq, alpha)
    return x2.reshape(bsz, seq, d)
```

```python
import functools
import math

import jax
import jax.numpy as jnp
from jax import lax
from jax.experimental import pallas as pl
from jax.experimental.pallas import tpu as pltpu

F32 = jnp.float32
BF16 = jnp.bfloat16

HEAD_DIM = 128
CONV_K = 31
DSW_PATTERNS = ((128, 1), (512, 4), (2048, 16))
NUM_BUCKETS = 32
MAX_DISTANCE = 2048
HGRN_CHUNK = 64
POOL_WINDOWS = (2, 4, 8, 16)
LN_EPS = 1e-5
NEG = -1e30

V7X_VMEM_BYTES = 64 * 1024 * 1024
VMEM_LIMIT = 58 * 1024 * 1024


def _cparams(sem):
    return pltpu.CompilerParams(dimension_semantics=sem, vmem_limit_bytes=VMEM_LIMIT)


def _silu(v):
    return v * jax.nn.sigmoid(v)


def _mod_kernel(c_ref, w_ref, b_ref, o_ref):
    cs = _silu(c_ref[...]).astype(BF16)
    w = w_ref[0].astype(BF16)
    o_ref[0] = jnp.dot(cs, w, preferred_element_type=F32) + b_ref[0]


def adaln_mod(c, ada_w, ada_b, *, tn=512):
    nl, d, n6 = ada_w.shape
    bsz = c.shape[0]
    rows = 8
    c8 = jnp.zeros((rows, d), F32).at[:bsz].set(c)
    out = pl.pallas_call(
        _mod_kernel,
        grid=(nl, n6 // tn),
        in_specs=[
            pl.BlockSpec((rows, d), lambda l, j: (0, 0)),
            pl.BlockSpec((1, d, tn), lambda l, j: (l, 0, j)),
            pl.BlockSpec((1, 1, tn), lambda l, j: (l, 0, j)),
        ],
        out_specs=pl.BlockSpec((1, rows, tn), lambda l, j: (l, 0, j)),
        out_shape=jax.ShapeDtypeStruct((nl, rows, n6), F32),
        compiler_params=_cparams(("arbitrary", "arbitrary")),
        name="adaln_mod",
    )(c8, ada_w, ada_b.reshape(nl, 1, n6))
    return out[:, :bsz].reshape(nl, bsz, 6, d)


LN_ROWS = 16
LN_COLS = 512


def _ln_chunk(load, d, g_ref, b_ref, emit):
    col_slices = [slice(c * LN_COLS, (c + 1) * LN_COLS) for c in range(d // LN_COLS)]
    tot = load(col_slices[0])
    for cols in col_slices[1:]:
        tot = tot + load(cols)
    mu = jnp.sum(tot, axis=-1, keepdims=True) * (1.0 / d)
    sq = None
    for cols in col_slices:
        dv = load(cols) - mu
        sq = dv * dv if sq is None else sq + dv * dv
    rstd = lax.rsqrt(jnp.sum(sq, axis=-1, keepdims=True) * (1.0 / d) + LN_EPS)
    for cols in col_slices:
        emit(cols, (load(cols) - mu) * rstd * g_ref[:, cols] + b_ref[:, cols])


def _ln_mod_phase(step, rows_per_step, src_ref, xr_ref, alpha, gate_ref, g_ref, b_ref, mod_ref, sh_row, sc_row,
                  xo_ref, ho_ref):
    d = xo_ref.shape[1]
    for k in range(rows_per_step // LN_ROWS):
        src_rows = pl.ds(pl.multiple_of(step * rows_per_step + k * LN_ROWS, LN_ROWS), LN_ROWS)
        out_rows = slice(k * LN_ROWS, (k + 1) * LN_ROWS)
        if xr_ref is not None:
            for c in range(d // LN_COLS):
                cols = slice(c * LN_COLS, (c + 1) * LN_COLS)
                xo_ref[out_rows, cols] = alpha * xr_ref[out_rows, cols] + gate_ref[0, 5:6, cols] * src_ref[src_rows, cols]
            load = lambda cols, out_rows=out_rows: xo_ref[out_rows, cols]
        else:
            load = lambda cols, src_rows=src_rows: src_ref[src_rows, cols]

        def emit(cols, y, out_rows=out_rows):
            xo_ref[out_rows, cols] = y
            if ho_ref is not None:
                h = y * (1.0 + mod_ref[0, sc_row:sc_row + 1, cols]) + mod_ref[0, sh_row:sh_row + 1, cols]
                ho_ref[out_rows, cols] = h.astype(BF16)
        _ln_chunk(load, d, g_ref, b_ref, emit)


def _cast_kernel(w_ref, o_ref):
    o_ref[...] = w_ref[0].astype(o_ref.dtype)


def cast_layer_bf16(w_stack, layer, *, tr=512, tc=2048):
    _, rows, cols = w_stack.shape
    tc = min(tc, cols)
    return pl.pallas_call(
        _cast_kernel,
        grid=(rows // tr, cols // tc),
        in_specs=[pl.BlockSpec((1, tr, tc), lambda i, j: (layer, i, j))],
        out_specs=pl.BlockSpec((tr, tc), lambda i, j: (i, j)),
        out_shape=jax.ShapeDtypeStruct((rows, cols), BF16),
        compiler_params=_cparams(("arbitrary", "arbitrary")),
        name="cast_bf16",
    )(w_stack)


def _modulate_kernel(x_ref, mod_ref, o_ref):
    o_ref[...] = (x_ref[...] * (1.0 + mod_ref[0, 1:2, :]) + mod_ref[0, 0:1, :]).astype(o_ref.dtype)


def modulate(x2, mod_l, seq, *, tm=256):
    m, d = x2.shape
    return pl.pallas_call(
        _modulate_kernel,
        grid=(m // tm,),
        in_specs=[
            pl.BlockSpec((tm, d), lambda i: (i, 0)),
            pl.BlockSpec((1, 6, d), lambda i: ((i * tm) // seq, 0, 0)),
        ],
        out_specs=pl.BlockSpec((tm, d), lambda i: (i, 0)),
        out_shape=jax.ShapeDtypeStruct((m, d), BF16),
        compiler_params=_cparams(("arbitrary",)),
        name="modulate",
    )(x2, mod_l)


def _mm_in_kernel(h_ref, w_ref, o_ref, *scratch, perm_tile):
    if perm_tile is None:
        o_ref[...] = jnp.dot(h_ref[...], w_ref[...], preferred_element_type=F32).astype(o_ref.dtype)
        return
    hp_ref, = scratch
    j = pl.program_id(1)

    @pl.when(j == 0)
    def _():
        perm = _group_perm()
        for g in range(h_ref.shape[0] // ATT_GROUP):
            rows = slice(g * ATT_GROUP, (g + 1) * ATT_GROUP)
            hp_ref[rows, :] = jnp.dot(perm, h_ref[rows, :], preferred_element_type=F32).astype(BF16)

    @pl.when(j < perm_tile)
    def _():
        o_ref[...] = jnp.dot(h_ref[...], w_ref[...], preferred_element_type=F32).astype(o_ref.dtype)

    @pl.when(j >= perm_tile)
    def _():
        o_ref[...] = jnp.dot(hp_ref[...], w_ref[...], preferred_element_type=F32).astype(o_ref.dtype)


def mm_in(h, w, *, perm_from=None, tm=1024, tn=1024):
    m, d = h.shape
    n = w.shape[1]
    perm_tile = None
    scratch = []
    if perm_from is not None:
        assert perm_from % tn == 0 and tm % ATT_GROUP == 0
        perm_tile = perm_from // tn
        scratch = [pltpu.VMEM((tm, d), BF16)]
    return pl.pallas_call(
        functools.partial(_mm_in_kernel, perm_tile=perm_tile),
        grid=(m // tm, n // tn),
        in_specs=[
            pl.BlockSpec((tm, d), lambda i, j: (i, 0)),
            pl.BlockSpec((d, tn), lambda i, j: (0, j)),
        ],
        out_specs=pl.BlockSpec((tm, tn), lambda i, j: (i, j)),
        out_shape=jax.ShapeDtypeStruct((m, n), BF16),
        scratch_shapes=scratch,
        compiler_params=_cparams(("arbitrary", "arbitrary")),
        name="mm_in",
    )(h, w)


def _mm_out_kernel(a_ref, b_ref, w_ref, x_ref, mod_ref, modp_ref, g_ref, beta_ref, xo_ref, ho_ref, acc_ref, ln_ref,
                   *, n_t, n_n, tn, k1, alpha):
    i = pl.program_id(0)
    n = pl.program_id(1)
    rows_per_step = acc_ref.shape[0] // n_n

    @pl.when((i == 0) & (n == 0))
    def _():
        ln_ref[...] = jnp.zeros(ln_ref.shape, F32)

    def ln_phase():
        _ln_mod_phase(n, rows_per_step, ln_ref, None, alpha, None, g_ref, beta_ref, modp_ref, 3, 4, xo_ref, ho_ref)

    @pl.when(i < n_t)
    def _():
        y = jnp.dot(a_ref[...], w_ref[0:k1, :], preferred_element_type=F32)
        y = y + jnp.dot(b_ref[...], w_ref[k1:, :], preferred_element_type=F32)
        cols = pl.ds(pl.multiple_of(n * tn, tn), tn)
        acc_ref[:, cols] = alpha * x_ref[...] + mod_ref[0, 2:3, cols] * y
        ln_phase()

    @pl.when(i == n_t)
    def _():
        ln_phase()

    @pl.when((i < n_t) & (n == n_n - 1))
    def _():
        ln_ref[...] = acc_ref[...]


def mm_out_ln(za, zb, w, x2, mod_l, ln_g, ln_b, seq, alpha, *, tm=512, tn=512):
    m, k1 = za.shape
    d = w.shape[1]
    n_n = d // tn
    n_t = m // tm
    rl = tm // n_n
    assert rl % LN_ROWS == 0
    cur = lambda i: jnp.minimum(i, n_t - 1)
    prv = lambda i: jnp.maximum(i - 1, 0)
    col = lambda i, n: jnp.where(i < n_t, n, n_n - 1)
    out_row = pl.BlockSpec((rl, d), lambda i, n: (jnp.where(i == 0, 0, (i - 1) * n_n + n), 0))
    return pl.pallas_call(
        functools.partial(_mm_out_kernel, n_t=n_t, n_n=n_n, tn=tn, k1=k1, alpha=alpha),
        grid=(n_t + 1, n_n),
        in_specs=[
            pl.BlockSpec((tm, k1), lambda i, n: (cur(i), 0)),
            pl.BlockSpec((tm, zb.shape[1]), lambda i, n: (cur(i), 0)),
            pl.BlockSpec((w.shape[0], tn), lambda i, n: (0, col(i, n))),
            pl.BlockSpec((tm, tn), lambda i, n: (cur(i), col(i, n))),
            pl.BlockSpec((1, 6, d), lambda i, n: ((cur(i) * tm) // seq, 0, 0)),
            pl.BlockSpec((1, 6, d), lambda i, n: ((prv(i) * tm) // seq, 0, 0)),
            pl.BlockSpec((1, d), lambda i, n: (0, 0)),
            pl.BlockSpec((1, d), lambda i, n: (0, 0)),
        ],
        out_specs=[out_row, out_row],
        out_shape=[jax.ShapeDtypeStruct((m, d), F32), jax.ShapeDtypeStruct((m, d), BF16)],
        scratch_shapes=[pltpu.VMEM((tm, d), F32), pltpu.VMEM((tm, d), F32)],
        compiler_params=_cparams(("arbitrary", "arbitrary")),
        name="mm_out_ln",
    )(za, zb, w, x2, mod_l, mod_l, ln_g.reshape(1, d), ln_b.reshape(1, d))


def _mlp_kernel(h_ref, xr_ref, modp_ref, modn_ref, w1_ref, w2_ref, g_ref, beta_ref, *rest,
                n_t, n_f, tn2, alpha, emit_h):
    if emit_h:
        xo_ref, ho_ref, acc_ref, ln_ref = rest
    else:
        xo_ref, acc_ref, ln_ref = rest
        ho_ref = None
    i = pl.program_id(0)
    f = pl.program_id(1)
    d = acc_ref.shape[1]
    rows_per_step = acc_ref.shape[0] // n_f

    @pl.when((i == 0) & (f == 0))
    def _():
        ln_ref[...] = jnp.zeros(ln_ref.shape, F32)

    @pl.when((i < n_t) & (f == 0))
    def _():
        acc_ref[...] = jnp.zeros(acc_ref.shape, F32)

    def ln_phase():
        _ln_mod_phase(f, rows_per_step, ln_ref, xr_ref, alpha, modp_ref, g_ref, beta_ref, modn_ref, 0, 1,
                      xo_ref, ho_ref)

    @pl.when(i < n_t)
    def _():
        t = jnp.dot(h_ref[...], w1_ref[...], preferred_element_type=F32)
        t = jnp.maximum(t, 0.0)
        t = (t * t).astype(BF16)
        for nb in range(d // tn2):
            cols = slice(nb * tn2, (nb + 1) * tn2)
            acc_ref[:, cols] += jnp.dot(t, w2_ref[:, cols], preferred_element_type=F32)
        ln_phase()

    @pl.when(i == n_t)
    def _():
        ln_phase()

    @pl.when((i < n_t) & (f == n_f - 1))
    def _():
        ln_ref[...] = acc_ref[...]


def mlp_ln(h, x2, mod_l, mod_next, w1, w2, ln_g, ln_b, seq, alpha, *, tm=512, tf=512, tn2=512):
    m, d = x2.shape
    dff = w1.shape[1]
    n_f = dff // tf
    n_t = m // tm
    rl = tm // n_f
    assert rl % LN_ROWS == 0
    emit_h = mod_next is not None
    cur = lambda i: jnp.minimum(i, n_t - 1)
    prv = lambda i: jnp.maximum(i - 1, 0)
    chunk = lambda i, f: jnp.where(i < n_t, f, n_f - 1)
    out_row = pl.BlockSpec((rl, d), lambda i, f: (jnp.where(i == 0, 0, (i - 1) * n_f + f), 0))
    modspec = pl.BlockSpec((1, 6, d), lambda i, f: ((prv(i) * tm) // seq, 0, 0))
    f32_out = jax.ShapeDtypeStruct((m, d), F32)
    res = pl.pallas_call(
        functools.partial(_mlp_kernel, n_t=n_t, n_f=n_f, tn2=tn2, alpha=alpha, emit_h=emit_h),
        grid=(n_t + 1, n_f),
        in_specs=[
            pl.BlockSpec((tm, d), lambda i, f: (cur(i), 0)),
            pl.BlockSpec((rl, d), lambda i, f: (prv(i) * n_f + f, 0)),
            modspec,
            modspec,
            pl.BlockSpec((d, tf), lambda i, f: (0, chunk(i, f))),
            pl.BlockSpec((tf, d), lambda i, f: (chunk(i, f), 0)),
            pl.BlockSpec((1, d), lambda i, f: (0, 0)),
            pl.BlockSpec((1, d), lambda i, f: (0, 0)),
        ],
        out_specs=[out_row, out_row] if emit_h else out_row,
        out_shape=[f32_out, jax.ShapeDtypeStruct((m, d), BF16)] if emit_h else f32_out,
        scratch_shapes=[pltpu.VMEM((tm, d), F32), pltpu.VMEM((tm, d), F32)],
        compiler_params=_cparams(("arbitrary", "arbitrary")),
        name="mlp_ln",
    )(h, x2, mod_l, mod_next if emit_h else mod_l, w1, w2, ln_g.reshape(1, d), ln_b.reshape(1, d))
    return res if emit_h else (res, None)


CONV_HALO = 32


def _conv_kernel(av_ref, ag_ref, w_ref, cb_ref, g_ref, beta_ref, o_ref, glu_ref, y_ref, *, ts, rc):
    s = pl.program_id(1)
    ch = o_ref.shape[2]

    @pl.when(s == 0)
    def _():
        glu_ref[:, 0:CONV_HALO, :] = jnp.zeros((ch // 128, CONV_HALO, 128), F32)

    @pl.when(s > 0)
    def _():
        glu_ref[:, 0:CONV_HALO, :] = glu_ref[:, ts:ts + CONV_HALO, :]

    def glu_body(i, carry):
        r0 = pl.multiple_of(i * rc, rc)
        a = av_ref[0, pl.ds(r0, rc), :].astype(F32)
        gt = ag_ref[0, pl.ds(r0, rc), :].astype(F32)
        glu = a * jax.nn.sigmoid(gt)
        for cc in range(ch // 128):
            glu_ref[cc, pl.ds(CONV_HALO + r0, rc), :] = glu[:, cc * 128:(cc + 1) * 128]
        return carry
    lax.fori_loop(0, ts // rc, glu_body, 0)

    off = CONV_HALO - (CONV_K - 1)

    def conv_body(i, carry):
        r0 = pl.multiple_of(i * rc, rc)
        for cc in range(ch // 128):
            lanes = slice(cc * 128, (cc + 1) * 128)
            acc = jnp.zeros((rc, 128), F32)
            for j in range(CONV_K):
                acc = acc + w_ref[j:j + 1, lanes] * glu_ref[cc, pl.ds(r0 + off + j, rc), :]
            y_ref[pl.ds(r0, rc), lanes] = acc + cb_ref[0:1, lanes]
        return carry
    lax.fori_loop(0, ts // rc, conv_body, 0)

    def ln_body(i, carry):
        rows = pl.ds(pl.multiple_of(i * LN_ROWS, LN_ROWS), LN_ROWS)

        def emit(cols, z):
            o_ref[0, rows, cols] = _silu(z).astype(o_ref.dtype)
        _ln_chunk(lambda cols: y_ref[rows, cols], ch, g_ref, beta_ref, emit)
        return carry
    lax.fori_loop(0, ts // LN_ROWS, ln_body, 0, unroll=2)


def conv_branch(u3, conv_w, conv_b, ln_g, ln_b, mix, *, ts=512, rc=32):
    bsz, seq, _ = u3.shape
    return pl.pallas_call(
        functools.partial(_conv_kernel, ts=ts, rc=rc),
        grid=(bsz, seq // ts),
        in_specs=[
            pl.BlockSpec((1, ts, mix), lambda b, s: (b, s, 0)),
            pl.BlockSpec((1, ts, mix), lambda b, s: (b, s, 1)),
            pl.BlockSpec((CONV_K, mix), lambda b, s: (0, 0)),
            pl.BlockSpec((1, mix), lambda b, s: (0, 0)),
            pl.BlockSpec((1, mix), lambda b, s: (0, 0)),
            pl.BlockSpec((1, mix), lambda b, s: (0, 0)),
        ],
        out_specs=pl.BlockSpec((1, ts, mix), lambda b, s: (b, s, 0)),
        out_shape=jax.ShapeDtypeStruct((bsz, seq, mix), BF16),
        scratch_shapes=[pltpu.VMEM((mix // 128, ts + CONV_HALO, 128), F32), pltpu.VMEM((ts, mix), F32)],
        compiler_params=_cparams(("arbitrary", "arbitrary")),
        name="conv_branch",
    )(u3, u3, conv_w, conv_b.reshape(1, mix), ln_g.reshape(1, mix), ln_b.reshape(1, mix))


def _t5_bucket(dist):
    max_exact = NUM_BUCKETS // 2
    nf = jnp.maximum(dist, 1).astype(F32)
    large = max_exact + (jnp.log(nf / max_exact) / math.log(MAX_DISTANCE / max_exact)
                         * (NUM_BUCKETS - max_exact)).astype(jnp.int32)
    large = jnp.minimum(large, NUM_BUCKETS - 1)
    return jnp.where(dist < max_exact, dist, large)


ATT_GROUP = 256
ATT_RES = 16


def _natural_index(dil):
    if dil == 1:
        i = jnp.arange(ATT_GROUP)
        return ATT_GROUP, ATT_RES * (i % ATT_RES) + i // ATT_RES
    if dil == 4:
        i = jnp.arange(HEAD_DIM)
        return HEAD_DIM, (i // 64) * 64 + 4 * (i % 16) + (i % 64) // 16
    assert dil == ATT_RES
    return HEAD_DIM, jnp.arange(HEAD_DIM)


def _bucket_tile(window, dil):
    steps = window // dil
    qr, nat = _natural_index(dil)
    qn = nat[:, None] + qr
    kn = jnp.concatenate([nat, nat + qr])[None, :]
    step = qn - kn
    bucket = _t5_bucket(jnp.clip(step, 0, steps) * dil)
    valid = (step >= 0) & (step <= steps)
    return jnp.where(valid, bucket, -1).astype(jnp.int32)


def _bias_kernel(rb_ref, idx_ref, o_ref, *, heads):
    idx = idx_ref[...]
    for h in range(heads):
        acc = jnp.full(idx.shape, NEG, F32)
        for bk in range(NUM_BUCKETS):
            acc = jnp.where(idx == bk, rb_ref[bk, h], acc)
        o_ref[h] = acc


def attn_bias(rel_bias, window, dil):
    heads = rel_bias.shape[1]
    idx = _bucket_tile(window, dil)
    qr, qr2 = idx.shape
    return pl.pallas_call(
        functools.partial(_bias_kernel, heads=heads),
        grid=(1,),
        in_specs=[
            pl.BlockSpec(memory_space=pltpu.SMEM),
            pl.BlockSpec((qr, qr2), lambda i: (0, 0)),
        ],
        out_specs=pl.BlockSpec((heads, qr, qr2), lambda i: (0, 0, 0)),
        out_shape=jax.ShapeDtypeStruct((heads, qr, qr2), F32),
        compiler_params=_cparams(("arbitrary",)),
        name=f"attn_bias_d{dil}",
    )(rel_bias.astype(F32), idx)


def _group_perm():
    shift = ATT_RES.bit_length() - 1
    row = lax.broadcasted_iota(jnp.int32, (ATT_GROUP, ATT_GROUP), 0)
    col = lax.broadcasted_iota(jnp.int32, (ATT_GROUP, ATT_GROUP), 1)
    src = ((row & (ATT_RES - 1)) << shift) | (row >> shift)
    return jnp.where(col == src, 1.0, 0.0).astype(BF16)


def _attn_kernel(*refs, hb, qr, nqb, has_halo, has_prev, last, scale):
    it = iter(refs)
    q_ref, k_ref, v_ref = next(it), next(it), next(it)
    kp_ref, vp_ref = (next(it), next(it)) if has_halo else (None, None)
    bias_ref = next(it)
    op_ref, lp_ref = (next(it), next(it)) if has_prev else (None, None)
    o_ref = next(it)
    l_ref = None if last else next(it)
    kbuf, vbuf, s_scr, p_scr, m_scr, d_scr, obuf, lbuf = (next(it) for _ in range(8))
    t = pl.program_id(3)
    tq = qr * nqb
    wb = hb * HEAD_DIM
    nt = (((1,), (1,)), ((), ()))
    units = [(h, j) for h in range(hb) for j in range(nqb)]

    q = q_ref[...].reshape(tq, wb)
    kbuf[qr:, :] = k_ref[...].reshape(tq, wb)
    vbuf[qr:, :] = v_ref[...].reshape(tq, wb)
    if has_halo:
        kbuf[0:qr, :] = kp_ref[...].reshape(qr, wb)
        vbuf[0:qr, :] = vp_ref[...].reshape(qr, wb)
        first = t == 0
    else:
        kbuf[0:qr, :] = jnp.zeros((qr, wb), BF16)
        vbuf[0:qr, :] = jnp.zeros((qr, wb), BF16)
        first = t >= 0
    prev_cols = lax.broadcasted_iota(jnp.int32, (qr, 2 * qr), 1) < qr

    for u, (h, j) in enumerate(units):
        lanes = slice(h * HEAD_DIM, (h + 1) * HEAD_DIM)
        s = lax.dot_general(q[j * qr:(j + 1) * qr, lanes], kbuf[j * qr:(j + 2) * qr, lanes], nt,
                            preferred_element_type=F32)
        s = s * scale + bias_ref[h]
        if j == 0:
            s = jnp.where(prev_cols & first, NEG, s)
        s_scr[u] = s

    for u in range(len(units)):
        s = s_scr[u]
        m = jnp.max(s, axis=-1, keepdims=True)
        p = jnp.exp(s - m)
        m_scr[u] = m
        d_scr[u] = jnp.sum(p, axis=-1, keepdims=True)
        p_scr[u] = p.astype(BF16)

    if has_prev:
        o_prev = op_ref[...].reshape(tq, wb)
        l_prev = lp_ref[...].reshape(tq, wb)
    for u, (h, j) in enumerate(units):
        lanes = slice(h * HEAD_DIM, (h + 1) * HEAD_DIM)
        rows = slice(j * qr, (j + 1) * qr)
        acc = jnp.dot(p_scr[u], vbuf[j * qr:(j + 2) * qr, lanes], preferred_element_type=F32)
        den = d_scr[u]
        o = acc / den
        lse = m_scr[u] + jnp.log(den)
        if has_prev:
            lse0 = l_prev[rows, lanes]
            mx = jnp.maximum(lse0, lse)
            w0 = jnp.exp(lse0 - mx)
            w1 = jnp.exp(lse - mx)
            tot = w0 + w1
            o = (w0 * o_prev[rows, lanes] + w1 * o) / tot
            lse = mx + jnp.log(tot)
        obuf[rows, lanes] = o
        if not last:
            lbuf[rows, lanes] = jnp.broadcast_to(lse, (qr, HEAD_DIM))

    if last:
        res = jnp.dot(_group_perm(), obuf[...].astype(BF16), preferred_element_type=F32)
        o_ref[...] = res.astype(o_ref.dtype).reshape(o_ref.shape)
    else:
        o_ref[...] = obuf[...].reshape(o_ref.shape)
        l_ref[...] = lbuf[...].reshape(l_ref.shape)


def attn_pattern(u3, bias, dil, prev, *, mix, last, hb=4):
    bsz, seq, _ = u3.shape
    heads = mix // HEAD_DIM
    ng = seq // ATT_GROUP
    wb = hb * HEAD_DIM
    q0, k0, v0 = (2 * mix) // wb, (3 * mix) // wb, (4 * mix) // wb
    qr = bias.shape[1]
    halo = None
    if dil == 1:
        nqb, n_res, n_t = 1, 1, ng
        view = lambda a: a.reshape(bsz, ng, ATT_GROUP, a.shape[-1])
        cur = lambda c0: pl.BlockSpec((1, 1, ATT_GROUP, wb), lambda g, b, r, t: (b, t, 0, c0 + g))
        halo = lambda c0: pl.BlockSpec((1, 1, ATT_GROUP, wb),
                                       lambda g, b, r, t: (b, jnp.maximum(t - 1, 0), 0, c0 + g))
    elif dil == 4:
        nqb, n_res, gt = 2, 4, 4
        n_t = ng // gt
        view = lambda a: a.reshape(bsz, ng, 4, 4, ATT_RES, a.shape[-1])
        cur = lambda c0: pl.BlockSpec((1, gt, 4, 1, ATT_RES, wb), lambda g, b, r, t: (b, t, 0, r, 0, c0 + g))
        halo = lambda c0: pl.BlockSpec((1, gt // 2, 4, 1, ATT_RES, wb),
                                       lambda g, b, r, t: (b, jnp.maximum(2 * t - 1, 0), 0, r, 0, c0 + g))
    else:
        assert dil == ATT_RES and ng * ATT_RES == 2 * HEAD_DIM
        nqb, n_res, n_t = 2, ATT_RES, 1
        view = lambda a: a.reshape(bsz, ng, ATT_RES, ATT_RES, a.shape[-1])
        cur = lambda c0: pl.BlockSpec((1, ng, 1, ATT_RES, wb), lambda g, b, r, t: (b, 0, r, 0, c0 + g))
    has_halo = halo is not None
    tq = qr * nqb
    uv = view(u3)
    in_specs = [cur(q0), cur(k0), cur(v0)]
    args = [uv, uv, uv]
    if has_halo:
        in_specs += [halo(k0), halo(v0)]
        args += [uv, uv]
    in_specs.append(pl.BlockSpec((hb, qr, 2 * qr), lambda g, b, r, t: (g, 0, 0)))
    args.append(bias)
    if prev is not None:
        in_specs += [cur(0), cur(0)]
        args += [view(prev[0]), view(prev[1])]
    if last:
        assert dil == 1
        out_specs = pl.BlockSpec((1, ATT_GROUP, wb), lambda g, b, r, t: (b, t, g))
        out_shape = jax.ShapeDtypeStruct((bsz, seq, mix), BF16)
    else:
        oshape = jax.eval_shape(view, jax.ShapeDtypeStruct((bsz, seq, mix), F32))
        out_specs = [cur(0), cur(0)]
        out_shape = [oshape, oshape]
    n_u = hb * nqb
    res = pl.pallas_call(
        functools.partial(_attn_kernel, hb=hb, qr=qr, nqb=nqb, has_halo=has_halo, has_prev=prev is not None,
                          last=last, scale=HEAD_DIM ** -0.5),
        grid=(heads // hb, bsz, n_res, n_t),
        in_specs=in_specs,
        out_specs=out_specs,
        out_shape=out_shape,
        scratch_shapes=[
            pltpu.VMEM((tq + qr, wb), BF16), pltpu.VMEM((tq + qr, wb), BF16),
            pltpu.VMEM((n_u, qr, 2 * qr), F32), pltpu.VMEM((n_u, qr, 2 * qr), BF16),
            pltpu.VMEM((n_u, qr, 1), F32), pltpu.VMEM((n_u, qr, 1), F32),
            pltpu.VMEM((tq, wb), F32), pltpu.VMEM((tq, wb), F32),
        ],
        compiler_params=_cparams(("arbitrary",) * 4),
        name=f"attn_d{dil}",
    )(*args)
    if last:
        return res
    return res[0].reshape(bsz, seq, mix), res[1].reshape(bsz, seq, mix)


def dilated_attention(u3, rel_bias, mix):
    order = sorted(DSW_PATTERNS, key=lambda wd: wd[1] == 1)
    prev = None
    for gi, (window, dil) in enumerate(order):
        assert window // dil == HEAD_DIM
        prev = attn_pattern(u3, attn_bias(rel_bias, window, dil), dil, prev, mix=mix, last=gi == len(order) - 1)
    return prev


def _split3(v):
    hi = v.astype(BF16)
    r1 = v - hi.astype(F32)
    mid = r1.astype(BF16)
    lo = (r1 - mid.astype(F32)).astype(BF16)
    return hi, mid, lo


def _scaled(v, expo, mask):
    return jnp.where(mask, v * jnp.exp(jnp.where(mask, expo, 0.0)), 0.0)


def _hgrn_kernel(cq_ref, cf_ref, ci_ref, cg_ref, lb_ref, ng_ref, o_ref, st_ref, q_s, kk_s, b_s, *, hb, ts, layer):
    s = pl.program_id(2)
    c_len = HGRN_CHUNK
    sub = 16
    nt = (((1,), (1,)), ((), ()))
    tn_ = (((0,), (0,)), ((), ()))

    @pl.when(s == 0)
    def _():
        st_ref[...] = jnp.zeros(st_ref.shape, F32)

    row = lax.broadcasted_iota(jnp.int32, (c_len, c_len), 0)
    col = lax.broadcasted_iota(jnp.int32, (c_len, c_len), 1)
    tri = jnp.where(row >= col, 1.0, 0.0).astype(BF16)
    r64 = lax.broadcasted_iota(jnp.int32, (c_len, HEAD_DIM), 0)
    half = c_len // 2
    mask_b = (((row >= sub) & (row < half) & (col < sub))
              | ((row >= half + sub) & (col >= half) & (col < half + sub)))
    r16 = lax.broadcasted_iota(jnp.int32, (sub, HEAD_DIM), 0)
    c16 = lax.broadcasted_iota(jnp.int32, (sub, c_len), 1)
    lrow = lax.broadcasted_iota(jnp.int32, (lb_ref.shape[0], HEAD_DIM), 0)

    def chunk_body(ci, carry):
        r0 = pl.multiple_of(ci * c_len, c_len)
        rows = pl.ds(r0, c_len)
        for h in range(hb):
            lanes = slice(h * HEAD_DIM, (h + 1) * HEAD_DIM)
            lg = lb_ref[:, lanes]
            pe = jnp.exp(lg - jnp.max(lg, axis=0, keepdims=True))
            lb = jnp.sum(jnp.where((lrow >= 1) & (lrow <= layer), pe, 0.0), axis=0, keepdims=True) / jnp.sum(pe, axis=0, keepdims=True)
            q = _silu(cq_ref[0, rows, lanes].astype(F32))
            f = lb + (1.0 - lb) * jax.nn.sigmoid(cf_ref[0, rows, lanes].astype(F32))
            logf = jnp.log(f)
            kk = 1.0 - f
            v = ci_ref[0, rows, lanes]
            hi, mid, lo = _split3(logf)
            b = (jnp.dot(tri, hi, preferred_element_type=F32)
                 + jnp.dot(tri, mid, preferred_element_type=F32)
                 + jnp.dot(tri, lo, preferred_element_type=F32))
            q_s[h] = q
            kk_s[h] = kk
            b_s[h] = b
            b_last = b_s[h, c_len - 1:c_len, :]
            st_t = st_ref[h]
            inter = lax.dot_general((q * jnp.exp(b)).astype(BF16), st_t.astype(BF16), nt,
                                    preferred_element_type=F32)
            b_a = b_s[h, half - 1:half, :]
            qa = _scaled(q, b - b_a, r64 >= half)
            ka = _scaled(kk, b_a - b, r64 < half)
            attn = lax.dot_general(qa.astype(BF16), ka.astype(BF16), nt, preferred_element_type=F32)
            b_r = jnp.where(r64 < half, b_s[h, sub - 1:sub, :], b_s[h, half + sub - 1:half + sub, :])
            qsel = ((r64 >= sub) & (r64 < half)) | (r64 >= half + sub)
            ksel = (r64 < sub) | ((r64 >= half) & (r64 < half + sub))
            qbm = _scaled(q, b - b_r, qsel)
            kbm = _scaled(kk, b_r - b, ksel)
            attn_b = lax.dot_general(qbm.astype(BF16), kbm.astype(BF16), nt, preferred_element_type=F32)
            attn = attn + jnp.where(mask_b, attn_b, 0.0)
            diag_rows = []
            for jb in range(c_len // sub):
                blk = slice(jb * sub, (jb + 1) * sub)
                qt = q_s[h, blk, :]
                bt = b_s[h, blk, :]
                dblk = jnp.zeros((sub, c_len), F32)
                for si in range(sub):
                    r = jb * sub + si
                    e = jnp.exp(jnp.where(r16 >= si, bt - b_s[h, r:r + 1, :], NEG))
                    cvec = jnp.sum(qt * kk_s[h, r:r + 1, :] * e, axis=-1, keepdims=True)
                    dblk = jnp.where(c16 == r, cvec, dblk)
                diag_rows.append(dblk)
            attn = attn + jnp.concatenate(diag_rows, axis=0)
            o = inter + jnp.dot(attn.astype(BF16), v, preferred_element_type=F32)
            kd = (kk * jnp.exp(b_last - b)).astype(BF16)
            st_ref[h] = st_t * jnp.exp(b_last) + lax.dot_general(v, kd, tn_, preferred_element_type=F32)
            ms = jnp.mean(o * o, axis=-1, keepdims=True)
            o = o * lax.rsqrt(ms + LN_EPS) * ng_ref[0:1, lanes]
            o = o * _silu(cg_ref[0, rows, lanes].astype(F32))
            o_ref[0, rows, lanes] = o.astype(o_ref.dtype)
        return carry
    lax.fori_loop(0, ts // c_len, chunk_body, 0)


def hgrn_branch(u3, lb_logits, layer, norm_g, mix, *, hb=4, ts=256):
    bsz, seq, _ = u3.shape
    heads = mix // HEAD_DIM
    wb = hb * HEAD_DIM
    nb = mix // wb

    def col(k):
        return pl.BlockSpec((1, ts, wb), lambda b, g, s: (b, s, k * nb + g))

    vec = pl.BlockSpec((1, wb), lambda b, g, s: (0, g))
    return pl.pallas_call(
        functools.partial(_hgrn_kernel, hb=hb, ts=ts, layer=layer),
        grid=(bsz, heads // hb, seq // ts),
        in_specs=[col(0), col(1), col(2), col(3),
                  pl.BlockSpec((lb_logits.shape[0], wb), lambda b, g, s: (0, g)), vec],
        out_specs=pl.BlockSpec((1, ts, wb), lambda b, g, s: (b, s, g)),
        out_shape=jax.ShapeDtypeStruct((bsz, seq, mix), BF16),
        scratch_shapes=[pltpu.VMEM((hb, HEAD_DIM, HEAD_DIM), F32)]
        + [pltpu.VMEM((hb, HGRN_CHUNK, HEAD_DIM), F32)] * 3,
        compiler_params=_cparams(("arbitrary",) * 3),
        name="hgrn",
    )(u3, u3, u3, u3, lb_logits.astype(F32), norm_g.reshape(1, mix))


POOL_HALO = 16


def _pool_kernel(dp_ref, pw_ref, ps_ref, o_ref, x_ref, p_ref, *, ts, rc):
    s = pl.program_id(1)
    ch = o_ref.shape[2]
    grp = ch // len(POOL_WINDOWS)

    @pl.when(s == 0)
    def _():
        x_ref[:, 0:POOL_HALO, :] = jnp.zeros((ch // 128, POOL_HALO, 128), F32)

    @pl.when(s > 0)
    def _():
        x_ref[:, 0:POOL_HALO, :] = x_ref[:, ts:ts + POOL_HALO, :]

    def load_body(i, carry):
        r0 = pl.multiple_of(i * rc, rc)
        xv = dp_ref[0, pl.ds(r0, rc), :].astype(F32)
        for cc in range(ch // 128):
            x_ref[cc, pl.ds(POOL_HALO + r0, rc), :] = xv[:, cc * 128:(cc + 1) * 128]
        return carry
    lax.fori_loop(0, ts // rc, load_body, 0)

    def pool_body(i, carry):
        r0 = pl.multiple_of(i * rc, rc)
        pos = s * ts + r0 + lax.broadcasted_iota(jnp.int32, (rc, 1), 0)
        for cc in range(ch // 128):
            w = POOL_WINDOWS[(cc * 128) // grp]
            cur = x_ref[cc, pl.ds(POOL_HALO + r0, rc), :]
            tot = cur
            for j in range(1, w):
                tot = tot + x_ref[cc, pl.ds(POOL_HALO + r0 - j, rc), :]
            cnt = jnp.minimum(pos + 1, w).astype(F32)
            p_ref[pl.ds(r0, rc), cc * 128:(cc + 1) * 128] = (tot / cnt - cur).astype(BF16)
        return carry
    lax.fori_loop(0, ts // rc, pool_body, 0)

    for gi in range(len(POOL_WINDOWS)):
        lanes = slice(gi * grp, (gi + 1) * grp)
        y = jnp.dot(p_ref[:, lanes], pw_ref[gi], preferred_element_type=F32)
        o_ref[0, :, lanes] = (y * ps_ref[0:1, lanes]).astype(o_ref.dtype)


def pool_branch(u3, pool_w, pool_scale, mix, *, ts=512, rc=32):
    bsz, seq, _ = u3.shape
    ng, grp, _ = pool_w.shape
    return pl.pallas_call(
        functools.partial(_pool_kernel, ts=ts, rc=rc),
        grid=(bsz, seq // ts),
        in_specs=[
            pl.BlockSpec((1, ts, mix), lambda b, s: (b, s, 4)),
            pl.BlockSpec((ng, grp, grp), lambda b, s: (0, 0, 0)),
            pl.BlockSpec((1, mix), lambda b, s: (0, 0)),
        ],
        out_specs=pl.BlockSpec((1, ts, mix), lambda b, s: (b, s, 0)),
        out_shape=jax.ShapeDtypeStruct((bsz, seq, mix), BF16),
        scratch_shapes=[pltpu.VMEM((mix // 128, ts + POOL_HALO, 128), F32), pltpu.VMEM((ts, mix), BF16)],
        compiler_params=_cparams(("arbitrary", "arbitrary")),
        name="pool_branch",
    )(u3, pool_w.astype(BF16), pool_scale.reshape(1, mix))


def kernel(x, c, ada_w, ada_b, w_in, w_out, ln_g, ln_b, mlp_w1, mlp_w2, conv_w, conv_b, conv_ln_g, conv_ln_b,
           rel_bias, hgrn_lb_logits, hgrn_norm_g, pool_w, pool_scale):
    bsz, seq, d = x.shape
    depth = ada_w.shape[0]
    mix = d // 2
    alpha = (2.0 * depth) ** 0.25
    m = bsz * seq

    mod = adaln_mod(c, ada_w, ada_b)
    x2 = x.reshape(m, d)
    h = modulate(x2, mod[0], seq)
    for l in range(depth):
        u = mm_in(h, cast_layer_bf16(w_in, l), perm_from=2 * mix if l % 2 == 0 else None)
        u3 = u.reshape(bsz, seq, 5 * mix)
        if l % 2 == 0:
            e = l // 2
            za = conv_branch(u3, conv_w[e], conv_b[e], conv_ln_g[e], conv_ln_b[e], mix)
            zb = dilated_attention(u3, rel_bias, mix)
        else:
            o = l // 2
            za = hgrn_branch(u3, hgrn_lb_logits, l, hgrn_norm_g[o], mix)
            zb = pool_branch(u3, pool_w[o], pool_scale[o], mix)
        x2, h2 = mm_out_ln(za.reshape(m, mix), zb.reshape(m, mix), cast_layer_bf16(w_out, l), x2, mod[l],
                           ln_g[l, 0], ln_b[l, 0], seq, alpha)
        x2, h = mlp_ln(h2, x2, mod[l], mod[l + 1] if l + 1 < depth else None,
                       cast_layer_bf16(mlp_w1, l), cast_layer_bf16(mlp_w2, l), ln_g[l, 1], ln_b[l, 1], seq, alpha)
    return x2.reshape(bsz, seq, d)
```

```python
import functools
import math

import jax
import jax.numpy as jnp
from jax import lax
from jax.experimental import pallas as pl
from jax.experimental.pallas import tpu as pltpu

F32 = jnp.float32
BF16 = jnp.bfloat16

HEAD_DIM = 128
CONV_K = 31
DSW_PATTERNS = ((128, 1), (512, 4), (2048, 16))
NUM_BUCKETS = 32
MAX_DISTANCE = 2048
HGRN_CHUNK = 64
POOL_WINDOWS = (2, 4, 8, 16)
LN_EPS = 1e-5
NEG = -1e30

V7X_VMEM_BYTES = 64 * 1024 * 1024
VMEM_LIMIT = 58 * 1024 * 1024


def _cparams(sem):
    return pltpu.CompilerParams(dimension_semantics=sem, vmem_limit_bytes=VMEM_LIMIT)


def _silu(v):
    return v * jax.nn.sigmoid(v)


def _mod_kernel(c_ref, w_ref, b_ref, o_ref):
    cs = _silu(c_ref[...]).astype(BF16)
    w = w_ref[0].astype(BF16)
    o_ref[0] = jnp.dot(cs, w, preferred_element_type=F32) + b_ref[0]


def adaln_mod(c, ada_w, ada_b, *, tn=512):
    nl, d, n6 = ada_w.shape
    bsz = c.shape[0]
    rows = 8
    c8 = jnp.zeros((rows, d), F32).at[:bsz].set(c)
    out = pl.pallas_call(
        _mod_kernel,
        grid=(nl, n6 // tn),
        in_specs=[
            pl.BlockSpec((rows, d), lambda l, j: (0, 0)),
            pl.BlockSpec((1, d, tn), lambda l, j: (l, 0, j)),
            pl.BlockSpec((1, 1, tn), lambda l, j: (l, 0, j)),
        ],
        out_specs=pl.BlockSpec((1, rows, tn), lambda l, j: (l, 0, j)),
        out_shape=jax.ShapeDtypeStruct((nl, rows, n6), F32),
        compiler_params=_cparams(("arbitrary", "arbitrary")),
        name="adaln_mod",
    )(c8, ada_w, ada_b.reshape(nl, 1, n6))
    return out[:, :bsz].reshape(nl, bsz, 6, d)


LN_ROWS = 16
LN_COLS = 512


def _ln_chunk(load, d, g_ref, b_ref, emit):
    col_slices = [slice(c * LN_COLS, (c + 1) * LN_COLS) for c in range(d // LN_COLS)]
    tot = load(col_slices[0])
    for cols in col_slices[1:]:
        tot = tot + load(cols)
    mu = jnp.sum(tot, axis=-1, keepdims=True) * (1.0 / d)
    sq = None
    for cols in col_slices:
        dv = load(cols) - mu
        sq = dv * dv if sq is None else sq + dv * dv
    rstd = lax.rsqrt(jnp.sum(sq, axis=-1, keepdims=True) * (1.0 / d) + LN_EPS)
    for cols in col_slices:
        emit(cols, (load(cols) - mu) * rstd * g_ref[:, cols] + b_ref[:, cols])


def _ln_mod_phase(step, rows_per_step, src_ref, xr_ref, alpha, gate_ref, g_ref, b_ref, mod_ref, sh_row, sc_row,
                  xo_ref, ho_ref):
    d = xo_ref.shape[1]
    for k in range(rows_per_step // LN_ROWS):
        src_rows = pl.ds(pl.multiple_of(step * rows_per_step + k * LN_ROWS, LN_ROWS), LN_ROWS)
        out_rows = slice(k * LN_ROWS, (k + 1) * LN_ROWS)
        if xr_ref is not None:
            for c in range(d // LN_COLS):
                cols = slice(c * LN_COLS, (c + 1) * LN_COLS)
                xo_ref[out_rows, cols] = alpha * xr_ref[out_rows, cols] + gate_ref[0, 5:6, cols] * src_ref[src_rows, cols]
            load = lambda cols, out_rows=out_rows: xo_ref[out_rows, cols]
        else:
            load = lambda cols, src_rows=src_rows: src_ref[src_rows, cols]

        def emit(cols, y, out_rows=out_rows):
            xo_ref[out_rows, cols] = y
            if ho_ref is not None:
                h = y * (1.0 + mod_ref[0, sc_row:sc_row + 1, cols]) + mod_ref[0, sh_row:sh_row + 1, cols]
                ho_ref[out_rows, cols] = h.astype(BF16)
        _ln_chunk(load, d, g_ref, b_ref, emit)


def _cast_kernel(w_ref, o_ref):
    o_ref[...] = w_ref[0].astype(o_ref.dtype)


def cast_layer_bf16(w_stack, layer, *, tr=512, tc=2048):
    _, rows, cols = w_stack.shape
    tc = min(tc, cols)
    return pl.pallas_call(
        _cast_kernel,
        grid=(rows // tr, cols // tc),
        in_specs=[pl.BlockSpec((1, tr, tc), lambda i, j: (layer, i, j))],
        out_specs=pl.BlockSpec((tr, tc), lambda i, j: (i, j)),
        out_shape=jax.ShapeDtypeStruct((rows, cols), BF16),
        compiler_params=_cparams(("arbitrary", "arbitrary")),
        name="cast_bf16",
    )(w_stack)


def _modulate_kernel(x_ref, mod_ref, o_ref):
    o_ref[...] = (x_ref[...] * (1.0 + mod_ref[0, 1:2, :]) + mod_ref[0, 0:1, :]).astype(o_ref.dtype)


def modulate(x2, mod_l, seq, *, tm=256):
    m, d = x2.shape
    return pl.pallas_call(
        _modulate_kernel,
        grid=(m // tm,),
        in_specs=[
            pl.BlockSpec((tm, d), lambda i: (i, 0)),
            pl.BlockSpec((1, 6, d), lambda i: ((i * tm) // seq, 0, 0)),
        ],
        out_specs=pl.BlockSpec((tm, d), lambda i: (i, 0)),
        out_shape=jax.ShapeDtypeStruct((m, d), BF16),
        compiler_params=_cparams(("arbitrary",)),
        name="modulate",
    )(x2, mod_l)


def _mm_in_kernel(*refs, perm_tile, n_side, side_blocks, n_j):
    h_ref, w_ref = refs[:2]
    side_in = refs[2:2 + n_side]
    o_ref = refs[2 + n_side]
    side_out = refs[3 + n_side:3 + 2 * n_side]
    scratch = refs[3 + 2 * n_side:]
    i = pl.program_id(0)
    j = pl.program_id(1)

    step = i * n_j + j
    for src, dst, nb in zip(side_in, side_out, side_blocks):
        @pl.when(step < nb)
        def _(src=src, dst=dst):
            dst[...] = src[0].astype(dst.dtype)

    if perm_tile is None:
        o_ref[...] = jnp.dot(h_ref[...], w_ref[...], preferred_element_type=F32).astype(o_ref.dtype)
        return
    hp_ref, = scratch

    @pl.when(j == 0)
    def _():
        perm = _group_perm()
        for g in range(h_ref.shape[0] // ATT_GROUP):
            rows = slice(g * ATT_GROUP, (g + 1) * ATT_GROUP)
            hp_ref[rows, :] = jnp.dot(perm, h_ref[rows, :], preferred_element_type=F32).astype(BF16)

    @pl.when(j < perm_tile)
    def _():
        o_ref[...] = jnp.dot(h_ref[...], w_ref[...], preferred_element_type=F32).astype(o_ref.dtype)

    @pl.when(j >= perm_tile)
    def _():
        o_ref[...] = jnp.dot(hp_ref[...], w_ref[...], preferred_element_type=F32).astype(o_ref.dtype)


def mm_in(h, w, *, perm_from=None, side_casts=(), tm=1024, tn=512):
    m, d = h.shape
    n = w.shape[1]
    n_i, n_j = m // tm, n // tn
    perm_tile = None
    scratch = []
    if perm_from is not None:
        assert perm_from % tn == 0 and tm % ATT_GROUP == 0
        perm_tile = perm_from // tn
        scratch = [pltpu.VMEM((tm, d), BF16)]
    in_specs = [
        pl.BlockSpec((tm, d), lambda i, j: (i, 0)),
        pl.BlockSpec((d, tn), lambda i, j: (0, j)),
    ]
    out_specs = [pl.BlockSpec((tm, tn), lambda i, j: (i, j))]
    out_shape = [jax.ShapeDtypeStruct((m, n), BF16)]
    side_blocks = []
    for arr, layer, br in side_casts:
        _, rows, cols = arr.shape
        nb = rows // br
        assert rows % br == 0 and nb <= n_i * n_j
        side_blocks.append(nb)
        blk = lambda i, j, nb=nb: jnp.minimum(i * n_j + j, nb - 1)
        in_specs.append(pl.BlockSpec((1, br, cols), lambda i, j, layer=layer, blk=blk: (layer, blk(i, j), 0)))
        out_specs.append(pl.BlockSpec((br, cols), lambda i, j, blk=blk: (blk(i, j), 0)))
        out_shape.append(jax.ShapeDtypeStruct((rows, cols), BF16))
    res = pl.pallas_call(
        functools.partial(_mm_in_kernel, perm_tile=perm_tile, n_side=len(side_casts),
                          side_blocks=tuple(side_blocks), n_j=n_j),
        grid=(n_i, n_j),
        in_specs=in_specs,
        out_specs=out_specs,
        out_shape=out_shape,
        scratch_shapes=scratch,
        compiler_params=_cparams(("arbitrary", "arbitrary")),
        name="mm_in",
    )(h, w, *[a for a, _, _ in side_casts])
    return res[0], list(res[1:])


def _mm_out_kernel(a_ref, b_ref, w_ref, x_ref, mod_ref, modp_ref, g_ref, beta_ref, xo_ref, ho_ref, acc_ref, ln_ref,
                   *, n_t, n_n, tn, k1, alpha):
    i = pl.program_id(0)
    n = pl.program_id(1)
    rows_per_step = acc_ref.shape[0] // n_n

    @pl.when((i == 0) & (n == 0))
    def _():
        ln_ref[...] = jnp.zeros(ln_ref.shape, F32)

    def ln_phase():
        _ln_mod_phase(n, rows_per_step, ln_ref, None, alpha, None, g_ref, beta_ref, modp_ref, 3, 4, xo_ref, ho_ref)

    @pl.when(i < n_t)
    def _():
        y = jnp.dot(a_ref[...], w_ref[0:k1, :], preferred_element_type=F32)
        y = y + jnp.dot(b_ref[...], w_ref[k1:, :], preferred_element_type=F32)
        cols = pl.ds(pl.multiple_of(n * tn, tn), tn)
        acc_ref[:, cols] = alpha * x_ref[...] + mod_ref[0, 2:3, cols] * y
        ln_phase()

    @pl.when(i == n_t)
    def _():
        ln_phase()

    @pl.when((i < n_t) & (n == n_n - 1))
    def _():
        ln_ref[...] = acc_ref[...]


def mm_out_ln(za, zb, w, x2, mod_l, ln_g, ln_b, seq, alpha, *, tm=512, tn=512):
    m, k1 = za.shape
    d = w.shape[1]
    n_n = d // tn
    n_t = m // tm
    rl = tm // n_n
    assert rl % LN_ROWS == 0
    cur = lambda i: jnp.minimum(i, n_t - 1)
    prv = lambda i: jnp.maximum(i - 1, 0)
    col = lambda i, n: jnp.where(i < n_t, n, n_n - 1)
    out_row = pl.BlockSpec((rl, d), lambda i, n: (jnp.where(i == 0, 0, (i - 1) * n_n + n), 0))
    return pl.pallas_call(
        functools.partial(_mm_out_kernel, n_t=n_t, n_n=n_n, tn=tn, k1=k1, alpha=alpha),
        grid=(n_t + 1, n_n),
        in_specs=[
            pl.BlockSpec((tm, k1), lambda i, n: (cur(i), 0)),
            pl.BlockSpec((tm, zb.shape[1]), lambda i, n: (cur(i), 0)),
            pl.BlockSpec((w.shape[0], tn), lambda i, n: (0, col(i, n))),
            pl.BlockSpec((tm, tn), lambda i, n: (cur(i), col(i, n))),
            pl.BlockSpec((1, 6, d), lambda i, n: ((cur(i) * tm) // seq, 0, 0)),
            pl.BlockSpec((1, 6, d), lambda i, n: ((prv(i) * tm) // seq, 0, 0)),
            pl.BlockSpec((1, d), lambda i, n: (0, 0)),
            pl.BlockSpec((1, d), lambda i, n: (0, 0)),
        ],
        out_specs=[out_row, out_row],
        out_shape=[jax.ShapeDtypeStruct((m, d), F32), jax.ShapeDtypeStruct((m, d), BF16)],
        scratch_shapes=[pltpu.VMEM((tm, d), F32), pltpu.VMEM((tm, d), F32)],
        compiler_params=_cparams(("arbitrary", "arbitrary")),
        name="mm_out_ln",
    )(za, zb, w, x2, mod_l, mod_l, ln_g.reshape(1, d), ln_b.reshape(1, d))


def _mlp_kernel(h_ref, xr_ref, modp_ref, modn_ref, w1_ref, w2_ref, g_ref, beta_ref, *rest,
                n_t, n_f, tn2, alpha, emit_h):
    if emit_h:
        xo_ref, ho_ref, acc_ref, ln_ref = rest
    else:
        xo_ref, acc_ref, ln_ref = rest
        ho_ref = None
    i = pl.program_id(0)
    f = pl.program_id(1)
    d = acc_ref.shape[1]
    rows_per_step = acc_ref.shape[0] // n_f

    @pl.when((i == 0) & (f == 0))
    def _():
        ln_ref[...] = jnp.zeros(ln_ref.shape, F32)

    @pl.when((i < n_t) & (f == 0))
    def _():
        acc_ref[...] = jnp.zeros(acc_ref.shape, F32)

    def ln_phase():
        _ln_mod_phase(f, rows_per_step, ln_ref, xr_ref, alpha, modp_ref, g_ref, beta_ref, modn_ref, 0, 1,
                      xo_ref, ho_ref)

    @pl.when(i < n_t)
    def _():
        t = jnp.dot(h_ref[...], w1_ref[...], preferred_element_type=F32)
        t = jnp.maximum(t, 0.0)
        t = (t * t).astype(BF16)
        for nb in range(d // tn2):
            cols = slice(nb * tn2, (nb + 1) * tn2)
            acc_ref[:, cols] += jnp.dot(t, w2_ref[:, cols], preferred_element_type=F32)
        ln_phase()

    @pl.when(i == n_t)
    def _():
        ln_phase()

    @pl.when((i < n_t) & (f == n_f - 1))
    def _():
        ln_ref[...] = acc_ref[...]


def mlp_ln(h, x2, mod_l, mod_next, w1, w2, ln_g, ln_b, seq, alpha, *, tm=512, tf=512, tn2=512):
    m, d = x2.shape
    dff = w1.shape[1]
    n_f = dff // tf
    n_t = m // tm
    rl = tm // n_f
    assert rl % LN_ROWS == 0
    emit_h = mod_next is not None
    cur = lambda i: jnp.minimum(i, n_t - 1)
    prv = lambda i: jnp.maximum(i - 1, 0)
    chunk = lambda i, f: jnp.where(i < n_t, f, n_f - 1)
    out_row = pl.BlockSpec((rl, d), lambda i, f: (jnp.where(i == 0, 0, (i - 1) * n_f + f), 0))
    modspec = pl.BlockSpec((1, 6, d), lambda i, f: ((prv(i) * tm) // seq, 0, 0))
    f32_out = jax.ShapeDtypeStruct((m, d), F32)
    res = pl.pallas_call(
        functools.partial(_mlp_kernel, n_t=n_t, n_f=n_f, tn2=tn2, alpha=alpha, emit_h=emit_h),
        grid=(n_t + 1, n_f),
        in_specs=[
            pl.BlockSpec((tm, d), lambda i, f: (cur(i), 0)),
            pl.BlockSpec((rl, d), lambda i, f: (prv(i) * n_f + f, 0)),
            modspec,
            modspec,
            pl.BlockSpec((d, tf), lambda i, f: (0, chunk(i, f))),
            pl.BlockSpec((tf, d), lambda i, f: (chunk(i, f), 0)),
            pl.BlockSpec((1, d), lambda i, f: (0, 0)),
            pl.BlockSpec((1, d), lambda i, f: (0, 0)),
        ],
        out_specs=[out_row, out_row] if emit_h else out_row,
        out_shape=[f32_out, jax.ShapeDtypeStruct((m, d), BF16)] if emit_h else f32_out,
        scratch_shapes=[pltpu.VMEM((tm, d), F32), pltpu.VMEM((tm, d), F32)],
        compiler_params=_cparams(("arbitrary", "arbitrary")),
        name="mlp_ln",
    )(h, x2, mod_l, mod_next if emit_h else mod_l, w1, w2, ln_g.reshape(1, d), ln_b.reshape(1, d))
    return res if emit_h else (res, None)


CONV_HALO = 32


def _conv_kernel(av_ref, ag_ref, w_ref, cb_ref, g_ref, beta_ref, o_ref, glu_ref, y_ref, *, ts, rc):
    s = pl.program_id(1)
    ch = o_ref.shape[2]

    @pl.when(s == 0)
    def _():
        glu_ref[:, 0:CONV_HALO, :] = jnp.zeros((ch // 128, CONV_HALO, 128), F32)

    @pl.when(s > 0)
    def _():
        glu_ref[:, 0:CONV_HALO, :] = glu_ref[:, ts:ts + CONV_HALO, :]

    def glu_body(i, carry):
        r0 = pl.multiple_of(i * rc, rc)
        a = av_ref[0, pl.ds(r0, rc), :].astype(F32)
        gt = ag_ref[0, pl.ds(r0, rc), :].astype(F32)
        glu = a * jax.nn.sigmoid(gt)
        for cc in range(ch // 128):
            glu_ref[cc, pl.ds(CONV_HALO + r0, rc), :] = glu[:, cc * 128:(cc + 1) * 128]
        return carry
    lax.fori_loop(0, ts // rc, glu_body, 0)

    off = CONV_HALO - (CONV_K - 1)

    def conv_body(i, carry):
        r0 = pl.multiple_of(i * rc, rc)
        for cc in range(ch // 128):
            lanes = slice(cc * 128, (cc + 1) * 128)
            acc = jnp.zeros((rc, 128), F32)
            for j in range(CONV_K):
                acc = acc + w_ref[j:j + 1, lanes] * glu_ref[cc, pl.ds(r0 + off + j, rc), :]
            y_ref[pl.ds(r0, rc), lanes] = acc + cb_ref[0:1, lanes]
        return carry
    lax.fori_loop(0, ts // rc, conv_body, 0)

    def ln_body(i, carry):
        rows = pl.ds(pl.multiple_of(i * LN_ROWS, LN_ROWS), LN_ROWS)

        def emit(cols, z):
            o_ref[0, rows, cols] = _silu(z).astype(o_ref.dtype)
        _ln_chunk(lambda cols: y_ref[rows, cols], ch, g_ref, beta_ref, emit)
        return carry
    lax.fori_loop(0, ts // LN_ROWS, ln_body, 0, unroll=2)


def conv_branch(u3, conv_w, conv_b, ln_g, ln_b, mix, *, ts=512, rc=32):
    bsz, seq, _ = u3.shape
    return pl.pallas_call(
        functools.partial(_conv_kernel, ts=ts, rc=rc),
        grid=(bsz, seq // ts),
        in_specs=[
            pl.BlockSpec((1, ts, mix), lambda b, s: (b, s, 0)),
            pl.BlockSpec((1, ts, mix), lambda b, s: (b, s, 1)),
            pl.BlockSpec((CONV_K, mix), lambda b, s: (0, 0)),
            pl.BlockSpec((1, mix), lambda b, s: (0, 0)),
            pl.BlockSpec((1, mix), lambda b, s: (0, 0)),
            pl.BlockSpec((1, mix), lambda b, s: (0, 0)),
        ],
        out_specs=pl.BlockSpec((1, ts, mix), lambda b, s: (b, s, 0)),
        out_shape=jax.ShapeDtypeStruct((bsz, seq, mix), BF16),
        scratch_shapes=[pltpu.VMEM((mix // 128, ts + CONV_HALO, 128), F32), pltpu.VMEM((ts, mix), F32)],
        compiler_params=_cparams(("arbitrary", "arbitrary")),
        name="conv_branch",
    )(u3, u3, conv_w, conv_b.reshape(1, mix), ln_g.reshape(1, mix), ln_b.reshape(1, mix))


def _t5_bucket(dist):
    max_exact = NUM_BUCKETS // 2
    nf = jnp.maximum(dist, 1).astype(F32)
    large = max_exact + (jnp.log(nf / max_exact) / math.log(MAX_DISTANCE / max_exact)
                         * (NUM_BUCKETS - max_exact)).astype(jnp.int32)
    large = jnp.minimum(large, NUM_BUCKETS - 1)
    return jnp.where(dist < max_exact, dist, large)


ATT_GROUP = 256
ATT_RES = 16


def _natural_index(dil):
    if dil == 1:
        i = jnp.arange(ATT_GROUP)
        return ATT_GROUP, ATT_RES * (i % ATT_RES) + i // ATT_RES
    if dil == 4:
        i = jnp.arange(HEAD_DIM)
        return HEAD_DIM, (i // 64) * 64 + 4 * (i % 16) + (i % 64) // 16
    assert dil == ATT_RES
    return HEAD_DIM, jnp.arange(HEAD_DIM)


def _bucket_tile(window, dil):
    steps = window // dil
    qr, nat = _natural_index(dil)
    qn = nat[:, None] + qr
    kn = jnp.concatenate([nat, nat + qr])[None, :]
    step = qn - kn
    bucket = _t5_bucket(jnp.clip(step, 0, steps) * dil)
    valid = (step >= 0) & (step <= steps)
    return jnp.where(valid, bucket, -1).astype(jnp.int32)


def _bias_kernel(rb_ref, idx_ref, o_ref, *, heads):
    idx = idx_ref[...]
    for h in range(heads):
        acc = jnp.full(idx.shape, NEG, F32)
        for bk in range(NUM_BUCKETS):
            acc = jnp.where(idx == bk, rb_ref[bk, h], acc)
        o_ref[h] = acc


def attn_bias(rel_bias, window, dil):
    heads = rel_bias.shape[1]
    idx = _bucket_tile(window, dil)
    qr, qr2 = idx.shape
    return pl.pallas_call(
        functools.partial(_bias_kernel, heads=heads),
        grid=(1,),
        in_specs=[
            pl.BlockSpec(memory_space=pltpu.SMEM),
            pl.BlockSpec((qr, qr2), lambda i: (0, 0)),
        ],
        out_specs=pl.BlockSpec((heads, qr, qr2), lambda i: (0, 0, 0)),
        out_shape=jax.ShapeDtypeStruct((heads, qr, qr2), F32),
        compiler_params=_cparams(("arbitrary",)),
        name=f"attn_bias_d{dil}",
    )(rel_bias.astype(F32), idx)


def _group_perm():
    shift = ATT_RES.bit_length() - 1
    row = lax.broadcasted_iota(jnp.int32, (ATT_GROUP, ATT_GROUP), 0)
    col = lax.broadcasted_iota(jnp.int32, (ATT_GROUP, ATT_GROUP), 1)
    src = ((row & (ATT_RES - 1)) << shift) | (row >> shift)
    return jnp.where(col == src, 1.0, 0.0).astype(BF16)


def _attn_kernel(*refs, hb, qr, nqb, has_halo, has_prev, last, scale):
    it = iter(refs)
    q_ref, k_ref, v_ref = next(it), next(it), next(it)
    kp_ref, vp_ref = (next(it), next(it)) if has_halo else (None, None)
    bias_ref = next(it)
    op_ref, lp_ref = (next(it), next(it)) if has_prev else (None, None)
    o_ref = next(it)
    l_ref = None if last else next(it)
    kbuf, vbuf, s_scr, p_scr, m_scr, d_scr, obuf, lbuf = (next(it) for _ in range(8))
    t = pl.program_id(3)
    tq = qr * nqb
    wb = hb * HEAD_DIM
    nt = (((1,), (1,)), ((), ()))
    units = [(h, j) for h in range(hb) for j in range(nqb)]

    q = q_ref[...].reshape(tq, wb)
    kbuf[qr:, :] = k_ref[...].reshape(tq, wb)
    vbuf[qr:, :] = v_ref[...].reshape(tq, wb)
    if has_halo:
        kbuf[0:qr, :] = kp_ref[...].reshape(qr, wb)
        vbuf[0:qr, :] = vp_ref[...].reshape(qr, wb)
        first = t == 0
    else:
        kbuf[0:qr, :] = jnp.zeros((qr, wb), BF16)
        vbuf[0:qr, :] = jnp.zeros((qr, wb), BF16)
        first = t >= 0
    prev_cols = lax.broadcasted_iota(jnp.int32, (qr, 2 * qr), 1) < qr

    for u, (h, j) in enumerate(units):
        lanes = slice(h * HEAD_DIM, (h + 1) * HEAD_DIM)
        s = lax.dot_general(q[j * qr:(j + 1) * qr, lanes], kbuf[j * qr:(j + 2) * qr, lanes], nt,
                            preferred_element_type=F32)
        s = s * scale + bias_ref[h]
        if j == 0:
            s = jnp.where(prev_cols & first, NEG, s)
        s_scr[u] = s

    for u in range(len(units)):
        s = s_scr[u]
        m = jnp.max(s, axis=-1, keepdims=True)
        p = jnp.exp(s - m)
        m_scr[u] = jnp.broadcast_to(m, (qr, HEAD_DIM))
        d_scr[u] = jnp.broadcast_to(jnp.sum(p, axis=-1, keepdims=True), (qr, HEAD_DIM))
        p_scr[u] = p.astype(BF16)

    if has_prev:
        o_prev = op_ref[...].reshape(tq, wb)
        l_prev = lp_ref[...].reshape(tq, wb)
    for u, (h, j) in enumerate(units):
        lanes = slice(h * HEAD_DIM, (h + 1) * HEAD_DIM)
        rows = slice(j * qr, (j + 1) * qr)
        acc = jnp.dot(p_scr[u], vbuf[j * qr:(j + 2) * qr, lanes], preferred_element_type=F32)
        den = d_scr[u]
        o = acc / den
        lse = m_scr[u] + jnp.log(den)
        if has_prev:
            lse0 = l_prev[rows, lanes]
            mx = jnp.maximum(lse0, lse)
            w0 = jnp.exp(lse0 - mx)
            w1 = jnp.exp(lse - mx)
            tot = w0 + w1
            o = (w0 * o_prev[rows, lanes] + w1 * o) / tot
            lse = mx + jnp.log(tot)
        obuf[rows, lanes] = o
        if not last:
            lbuf[rows, lanes] = lse

    if last:
        res = jnp.dot(_group_perm(), obuf[...].astype(BF16), preferred_element_type=F32)
        o_ref[...] = res.astype(o_ref.dtype).reshape(o_ref.shape)
    else:
        o_ref[...] = obuf[...].reshape(o_ref.shape)
        l_ref[...] = lbuf[...].reshape(l_ref.shape)


def attn_pattern(u3, bias, dil, prev, *, mix, last, hb=4):
    bsz, seq, _ = u3.shape
    heads = mix // HEAD_DIM
    ng = seq // ATT_GROUP
    wb = hb * HEAD_DIM
    q0, k0, v0 = (2 * mix) // wb, (3 * mix) // wb, (4 * mix) // wb
    qr = bias.shape[1]
    halo = None
    if dil == 1:
        nqb, n_res, n_t = 1, 1, ng
        view = lambda a: a.reshape(bsz, ng, ATT_GROUP, a.shape[-1])
        cur = lambda c0: pl.BlockSpec((1, 1, ATT_GROUP, wb), lambda g, b, r, t: (b, t, 0, c0 + g))
        halo = lambda c0: pl.BlockSpec((1, 1, ATT_GROUP, wb),
                                       lambda g, b, r, t: (b, jnp.maximum(t - 1, 0), 0, c0 + g))
    elif dil == 4:
        nqb, n_res, gt = 2, 4, 4
        n_t = ng // gt
        view = lambda a: a.reshape(bsz, ng, 4, 4, ATT_RES, a.shape[-1])
        cur = lambda c0: pl.BlockSpec((1, gt, 4, 1, ATT_RES, wb), lambda g, b, r, t: (b, t, 0, r, 0, c0 + g))
        halo = lambda c0: pl.BlockSpec((1, gt // 2, 4, 1, ATT_RES, wb),
                                       lambda g, b, r, t: (b, jnp.maximum(2 * t - 1, 0), 0, r, 0, c0 + g))
    else:
        assert dil == ATT_RES and ng * ATT_RES == 2 * HEAD_DIM
        nqb, n_res, n_t = 2, ATT_RES, 1
        view = lambda a: a.reshape(bsz, ng, ATT_RES, ATT_RES, a.shape[-1])
        cur = lambda c0: pl.BlockSpec((1, ng, 1, ATT_RES, wb), lambda g, b, r, t: (b, 0, r, 0, c0 + g))
    has_halo = halo is not None
    tq = qr * nqb
    uv = view(u3)
    in_specs = [cur(q0), cur(k0), cur(v0)]
    args = [uv, uv, uv]
    if has_halo:
        in_specs += [halo(k0), halo(v0)]
        args += [uv, uv]
    in_specs.append(pl.BlockSpec((hb, qr, 2 * qr), lambda g, b, r, t: (g, 0, 0)))
    args.append(bias)
    if prev is not None:
        in_specs += [cur(0), cur(0)]
        args += [view(prev[0]), view(prev[1])]
    if last:
        assert dil == 1
        out_specs = pl.BlockSpec((1, ATT_GROUP, wb), lambda g, b, r, t: (b, t, g))
        out_shape = jax.ShapeDtypeStruct((bsz, seq, mix), BF16)
    else:
        oshape = jax.eval_shape(view, jax.ShapeDtypeStruct((bsz, seq, mix), F32))
        out_specs = [cur(0), cur(0)]
        out_shape = [oshape, oshape]
    n_u = hb * nqb
    res = pl.pallas_call(
        functools.partial(_attn_kernel, hb=hb, qr=qr, nqb=nqb, has_halo=has_halo, has_prev=prev is not None,
                          last=last, scale=HEAD_DIM ** -0.5),
        grid=(heads // hb, bsz, n_res, n_t),
        in_specs=in_specs,
        out_specs=out_specs,
        out_shape=out_shape,
        scratch_shapes=[
            pltpu.VMEM((tq + qr, wb), BF16), pltpu.VMEM((tq + qr, wb), BF16),
            pltpu.VMEM((n_u, qr, 2 * qr), F32), pltpu.VMEM((n_u, qr, 2 * qr), BF16),
            pltpu.VMEM((n_u, qr, HEAD_DIM), F32), pltpu.VMEM((n_u, qr, HEAD_DIM), F32),
            pltpu.VMEM((tq, wb), F32), pltpu.VMEM((tq, wb), F32),
        ],
        compiler_params=_cparams(("arbitrary",) * 4),
        name=f"attn_d{dil}",
    )(*args)
    if last:
        return res
    return res[0].reshape(bsz, seq, mix), res[1].reshape(bsz, seq, mix)


def dilated_attention(u3, rel_bias, mix):
    order = sorted(DSW_PATTERNS, key=lambda wd: wd[1] == 1)
    prev = None
    for gi, (window, dil) in enumerate(order):
        assert window // dil == HEAD_DIM
        prev = attn_pattern(u3, attn_bias(rel_bias, window, dil), dil, prev, mix=mix, last=gi == len(order) - 1)
    return prev


def _split3(v):
    hi = v.astype(BF16)
    r1 = v - hi.astype(F32)
    mid = r1.astype(BF16)
    lo = (r1 - mid.astype(F32)).astype(BF16)
    return hi, mid, lo


def _scaled(v, expo, mask):
    return jnp.where(mask, v * jnp.exp(jnp.where(mask, expo, 0.0)), 0.0)


def _hgrn_kernel(cq_ref, cf_ref, ci_ref, cg_ref, lb_ref, ng_ref, o_ref, st_ref, q_s, kk_s, b_s, a_s,
                 *, hb, ts, layer):
    s = pl.program_id(2)
    c_len = HGRN_CHUNK
    sub = 16
    nt = (((1,), (1,)), ((), ()))
    tn_ = (((0,), (0,)), ((), ()))

    @pl.when(s == 0)
    def _():
        st_ref[...] = jnp.zeros(st_ref.shape, F32)

    row = lax.broadcasted_iota(jnp.int32, (c_len, c_len), 0)
    col = lax.broadcasted_iota(jnp.int32, (c_len, c_len), 1)
    tri = jnp.where(row >= col, 1.0, 0.0).astype(BF16)
    r64 = lax.broadcasted_iota(jnp.int32, (c_len, HEAD_DIM), 0)
    half = c_len // 2
    mask_b = (((row >= sub) & (row < half) & (col < sub))
              | ((row >= half + sub) & (col >= half) & (col < half + sub)))
    lrow = lax.broadcasted_iota(jnp.int32, (lb_ref.shape[0], HEAD_DIM), 0)

    def chunk_body(ci, carry):
        r0 = pl.multiple_of(ci * c_len, c_len)
        rows = pl.ds(r0, c_len)
        head_lanes = [slice(h * HEAD_DIM, (h + 1) * HEAD_DIM) for h in range(hb)]
        for h, lanes in enumerate(head_lanes):
            lg = lb_ref[:, lanes]
            pe = jnp.exp(lg - jnp.max(lg, axis=0, keepdims=True))
            lb = (jnp.sum(jnp.where((lrow >= 1) & (lrow <= layer), pe, 0.0), axis=0, keepdims=True)
                  / jnp.sum(pe, axis=0, keepdims=True))
            f = lb + (1.0 - lb) * jax.nn.sigmoid(cf_ref[0, rows, lanes].astype(F32))
            hi, mid, lo = _split3(jnp.log(f))
            q_s[h] = _silu(cq_ref[0, rows, lanes].astype(F32))
            kk_s[h] = 1.0 - f
            b_s[h] = (jnp.dot(tri, hi, preferred_element_type=F32)
                      + jnp.dot(tri, mid, preferred_element_type=F32)
                      + jnp.dot(tri, lo, preferred_element_type=F32))

        for h in range(hb):
            q, kk, b = q_s[h], kk_s[h], b_s[h]
            b_a = b_s[h, half - 1:half, :]
            qa = _scaled(q, b - b_a, r64 >= half)
            ka = _scaled(kk, b_a - b, r64 < half)
            attn = lax.dot_general(qa.astype(BF16), ka.astype(BF16), nt, preferred_element_type=F32)
            b_r = jnp.where(r64 < half, b_s[h, sub - 1:sub, :], b_s[h, half + sub - 1:half + sub, :])
            qsel = ((r64 >= sub) & (r64 < half)) | (r64 >= half + sub)
            ksel = (r64 < sub) | ((r64 >= half) & (r64 < half + sub))
            qbm = _scaled(q, b - b_r, qsel)
            kbm = _scaled(kk, b_r - b, ksel)
            attn_b = lax.dot_general(qbm.astype(BF16), kbm.astype(BF16), nt, preferred_element_type=F32)
            a_s[h, :, 0:c_len] = attn + jnp.where(mask_b, attn_b, 0.0)

        for h in range(hb):
            for jb in range(c_len // sub):
                blk = slice(jb * sub, (jb + 1) * sub)
                qt = q_s[h, blk, :]
                bt = b_s[h, blk, :]
                for si in range(sub):
                    r = jb * sub + si
                    e = jnp.exp(bt - b_s[h, r:r + 1, :])
                    a_s[h, blk, r:r + 1] = jnp.sum(qt * kk_s[h, r:r + 1, :] * e, axis=-1, keepdims=True)

        for h, lanes in enumerate(head_lanes):
            q, kk, b = q_s[h], kk_s[h], b_s[h]
            v = ci_ref[0, rows, lanes]
            b_last = b_s[h, c_len - 1:c_len, :]
            st_t = st_ref[h]
            inter = lax.dot_general((q * jnp.exp(b)).astype(BF16), st_t.astype(BF16), nt,
                                    preferred_element_type=F32)
            attn = jnp.where(row >= col, a_s[h, :, 0:c_len], 0.0)
            o = inter + jnp.dot(attn.astype(BF16), v, preferred_element_type=F32)
            kd = (kk * jnp.exp(b_last - b)).astype(BF16)
            st_ref[h] = st_t * jnp.exp(b_last) + lax.dot_general(v, kd, tn_, preferred_element_type=F32)
            ms = jnp.mean(o * o, axis=-1, keepdims=True)
            o = o * lax.rsqrt(ms + LN_EPS) * ng_ref[0:1, lanes]
            o = o * _silu(cg_ref[0, rows, lanes].astype(F32))
            o_ref[0, rows, lanes] = o.astype(o_ref.dtype)
        return carry
    lax.fori_loop(0, ts // c_len, chunk_body, 0)


def hgrn_branch(u3, lb_logits, layer, norm_g, mix, *, hb=4, ts=256):
    bsz, seq, _ = u3.shape
    heads = mix // HEAD_DIM
    wb = hb * HEAD_DIM
    nb = mix // wb

    def col(k):
        return pl.BlockSpec((1, ts, wb), lambda b, g, s: (b, s, k * nb + g))

    vec = pl.BlockSpec((1, wb), lambda b, g, s: (0, g))
    return pl.pallas_call(
        functools.partial(_hgrn_kernel, hb=hb, ts=ts, layer=layer),
        grid=(bsz, heads // hb, seq // ts),
        in_specs=[col(0), col(1), col(2), col(3),
                  pl.BlockSpec((lb_logits.shape[0], wb), lambda b, g, s: (0, g)), vec],
        out_specs=pl.BlockSpec((1, ts, wb), lambda b, g, s: (b, s, g)),
        out_shape=jax.ShapeDtypeStruct((bsz, seq, mix), BF16),
        scratch_shapes=[pltpu.VMEM((hb, HEAD_DIM, HEAD_DIM), F32)]
        + [pltpu.VMEM((hb, HGRN_CHUNK, HEAD_DIM), F32)] * 4,
        compiler_params=_cparams(("arbitrary",) * 3),
        name="hgrn",
    )(u3, u3, u3, u3, lb_logits.astype(F32), norm_g.reshape(1, mix))


POOL_HALO = 16


def _pool_kernel(dp_ref, pw_ref, ps_ref, o_ref, x_ref, p_ref, *, ts, rc):
    s = pl.program_id(1)
    ch = o_ref.shape[2]
    grp = ch // len(POOL_WINDOWS)

    @pl.when(s == 0)
    def _():
        x_ref[:, 0:POOL_HALO, :] = jnp.zeros((ch // 128, POOL_HALO, 128), F32)

    @pl.when(s > 0)
    def _():
        x_ref[:, 0:POOL_HALO, :] = x_ref[:, ts:ts + POOL_HALO, :]

    def load_body(i, carry):
        r0 = pl.multiple_of(i * rc, rc)
        xv = dp_ref[0, pl.ds(r0, rc), :].astype(F32)
        for cc in range(ch // 128):
            x_ref[cc, pl.ds(POOL_HALO + r0, rc), :] = xv[:, cc * 128:(cc + 1) * 128]
        return carry
    lax.fori_loop(0, ts // rc, load_body, 0)

    def pool_body(i, carry):
        r0 = pl.multiple_of(i * rc, rc)
        pos = s * ts + r0 + lax.broadcasted_iota(jnp.int32, (rc, 1), 0)
        for cc in range(ch // 128):
            w = POOL_WINDOWS[(cc * 128) // grp]
            cur = x_ref[cc, pl.ds(POOL_HALO + r0, rc), :]
            tot = cur
            for j in range(1, w):
                tot = tot + x_ref[cc, pl.ds(POOL_HALO + r0 - j, rc), :]
            cnt = jnp.minimum(pos + 1, w).astype(F32)
            p_ref[pl.ds(r0, rc), cc * 128:(cc + 1) * 128] = (tot / cnt - cur).astype(BF16)
        return carry
    lax.fori_loop(0, ts // rc, pool_body, 0)

    for gi in range(len(POOL_WINDOWS)):
        lanes = slice(gi * grp, (gi + 1) * grp)
        y = jnp.dot(p_ref[:, lanes], pw_ref[gi], preferred_element_type=F32)
        o_ref[0, :, lanes] = (y * ps_ref[0:1, lanes]).astype(o_ref.dtype)


def pool_branch(u3, pool_w, pool_scale, mix, *, ts=512, rc=32):
    bsz, seq, _ = u3.shape
    ng, grp, _ = pool_w.shape
    return pl.pallas_call(
        functools.partial(_pool_kernel, ts=ts, rc=rc),
        grid=(bsz, seq // ts),
        in_specs=[
            pl.BlockSpec((1, ts, mix), lambda b, s: (b, s, 4)),
            pl.BlockSpec((ng, grp, grp), lambda b, s: (0, 0, 0)),
            pl.BlockSpec((1, mix), lambda b, s: (0, 0)),
        ],
        out_specs=pl.BlockSpec((1, ts, mix), lambda b, s: (b, s, 0)),
        out_shape=jax.ShapeDtypeStruct((bsz, seq, mix), BF16),
        scratch_shapes=[pltpu.VMEM((mix // 128, ts + POOL_HALO, 128), F32), pltpu.VMEM((ts, mix), BF16)],
        compiler_params=_cparams(("arbitrary", "arbitrary")),
        name="pool_branch",
    )(u3, pool_w.astype(BF16), pool_scale.reshape(1, mix))


def kernel(x, c, ada_w, ada_b, w_in, w_out, ln_g, ln_b, mlp_w1, mlp_w2, conv_w, conv_b, conv_ln_g, conv_ln_b,
           rel_bias, hgrn_lb_logits, hgrn_norm_g, pool_w, pool_scale):
    bsz, seq, d = x.shape
    depth = ada_w.shape[0]
    mix = d // 2
    alpha = (2.0 * depth) ** 0.25
    m = bsz * seq

    mod = adaln_mod(c, ada_w, ada_b)
    x2 = x.reshape(m, d)
    h = modulate(x2, mod[0], seq)
    for l in range(depth):
        u, (w_out_b, w1_b, w2_b) = mm_in(
            h, cast_layer_bf16(w_in, l), perm_from=2 * mix if l % 2 == 0 else None,
            side_casts=[(w_out, l, 16), (mlp_w1, l, 16), (mlp_w2, l, 64)])
        u3 = u.reshape(bsz, seq, 5 * mix)
        if l % 2 == 0:
            e = l // 2
            za = conv_branch(u3, conv_w[e], conv_b[e], conv_ln_g[e], conv_ln_b[e], mix)
            zb = dilated_attention(u3, rel_bias, mix)
        else:
            o = l // 2
            za = hgrn_branch(u3, hgrn_lb_logits, l, hgrn_norm_g[o], mix)
            zb = pool_branch(u3, pool_w[o], pool_scale[o], mix)
        x2, h2 = mm_out_ln(za.reshape(m, mix), zb.reshape(m, mix), w_out_b, x2, mod[l],
                           ln_g[l, 0], ln_b[l, 0], seq, alpha)
        x2, h = mlp_ln(h2, x2, mod[l], mod[l + 1] if l + 1 < depth else None,
                       w1_b, w2_b, ln_g[l, 1], ln_b[l, 1], seq, alpha)
    return x2.reshape(bsz, seq, d)
```

```python
import functools
import math

import jax
import jax.numpy as jnp
from jax import lax
from jax.experimental import pallas as pl
from jax.experimental.pallas import tpu as pltpu

F32 = jnp.float32
BF16 = jnp.bfloat16

HEAD_DIM = 128
CONV_K = 31
DSW_PATTERNS = ((128, 1), (512, 4), (2048, 16))
NUM_BUCKETS = 32
MAX_DISTANCE = 2048
HGRN_CHUNK = 64
POOL_WINDOWS = (2, 4, 8, 16)
LN_EPS = 1e-5
NEG = -1e30

V7X_VMEM_BYTES = 64 * 1024 * 1024
VMEM_LIMIT = V7X_VMEM_BYTES - 2 * 1024 * 1024


def _cparams(sem):
    return pltpu.CompilerParams(dimension_semantics=sem, vmem_limit_bytes=VMEM_LIMIT)


def _silu(v):
    return v * jax.nn.sigmoid(v)


def _mod_kernel(c_ref, w_ref, b_ref, o_ref):
    cs = _silu(c_ref[...]).astype(BF16)
    w = w_ref[0].astype(BF16)
    o_ref[0] = jnp.dot(cs, w, preferred_element_type=F32) + b_ref[0]


def adaln_mod(c, ada_w, ada_b, *, tn=512):
    nl, d, n6 = ada_w.shape
    bsz = c.shape[0]
    rows = 8
    c8 = jnp.zeros((rows, d), F32).at[:bsz].set(c)
    out = pl.pallas_call(
        _mod_kernel,
        grid=(nl, n6 // tn),
        in_specs=[
            pl.BlockSpec((rows, d), lambda l, j: (0, 0)),
            pl.BlockSpec((1, d, tn), lambda l, j: (l, 0, j)),
            pl.BlockSpec((1, 1, tn), lambda l, j: (l, 0, j)),
        ],
        out_specs=pl.BlockSpec((1, rows, tn), lambda l, j: (l, 0, j)),
        out_shape=jax.ShapeDtypeStruct((nl, rows, n6), F32),
        compiler_params=_cparams(("arbitrary", "arbitrary")),
        name="adaln_mod",
    )(c8, ada_w, ada_b.reshape(nl, 1, n6))
    return out[:, :bsz].reshape(nl, bsz, 6, d)


LN_ROWS = 16
LN_COLS = 512


def _ln_chunk(load, d, g_ref, b_ref, emit):
    col_slices = [slice(c * LN_COLS, (c + 1) * LN_COLS) for c in range(d // LN_COLS)]
    tot = load(col_slices[0])
    for cols in col_slices[1:]:
        tot = tot + load(cols)
    mu = jnp.sum(tot, axis=-1, keepdims=True) * (1.0 / d)
    sq = None
    for cols in col_slices:
        dv = load(cols) - mu
        sq = dv * dv if sq is None else sq + dv * dv
    rstd = lax.rsqrt(jnp.sum(sq, axis=-1, keepdims=True) * (1.0 / d) + LN_EPS)
    for cols in col_slices:
        emit(cols, (load(cols) - mu) * rstd * g_ref[:, cols] + b_ref[:, cols])


def _ln_mod_phase(step, rows_per_step, src_ref, xr_ref, alpha, gate_ref, g_ref, b_ref, mod_ref, sh_row, sc_row,
                  xo_ref, ho_ref):
    d = xo_ref.shape[1]
    for k in range(rows_per_step // LN_ROWS):
        src_rows = pl.ds(pl.multiple_of(step * rows_per_step + k * LN_ROWS, LN_ROWS), LN_ROWS)
        out_rows = slice(k * LN_ROWS, (k + 1) * LN_ROWS)
        if xr_ref is not None:
            for c in range(d // LN_COLS):
                cols = slice(c * LN_COLS, (c + 1) * LN_COLS)
                xo_ref[out_rows, cols] = alpha * xr_ref[out_rows, cols] + gate_ref[0, 5:6, cols] * src_ref[src_rows, cols]
            load = lambda cols, out_rows=out_rows: xo_ref[out_rows, cols]
        else:
            load = lambda cols, src_rows=src_rows: src_ref[src_rows, cols]

        def emit(cols, y, out_rows=out_rows):
            xo_ref[out_rows, cols] = y
            if ho_ref is not None:
                h = y * (1.0 + mod_ref[0, sc_row:sc_row + 1, cols]) + mod_ref[0, sh_row:sh_row + 1, cols]
                ho_ref[out_rows, cols] = h.astype(BF16)
        _ln_chunk(load, d, g_ref, b_ref, emit)


def _cast_kernel(w_ref, o_ref):
    o_ref[...] = w_ref[0].astype(o_ref.dtype)


def cast_layer_bf16(w_stack, layer, *, tr=512, tc=2048):
    _, rows, cols = w_stack.shape
    tc = min(tc, cols)
    return pl.pallas_call(
        _cast_kernel,
        grid=(rows // tr, cols // tc),
        in_specs=[pl.BlockSpec((1, tr, tc), lambda i, j: (layer, i, j))],
        out_specs=pl.BlockSpec((tr, tc), lambda i, j: (i, j)),
        out_shape=jax.ShapeDtypeStruct((rows, cols), BF16),
        compiler_params=_cparams(("arbitrary", "arbitrary")),
        name="cast_bf16",
    )(w_stack)


def _modulate_kernel(x_ref, mod_ref, o_ref):
    o_ref[...] = (x_ref[...] * (1.0 + mod_ref[0, 1:2, :]) + mod_ref[0, 0:1, :]).astype(o_ref.dtype)


def modulate(x2, mod_l, seq, *, tm=256):
    m, d = x2.shape
    return pl.pallas_call(
        _modulate_kernel,
        grid=(m // tm,),
        in_specs=[
            pl.BlockSpec((tm, d), lambda i: (i, 0)),
            pl.BlockSpec((1, 6, d), lambda i: ((i * tm) // seq, 0, 0)),
        ],
        out_specs=pl.BlockSpec((tm, d), lambda i: (i, 0)),
        out_shape=jax.ShapeDtypeStruct((m, d), BF16),
        compiler_params=_cparams(("arbitrary",)),
        name="modulate",
    )(x2, mod_l)


def _mm_in_kernel(*refs, perm_tile, n_side, side_blocks, n_j):
    h_ref, w_ref = refs[:2]
    side_in = refs[2:2 + n_side]
    o_ref = refs[2 + n_side]
    side_out = refs[3 + n_side:3 + 2 * n_side]
    scratch = refs[3 + 2 * n_side:]
    i = pl.program_id(0)
    j = pl.program_id(1)

    step = i * n_j + j
    for src, dst, nb in zip(side_in, side_out, side_blocks):
        @pl.when(step < nb)
        def _(src=src, dst=dst):
            dst[...] = src[0].astype(dst.dtype)

    if perm_tile is None:
        o_ref[...] = jnp.dot(h_ref[...], w_ref[...], preferred_element_type=F32).astype(o_ref.dtype)
        return
    hp_ref, = scratch

    @pl.when(j == 0)
    def _():
        perm = _group_perm()
        for g in range(h_ref.shape[0] // ATT_GROUP):
            rows = slice(g * ATT_GROUP, (g + 1) * ATT_GROUP)
            hp_ref[rows, :] = jnp.dot(perm, h_ref[rows, :], preferred_element_type=F32).astype(BF16)

    @pl.when(j < perm_tile)
    def _():
        o_ref[...] = jnp.dot(h_ref[...], w_ref[...], preferred_element_type=F32).astype(o_ref.dtype)

    @pl.when(j >= perm_tile)
    def _():
        o_ref[...] = jnp.dot(hp_ref[...], w_ref[...], preferred_element_type=F32).astype(o_ref.dtype)


def mm_in(h, w, *, perm_from=None, side_casts=(), tm=1024, tn=1024):
    m, d = h.shape
    n = w.shape[1]
    n_i, n_j = m // tm, n // tn
    perm_tile = None
    scratch = []
    if perm_from is not None:
        assert perm_from % tn == 0 and tm % ATT_GROUP == 0
        perm_tile = perm_from // tn
        scratch = [pltpu.VMEM((tm, d), BF16)]
    in_specs = [
        pl.BlockSpec((tm, d), lambda i, j: (i, 0), pipeline_mode=pl.Buffered(1)),
        pl.BlockSpec((d, tn), lambda i, j: (0, j)),
    ]
    out_specs = [pl.BlockSpec((tm, tn), lambda i, j: (i, j))]
    out_shape = [jax.ShapeDtypeStruct((m, n), BF16)]
    side_blocks = []
    for arr, layer, br in side_casts:
        _, rows, cols = arr.shape
        nb = rows // br
        assert rows % br == 0 and nb <= n_i * n_j
        side_blocks.append(nb)
        blk = lambda i, j, nb=nb: jnp.minimum(i * n_j + j, nb - 1)
        in_specs.append(pl.BlockSpec((1, br, cols), lambda i, j, layer=layer, blk=blk: (layer, blk(i, j), 0)))
        out_specs.append(pl.BlockSpec((br, cols), lambda i, j, blk=blk: (blk(i, j), 0)))
        out_shape.append(jax.ShapeDtypeStruct((rows, cols), BF16))
    res = pl.pallas_call(
        functools.partial(_mm_in_kernel, perm_tile=perm_tile, n_side=len(side_casts),
                          side_blocks=tuple(side_blocks), n_j=n_j),
        grid=(n_i, n_j),
        in_specs=in_specs,
        out_specs=out_specs,
        out_shape=out_shape,
        scratch_shapes=scratch,
        compiler_params=_cparams(("arbitrary", "arbitrary")),
        name="mm_in",
    )(h, w, *[a for a, _, _ in side_casts])
    return res[0], list(res[1:])


def _mm_out_kernel(a_ref, b_ref, w_ref, x_ref, mod_ref, modp_ref, g_ref, beta_ref, xo_ref, ho_ref, acc_ref, ln_ref,
                   *, n_t, n_n, tn, k1, alpha):
    i = pl.program_id(0)
    n = pl.program_id(1)
    rows_per_step = acc_ref.shape[0] // n_n

    @pl.when((i == 0) & (n == 0))
    def _():
        ln_ref[...] = jnp.zeros(ln_ref.shape, F32)

    def ln_phase():
        _ln_mod_phase(n, rows_per_step, ln_ref, None, alpha, None, g_ref, beta_ref, modp_ref, 3, 4, xo_ref, ho_ref)

    @pl.when(i < n_t)
    def _():
        y = jnp.dot(a_ref[...], w_ref[0:k1, :], preferred_element_type=F32)
        y = y + jnp.dot(b_ref[...], w_ref[k1:, :], preferred_element_type=F32)
        cols = pl.ds(pl.multiple_of(n * tn, tn), tn)
        acc_ref[:, cols] = alpha * x_ref[...] + mod_ref[0, 2:3, cols] * y
        ln_phase()

    @pl.when(i == n_t)
    def _():
        ln_phase()

    @pl.when((i < n_t) & (n == n_n - 1))
    def _():
        ln_ref[...] = acc_ref[...]


def mm_out_ln(za, zb, w, x2, mod_l, ln_g, ln_b, seq, alpha, *, tm=512, tn=512):
    m, k1 = za.shape
    d = w.shape[1]
    n_n = d // tn
    n_t = m // tm
    rl = tm // n_n
    assert rl % LN_ROWS == 0
    cur = lambda i: jnp.minimum(i, n_t - 1)
    prv = lambda i: jnp.maximum(i - 1, 0)
    col = lambda i, n: jnp.where(i < n_t, n, n_n - 1)
    out_row = pl.BlockSpec((rl, d), lambda i, n: (jnp.where(i == 0, 0, (i - 1) * n_n + n), 0))
    return pl.pallas_call(
        functools.partial(_mm_out_kernel, n_t=n_t, n_n=n_n, tn=tn, k1=k1, alpha=alpha),
        grid=(n_t + 1, n_n),
        in_specs=[
            pl.BlockSpec((tm, k1), lambda i, n: (cur(i), 0)),
            pl.BlockSpec((tm, zb.shape[1]), lambda i, n: (cur(i), 0)),
            pl.BlockSpec((w.shape[0], tn), lambda i, n: (0, col(i, n))),
            pl.BlockSpec((tm, tn), lambda i, n: (cur(i), col(i, n))),
            pl.BlockSpec((1, 6, d), lambda i, n: ((cur(i) * tm) // seq, 0, 0)),
            pl.BlockSpec((1, 6, d), lambda i, n: ((prv(i) * tm) // seq, 0, 0)),
            pl.BlockSpec((1, d), lambda i, n: (0, 0)),
            pl.BlockSpec((1, d), lambda i, n: (0, 0)),
        ],
        out_specs=[out_row, out_row],
        out_shape=[jax.ShapeDtypeStruct((m, d), F32), jax.ShapeDtypeStruct((m, d), BF16)],
        scratch_shapes=[pltpu.VMEM((tm, d), F32), pltpu.VMEM((tm, d), F32)],
        compiler_params=_cparams(("arbitrary", "arbitrary")),
        name="mm_out_ln",
    )(za, zb, w, x2, mod_l, mod_l, ln_g.reshape(1, d), ln_b.reshape(1, d))


def _mlp_kernel(h_ref, xr_ref, modp_ref, modn_ref, w1_ref, w2_ref, g_ref, beta_ref, *rest,
                n_t, n_f, tn2, alpha, emit_h):
    if emit_h:
        xo_ref, ho_ref, acc_ref, ln_ref = rest
    else:
        xo_ref, acc_ref, ln_ref = rest
        ho_ref = None
    i = pl.program_id(0)
    f = pl.program_id(1)
    d = acc_ref.shape[1]
    rows_per_step = acc_ref.shape[0] // n_f

    @pl.when((i == 0) & (f == 0))
    def _():
        ln_ref[...] = jnp.zeros(ln_ref.shape, F32)

    @pl.when((i < n_t) & (f == 0))
    def _():
        acc_ref[...] = jnp.zeros(acc_ref.shape, F32)

    def ln_phase():
        _ln_mod_phase(f, rows_per_step, ln_ref, xr_ref, alpha, modp_ref, g_ref, beta_ref, modn_ref, 0, 1,
                      xo_ref, ho_ref)

    @pl.when(i < n_t)
    def _():
        t = jnp.dot(h_ref[...], w1_ref[...], preferred_element_type=F32)
        t = jnp.maximum(t, 0.0)
        t = (t * t).astype(BF16)
        for nb in range(d // tn2):
            cols = slice(nb * tn2, (nb + 1) * tn2)
            acc_ref[:, cols] += jnp.dot(t, w2_ref[:, cols], preferred_element_type=F32)
        ln_phase()

    @pl.when(i == n_t)
    def _():
        ln_phase()

    @pl.when((i < n_t) & (f == n_f - 1))
    def _():
        ln_ref[...] = acc_ref[...]


def mlp_ln(h, x2, mod_l, mod_next, w1, w2, ln_g, ln_b, seq, alpha, *, tm=512, tf=1024, tn2=512):
    m, d = x2.shape
    dff = w1.shape[1]
    n_f = dff // tf
    n_t = m // tm
    rl = tm // n_f
    assert rl % LN_ROWS == 0
    emit_h = mod_next is not None
    cur = lambda i: jnp.minimum(i, n_t - 1)
    prv = lambda i: jnp.maximum(i - 1, 0)
    chunk = lambda i, f: jnp.where(i < n_t, f, n_f - 1)
    out_row = pl.BlockSpec((rl, d), lambda i, f: (jnp.where(i == 0, 0, (i - 1) * n_f + f), 0))
    modspec = pl.BlockSpec((1, 6, d), lambda i, f: ((prv(i) * tm) // seq, 0, 0))
    f32_out = jax.ShapeDtypeStruct((m, d), F32)
    res = pl.pallas_call(
        functools.partial(_mlp_kernel, n_t=n_t, n_f=n_f, tn2=tn2, alpha=alpha, emit_h=emit_h),
        grid=(n_t + 1, n_f),
        in_specs=[
            pl.BlockSpec((tm, d), lambda i, f: (cur(i), 0), pipeline_mode=pl.Buffered(1)),
            pl.BlockSpec((rl, d), lambda i, f: (prv(i) * n_f + f, 0)),
            modspec,
            modspec,
            pl.BlockSpec((d, tf), lambda i, f: (0, chunk(i, f))),
            pl.BlockSpec((tf, d), lambda i, f: (chunk(i, f), 0)),
            pl.BlockSpec((1, d), lambda i, f: (0, 0)),
            pl.BlockSpec((1, d), lambda i, f: (0, 0)),
        ],
        out_specs=[out_row, out_row] if emit_h else out_row,
        out_shape=[f32_out, jax.ShapeDtypeStruct((m, d), BF16)] if emit_h else f32_out,
        scratch_shapes=[pltpu.VMEM((tm, d), F32), pltpu.VMEM((tm, d), F32)],
        compiler_params=_cparams(("arbitrary", "arbitrary")),
        name="mlp_ln",
    )(h, x2, mod_l, mod_next if emit_h else mod_l, w1, w2, ln_g.reshape(1, d), ln_b.reshape(1, d))
    return res if emit_h else (res, None)


CONV_HALO = 32


def _conv_kernel(av_ref, ag_ref, w_ref, cb_ref, g_ref, beta_ref, o_ref, glu_ref, y_ref, *, ts, rc):
    s = pl.program_id(1)
    ch = o_ref.shape[2]

    @pl.when(s == 0)
    def _():
        glu_ref[:, 0:CONV_HALO, :] = jnp.zeros((ch // 128, CONV_HALO, 128), F32)

    @pl.when(s > 0)
    def _():
        glu_ref[:, 0:CONV_HALO, :] = glu_ref[:, ts:ts + CONV_HALO, :]

    def glu_body(i, carry):
        r0 = pl.multiple_of(i * rc, rc)
        a = av_ref[0, pl.ds(r0, rc), :].astype(F32)
        gt = ag_ref[0, pl.ds(r0, rc), :].astype(F32)
        glu = a * jax.nn.sigmoid(gt)
        for cc in range(ch // 128):
            glu_ref[cc, pl.ds(CONV_HALO + r0, rc), :] = glu[:, cc * 128:(cc + 1) * 128]
        return carry
    lax.fori_loop(0, ts // rc, glu_body, 0)

    off = CONV_HALO - (CONV_K - 1)

    def conv_body(i, carry):
        r0 = pl.multiple_of(i * rc, rc)
        for cc in range(ch // 128):
            lanes = slice(cc * 128, (cc + 1) * 128)
            acc = jnp.zeros((rc, 128), F32)
            for j in range(CONV_K):
                acc = acc + w_ref[j:j + 1, lanes] * glu_ref[cc, pl.ds(r0 + off + j, rc), :]
            y_ref[pl.ds(r0, rc), lanes] = acc + cb_ref[0:1, lanes]
        return carry
    lax.fori_loop(0, ts // rc, conv_body, 0)

    def ln_body(i, carry):
        rows = pl.ds(pl.multiple_of(i * LN_ROWS, LN_ROWS), LN_ROWS)

        def emit(cols, z):
            o_ref[0, rows, cols] = _silu(z).astype(o_ref.dtype)
        _ln_chunk(lambda cols: y_ref[rows, cols], ch, g_ref, beta_ref, emit)
        return carry
    lax.fori_loop(0, ts // LN_ROWS, ln_body, 0, unroll=2)


def conv_branch(u3, conv_w, conv_b, ln_g, ln_b, mix, *, ts=512, rc=32):
    bsz, seq, _ = u3.shape
    return pl.pallas_call(
        functools.partial(_conv_kernel, ts=ts, rc=rc),
        grid=(bsz, seq // ts),
        in_specs=[
            pl.BlockSpec((1, ts, mix), lambda b, s: (b, s, 0)),
            pl.BlockSpec((1, ts, mix), lambda b, s: (b, s, 1)),
            pl.BlockSpec((CONV_K, mix), lambda b, s: (0, 0)),
            pl.BlockSpec((1, mix), lambda b, s: (0, 0)),
            pl.BlockSpec((1, mix), lambda b, s: (0, 0)),
            pl.BlockSpec((1, mix), lambda b, s: (0, 0)),
        ],
        out_specs=pl.BlockSpec((1, ts, mix), lambda b, s: (b, s, 0)),
        out_shape=jax.ShapeDtypeStruct((bsz, seq, mix), BF16),
        scratch_shapes=[pltpu.VMEM((mix // 128, ts + CONV_HALO, 128), F32), pltpu.VMEM((ts, mix), F32)],
        compiler_params=_cparams(("arbitrary", "arbitrary")),
        name="conv_branch",
    )(u3, u3, conv_w, conv_b.reshape(1, mix), ln_g.reshape(1, mix), ln_b.reshape(1, mix))


def _t5_bucket(dist):
    max_exact = NUM_BUCKETS // 2
    nf = jnp.maximum(dist, 1).astype(F32)
    large = max_exact + (jnp.log(nf / max_exact) / math.log(MAX_DISTANCE / max_exact)
                         * (NUM_BUCKETS - max_exact)).astype(jnp.int32)
    large = jnp.minimum(large, NUM_BUCKETS - 1)
    return jnp.where(dist < max_exact, dist, large)


ATT_GROUP = 256
ATT_RES = 16


def _natural_index(dil):
    if dil == 1:
        i = jnp.arange(ATT_GROUP)
        return ATT_GROUP, ATT_RES * (i % ATT_RES) + i // ATT_RES
    if dil == 4:
        i = jnp.arange(HEAD_DIM)
        return HEAD_DIM, (i // 64) * 64 + 4 * (i % 16) + (i % 64) // 16
    assert dil == ATT_RES
    return HEAD_DIM, jnp.arange(HEAD_DIM)


def _bucket_tile(window, dil):
    steps = window // dil
    qr, nat = _natural_index(dil)
    qn = nat[:, None] + qr
    kn = jnp.concatenate([nat, nat + qr])[None, :]
    step = qn - kn
    bucket = _t5_bucket(jnp.clip(step, 0, steps) * dil)
    valid = (step >= 0) & (step <= steps)
    return jnp.where(valid, bucket, -1).astype(jnp.int32)


def _bias_kernel(rb_ref, idx_ref, o_ref, *, heads):
    idx = idx_ref[...]
    for h in range(heads):
        acc = jnp.full(idx.shape, NEG, F32)
        for bk in range(NUM_BUCKETS):
            acc = jnp.where(idx == bk, rb_ref[bk, h], acc)
        o_ref[h] = acc


def attn_bias(rel_bias, window, dil):
    heads = rel_bias.shape[1]
    idx = _bucket_tile(window, dil)
    qr, qr2 = idx.shape
    return pl.pallas_call(
        functools.partial(_bias_kernel, heads=heads),
        grid=(1,),
        in_specs=[
            pl.BlockSpec(memory_space=pltpu.SMEM),
            pl.BlockSpec((qr, qr2), lambda i: (0, 0)),
        ],
        out_specs=pl.BlockSpec((heads, qr, qr2), lambda i: (0, 0, 0)),
        out_shape=jax.ShapeDtypeStruct((heads, qr, qr2), F32),
        compiler_params=_cparams(("arbitrary",)),
        name=f"attn_bias_d{dil}",
    )(rel_bias.astype(F32), idx)


def _group_perm():
    shift = ATT_RES.bit_length() - 1
    row = lax.broadcasted_iota(jnp.int32, (ATT_GROUP, ATT_GROUP), 0)
    col = lax.broadcasted_iota(jnp.int32, (ATT_GROUP, ATT_GROUP), 1)
    src = ((row & (ATT_RES - 1)) << shift) | (row >> shift)
    return jnp.where(col == src, 1.0, 0.0).astype(BF16)


def _attn_kernel(*refs, hb, qr, nqb, has_halo, has_prev, last, scale):
    it = iter(refs)
    q_ref, k_ref, v_ref = next(it), next(it), next(it)
    kp_ref, vp_ref = (next(it), next(it)) if has_halo else (None, None)
    bias_ref = next(it)
    op_ref, lp_ref = (next(it), next(it)) if has_prev else (None, None)
    o_ref = next(it)
    l_ref = None if last else next(it)
    kbuf, vbuf, s_scr, p_scr, m_scr, d_scr, obuf, lbuf = (next(it) for _ in range(8))
    t = pl.program_id(3)
    tq = qr * nqb
    wb = hb * HEAD_DIM
    nt = (((1,), (1,)), ((), ()))
    units = [(h, j) for h in range(hb) for j in range(nqb)]

    q = q_ref[...].reshape(tq, wb)
    kbuf[qr:, :] = k_ref[...].reshape(tq, wb)
    vbuf[qr:, :] = v_ref[...].reshape(tq, wb)
    if has_halo:
        kbuf[0:qr, :] = kp_ref[...].reshape(qr, wb)
        vbuf[0:qr, :] = vp_ref[...].reshape(qr, wb)
        first = t == 0
    else:
        kbuf[0:qr, :] = jnp.zeros((qr, wb), BF16)
        vbuf[0:qr, :] = jnp.zeros((qr, wb), BF16)
        first = t >= 0
    prev_cols = lax.broadcasted_iota(jnp.int32, (qr, 2 * qr), 1) < qr

    for u, (h, j) in enumerate(units):
        lanes = slice(h * HEAD_DIM, (h + 1) * HEAD_DIM)
        s = lax.dot_general(q[j * qr:(j + 1) * qr, lanes], kbuf[j * qr:(j + 2) * qr, lanes], nt,
                            preferred_element_type=F32)
        s = s * scale + bias_ref[h]
        if j == 0:
            s = jnp.where(prev_cols & first, NEG, s)
        s_scr[u] = s

    for u in range(len(units)):
        s = s_scr[u]
        m = jnp.max(s, axis=-1, keepdims=True)
        p = jnp.exp(s - m)
        m_scr[u] = jnp.broadcast_to(m, (qr, HEAD_DIM))
        d_scr[u] = jnp.broadcast_to(jnp.sum(p, axis=-1, keepdims=True), (qr, HEAD_DIM))
        p_scr[u] = p.astype(BF16)

    if has_prev:
        o_prev = op_ref[...].reshape(tq, wb)
        l_prev = lp_ref[...].reshape(tq, wb)
    for u, (h, j) in enumerate(units):
        lanes = slice(h * HEAD_DIM, (h + 1) * HEAD_DIM)
        rows = slice(j * qr, (j + 1) * qr)
        acc = jnp.dot(p_scr[u], vbuf[j * qr:(j + 2) * qr, lanes], preferred_element_type=F32)
        den = d_scr[u]
        o = acc / den
        lse = m_scr[u] + jnp.log(den)
        if has_prev:
            lse0 = l_prev[rows, lanes]
            mx = jnp.maximum(lse0, lse)
            w0 = jnp.exp(lse0 - mx)
            w1 = jnp.exp(lse - mx)
            tot = w0 + w1
            o = (w0 * o_prev[rows, lanes] + w1 * o) / tot
            lse = mx + jnp.log(tot)
        obuf[rows, lanes] = o
        if not last:
            lbuf[rows, lanes] = lse

    if last:
        res = jnp.dot(_group_perm(), obuf[...].astype(BF16), preferred_element_type=F32)
        o_ref[...] = res.astype(o_ref.dtype).reshape(o_ref.shape)
    else:
        o_ref[...] = obuf[...].reshape(o_ref.shape)
        l_ref[...] = lbuf[...].reshape(l_ref.shape)


def attn_pattern(u3, bias, dil, prev, *, mix, last, hb=4):
    bsz, seq, _ = u3.shape
    heads = mix // HEAD_DIM
    ng = seq // ATT_GROUP
    wb = hb * HEAD_DIM
    q0, k0, v0 = (2 * mix) // wb, (3 * mix) // wb, (4 * mix) // wb
    qr = bias.shape[1]
    halo = None
    if dil == 1:
        nqb, n_res, n_t = 1, 1, ng
        view = lambda a: a.reshape(bsz, ng, ATT_GROUP, a.shape[-1])
        cur = lambda c0: pl.BlockSpec((1, 1, ATT_GROUP, wb), lambda g, b, r, t: (b, t, 0, c0 + g))
        halo = lambda c0: pl.BlockSpec((1, 1, ATT_GROUP, wb),
                                       lambda g, b, r, t: (b, jnp.maximum(t - 1, 0), 0, c0 + g))
    elif dil == 4:
        nqb, n_res, gt = 2, 4, 4
        n_t = ng // gt
        view = lambda a: a.reshape(bsz, ng, 4, 4, ATT_RES, a.shape[-1])
        cur = lambda c0: pl.BlockSpec((1, gt, 4, 1, ATT_RES, wb), lambda g, b, r, t: (b, t, 0, r, 0, c0 + g))
        halo = lambda c0: pl.BlockSpec((1, gt // 2, 4, 1, ATT_RES, wb),
                                       lambda g, b, r, t: (b, jnp.maximum(2 * t - 1, 0), 0, r, 0, c0 + g))
    else:
        assert dil == ATT_RES and ng * ATT_RES == 2 * HEAD_DIM
        nqb, n_res, n_t = 2, ATT_RES, 1
        view = lambda a: a.reshape(bsz, ng, ATT_RES, ATT_RES, a.shape[-1])
        cur = lambda c0: pl.BlockSpec((1, ng, 1, ATT_RES, wb), lambda g, b, r, t: (b, 0, r, 0, c0 + g))
    has_halo = halo is not None
    tq = qr * nqb
    uv = view(u3)
    in_specs = [cur(q0), cur(k0), cur(v0)]
    args = [uv, uv, uv]
    if has_halo:
        in_specs += [halo(k0), halo(v0)]
        args += [uv, uv]
    in_specs.append(pl.BlockSpec((hb, qr, 2 * qr), lambda g, b, r, t: (g, 0, 0)))
    args.append(bias)
    if prev is not None:
        in_specs += [cur(0), cur(0)]
        args += [view(prev[0]), view(prev[1])]
    if last:
        assert dil == 1
        out_specs = pl.BlockSpec((1, ATT_GROUP, wb), lambda g, b, r, t: (b, t, g))
        out_shape = jax.ShapeDtypeStruct((bsz, seq, mix), BF16)
    else:
        oshape = jax.eval_shape(view, jax.ShapeDtypeStruct((bsz, seq, mix), F32))
        out_specs = [cur(0), cur(0)]
        out_shape = [oshape, oshape]
    n_u = hb * nqb
    res = pl.pallas_call(
        functools.partial(_attn_kernel, hb=hb, qr=qr, nqb=nqb, has_halo=has_halo, has_prev=prev is not None,
                          last=last, scale=HEAD_DIM ** -0.5),
        grid=(heads // hb, bsz, n_res, n_t),
        in_specs=in_specs,
        out_specs=out_specs,
        out_shape=out_shape,
        scratch_shapes=[
            pltpu.VMEM((tq + qr, wb), BF16), pltpu.VMEM((tq + qr, wb), BF16),
            pltpu.VMEM((n_u, qr, 2 * qr), F32), pltpu.VMEM((n_u, qr, 2 * qr), BF16),
            pltpu.VMEM((n_u, qr, HEAD_DIM), F32), pltpu.VMEM((n_u, qr, HEAD_DIM), F32),
            pltpu.VMEM((tq, wb), F32), pltpu.VMEM((tq, wb), F32),
        ],
        compiler_params=_cparams(("arbitrary",) * 4),
        name=f"attn_d{dil}",
    )(*args)
    if last:
        return res
    return res[0].reshape(bsz, seq, mix), res[1].reshape(bsz, seq, mix)


def dilated_attention(u3, rel_bias, mix):
    order = sorted(DSW_PATTERNS, key=lambda wd: wd[1] == 1)
    prev = None
    for gi, (window, dil) in enumerate(order):
        assert window // dil == HEAD_DIM
        prev = attn_pattern(u3, attn_bias(rel_bias, window, dil), dil, prev, mix=mix, last=gi == len(order) - 1)
    return prev


def _split3(v):
    hi = v.astype(BF16)
    r1 = v - hi.astype(F32)
    mid = r1.astype(BF16)
    lo = (r1 - mid.astype(F32)).astype(BF16)
    return hi, mid, lo


def _scaled(v, expo, mask):
    return jnp.where(mask, v * jnp.exp(jnp.where(mask, expo, 0.0)), 0.0)


def _hgrn_kernel(cq_ref, cf_ref, ci_ref, cg_ref, lb_ref, ng_ref, o_ref, st_ref, q_s, kk_s, b_s, a_s,
                 *, hb, ts, layer):
    s = pl.program_id(2)
    c_len = HGRN_CHUNK
    sub = 16
    nt = (((1,), (1,)), ((), ()))
    tn_ = (((0,), (0,)), ((), ()))

    @pl.when(s == 0)
    def _():
        st_ref[...] = jnp.zeros(st_ref.shape, F32)

    row = lax.broadcasted_iota(jnp.int32, (c_len, c_len), 0)
    col = lax.broadcasted_iota(jnp.int32, (c_len, c_len), 1)
    tri = jnp.where(row >= col, 1.0, 0.0).astype(BF16)
    r64 = lax.broadcasted_iota(jnp.int32, (c_len, HEAD_DIM), 0)
    half = c_len // 2
    mask_b = (((row >= sub) & (row < half) & (col < sub))
              | ((row >= half + sub) & (col >= half) & (col < half + sub)))
    lrow = lax.broadcasted_iota(jnp.int32, (lb_ref.shape[0], HEAD_DIM), 0)

    def chunk_body(ci, carry):
        r0 = pl.multiple_of(ci * c_len, c_len)
        rows = pl.ds(r0, c_len)
        head_lanes = [slice(h * HEAD_DIM, (h + 1) * HEAD_DIM) for h in range(hb)]
        for h, lanes in enumerate(head_lanes):
            lg = lb_ref[:, lanes]
            pe = jnp.exp(lg - jnp.max(lg, axis=0, keepdims=True))
            lb = (jnp.sum(jnp.where((lrow >= 1) & (lrow <= layer), pe, 0.0), axis=0, keepdims=True)
                  / jnp.sum(pe, axis=0, keepdims=True))
            f = lb + (1.0 - lb) * jax.nn.sigmoid(cf_ref[0, rows, lanes].astype(F32))
            hi, mid, lo = _split3(jnp.log(f))
            q_s[h] = _silu(cq_ref[0, rows, lanes].astype(F32))
            kk_s[h] = 1.0 - f
            b_s[h] = (jnp.dot(tri, hi, preferred_element_type=F32)
                      + jnp.dot(tri, mid, preferred_element_type=F32)
                      + jnp.dot(tri, lo, preferred_element_type=F32))

        for h in range(hb):
            q, kk, b = q_s[h], kk_s[h], b_s[h]
            b_a = b_s[h, half - 1:half, :]
            qa = _scaled(q, b - b_a, r64 >= half)
            ka = _scaled(kk, b_a - b, r64 < half)
            attn = lax.dot_general(qa.astype(BF16), ka.astype(BF16), nt, preferred_element_type=F32)
            b_r = jnp.where(r64 < half, b_s[h, sub - 1:sub, :], b_s[h, half + sub - 1:half + sub, :])
            qsel = ((r64 >= sub) & (r64 < half)) | (r64 >= half + sub)
            ksel = (r64 < sub) | ((r64 >= half) & (r64 < half + sub))
            qbm = _scaled(q, b - b_r, qsel)
            kbm = _scaled(kk, b_r - b, ksel)
            attn_b = lax.dot_general(qbm.astype(BF16), kbm.astype(BF16), nt, preferred_element_type=F32)
            a_s[h, :, 0:c_len] = attn + jnp.where(mask_b, attn_b, 0.0)

        for h in range(hb):
            for jb in range(c_len // sub):
                blk = slice(jb * sub, (jb + 1) * sub)
                qt = q_s[h, blk, :]
                bt = b_s[h, blk, :]
                for si in range(sub):
                    r = jb * sub + si
                    e = jnp.exp(bt - b_s[h, r:r + 1, :])
                    a_s[h, blk, r:r + 1] = jnp.sum(qt * kk_s[h, r:r + 1, :] * e, axis=-1, keepdims=True)

        for h, lanes in enumerate(head_lanes):
            q, kk, b = q_s[h], kk_s[h], b_s[h]
            v = ci_ref[0, rows, lanes]
            b_last = b_s[h, c_len - 1:c_len, :]
            st_t = st_ref[h]
            inter = lax.dot_general((q * jnp.exp(b)).astype(BF16), st_t.astype(BF16), nt,
                                    preferred_element_type=F32)
            attn = jnp.where(row >= col, a_s[h, :, 0:c_len], 0.0)
            o = inter + jnp.dot(attn.astype(BF16), v, preferred_element_type=F32)
            kd = (kk * jnp.exp(b_last - b)).astype(BF16)
            st_ref[h] = st_t * jnp.exp(b_last) + lax.dot_general(v, kd, tn_, preferred_element_type=F32)
            ms = jnp.mean(o * o, axis=-1, keepdims=True)
            o = o * lax.rsqrt(ms + LN_EPS) * ng_ref[0:1, lanes]
            o = o * _silu(cg_ref[0, rows, lanes].astype(F32))
            o_ref[0, rows, lanes] = o.astype(o_ref.dtype)
        return carry
    lax.fori_loop(0, ts // c_len, chunk_body, 0)


def hgrn_branch(u3, lb_logits, layer, norm_g, mix, *, hb=4, ts=256):
    bsz, seq, _ = u3.shape
    heads = mix // HEAD_DIM
    wb = hb * HEAD_DIM
    nb = mix // wb

    def col(k):
        return pl.BlockSpec((1, ts, wb), lambda b, g, s: (b, s, k * nb + g))

    vec = pl.BlockSpec((1, wb), lambda b, g, s: (0, g))
    return pl.pallas_call(
        functools.partial(_hgrn_kernel, hb=hb, ts=ts, layer=layer),
        grid=(bsz, heads // hb, seq // ts),
        in_specs=[col(0), col(1), col(2), col(3),
                  pl.BlockSpec((lb_logits.shape[0], wb), lambda b, g, s: (0, g)), vec],
        out_specs=pl.BlockSpec((1, ts, wb), lambda b, g, s: (b, s, g)),
        out_shape=jax.ShapeDtypeStruct((bsz, seq, mix), BF16),
        scratch_shapes=[pltpu.VMEM((hb, HEAD_DIM, HEAD_DIM), F32)]
        + [pltpu.VMEM((hb, HGRN_CHUNK, HEAD_DIM), F32)] * 4,
        compiler_params=_cparams(("arbitrary",) * 3),
        name="hgrn",
    )(u3, u3, u3, u3, lb_logits.astype(F32), norm_g.reshape(1, mix))


POOL_HALO = 16


def _pool_kernel(dp_ref, pw_ref, ps_ref, o_ref, x_ref, p_ref, *, ts, rc):
    s = pl.program_id(1)
    ch = o_ref.shape[2]
    grp = ch // len(POOL_WINDOWS)

    @pl.when(s == 0)
    def _():
        x_ref[:, 0:POOL_HALO, :] = jnp.zeros((ch // 128, POOL_HALO, 128), F32)

    @pl.when(s > 0)
    def _():
        x_ref[:, 0:POOL_HALO, :] = x_ref[:, ts:ts + POOL_HALO, :]

    def load_body(i, carry):
        r0 = pl.multiple_of(i * rc, rc)
        xv = dp_ref[0, pl.ds(r0, rc), :].astype(F32)
        for cc in range(ch // 128):
            x_ref[cc, pl.ds(POOL_HALO + r0, rc), :] = xv[:, cc * 128:(cc + 1) * 128]
        return carry
    lax.fori_loop(0, ts // rc, load_body, 0)

    def pool_body(i, carry):
        r0 = pl.multiple_of(i * rc, rc)
        pos = s * ts + r0 + lax.broadcasted_iota(jnp.int32, (rc, 1), 0)
        for cc in range(ch // 128):
            w = POOL_WINDOWS[(cc * 128) // grp]
            cur = x_ref[cc, pl.ds(POOL_HALO + r0, rc), :]
            tot = cur
            for j in range(1, w):
                tot = tot + x_ref[cc, pl.ds(POOL_HALO + r0 - j, rc), :]
            cnt = jnp.minimum(pos + 1, w).astype(F32)
            p_ref[pl.ds(r0, rc), cc * 128:(cc + 1) * 128] = (tot / cnt - cur).astype(BF16)
        return carry
    lax.fori_loop(0, ts // rc, pool_body, 0)

    for gi in range(len(POOL_WINDOWS)):
        lanes = slice(gi * grp, (gi + 1) * grp)
        y = jnp.dot(p_ref[:, lanes], pw_ref[gi], preferred_element_type=F32)
        o_ref[0, :, lanes] = (y * ps_ref[0:1, lanes]).astype(o_ref.dtype)


def pool_branch(u3, pool_w, pool_scale, mix, *, ts=512, rc=32):
    bsz, seq, _ = u3.shape
    ng, grp, _ = pool_w.shape
    return pl.pallas_call(
        functools.partial(_pool_kernel, ts=ts, rc=rc),
        grid=(bsz, seq // ts),
        in_specs=[
            pl.BlockSpec((1, ts, mix), lambda b, s: (b, s, 4)),
            pl.BlockSpec((ng, grp, grp), lambda b, s: (0, 0, 0)),
            pl.BlockSpec((1, mix), lambda b, s: (0, 0)),
        ],
        out_specs=pl.BlockSpec((1, ts, mix), lambda b, s: (b, s, 0)),
        out_shape=jax.ShapeDtypeStruct((bsz, seq, mix), BF16),
        scratch_shapes=[pltpu.VMEM((mix // 128, ts + POOL_HALO, 128), F32), pltpu.VMEM((ts, mix), BF16)],
        compiler_params=_cparams(("arbitrary", "arbitrary")),
        name="pool_branch",
    )(u3, pool_w.astype(BF16), pool_scale.reshape(1, mix))


def kernel(x, c, ada_w, ada_b, w_in, w_out, ln_g, ln_b, mlp_w1, mlp_w2, conv_w, conv_b, conv_ln_g, conv_ln_b,
           rel_bias, hgrn_lb_logits, hgrn_norm_g, pool_w, pool_scale):
    bsz, seq, d = x.shape
    depth = ada_w.shape[0]
    mix = d // 2
    alpha = (2.0 * depth) ** 0.25
    m = bsz * seq

    mod = adaln_mod(c, ada_w, ada_b)
    x2 = x.reshape(m, d)
    h = modulate(x2, mod[0], seq)
    for l in range(depth):
        u, (w_out_b, w1_b, w2_b) = mm_in(
            h, cast_layer_bf16(w_in, l), perm_from=2 * mix if l % 2 == 0 else None,
            side_casts=[(w_out, l, 32), (mlp_w1, l, 32), (mlp_w2, l, 128)])
        u3 = u.reshape(bsz, seq, 5 * mix)
        if l % 2 == 0:
            e = l // 2
            za = conv_branch(u3, conv_w[e], conv_b[e], conv_ln_g[e], conv_ln_b[e], mix)
            zb = dilated_attention(u3, rel_bias, mix)
        else:
            o = l // 2
            za = hgrn_branch(u3, hgrn_lb_logits, l, hgrn_norm_g[o], mix)
            zb = pool_branch(u3, pool_w[o], pool_scale[o], mix)
        x2, h2 = mm_out_ln(za.reshape(m, mix), zb.reshape(m, mix), w_out_b, x2, mod[l],
                           ln_g[l, 0], ln_b[l, 0], seq, alpha)
        x2, h = mlp_ln(h2, x2, mod[l], mod[l + 1] if l + 1 < depth else None,
                       w1_b, w2_b, ln_g[l, 1], ln_b[l, 1], seq, alpha)
    return x2.reshape(bsz, seq, d)
```

```python
import functools
import math

import jax
import jax.numpy as jnp
from jax import lax
from jax.experimental import pallas as pl
from jax.experimental.pallas import tpu as pltpu

F32 = jnp.float32
BF16 = jnp.bfloat16

HEAD_DIM = 128
CONV_K = 31
DSW_PATTERNS = ((128, 1), (512, 4), (2048, 16))
NUM_BUCKETS = 32
MAX_DISTANCE = 2048
HGRN_CHUNK = 64
POOL_WINDOWS = (2, 4, 8, 16)
LN_EPS = 1e-5
NEG = -1e30

V7X_VMEM_BYTES = 64 * 1024 * 1024
VMEM_LIMIT = V7X_VMEM_BYTES - 2 * 1024 * 1024


def _cparams(sem):
    return pltpu.CompilerParams(dimension_semantics=sem, vmem_limit_bytes=VMEM_LIMIT)


def _silu(v):
    return v * jax.nn.sigmoid(v)


def _mod_kernel(c_ref, w_ref, b_ref, o_ref):
    cs = _silu(c_ref[...]).astype(BF16)
    w = w_ref[0].astype(BF16)
    o_ref[0] = jnp.dot(cs, w, preferred_element_type=F32) + b_ref[0]


def adaln_mod(c, ada_w, ada_b, *, tn=512):
    nl, d, n6 = ada_w.shape
    bsz = c.shape[0]
    rows = 8
    c8 = jnp.zeros((rows, d), F32).at[:bsz].set(c)
    out = pl.pallas_call(
        _mod_kernel,
        grid=(nl, n6 // tn),
        in_specs=[
            pl.BlockSpec((rows, d), lambda l, j: (0, 0)),
            pl.BlockSpec((1, d, tn), lambda l, j: (l, 0, j)),
            pl.BlockSpec((1, 1, tn), lambda l, j: (l, 0, j)),
        ],
        out_specs=pl.BlockSpec((1, rows, tn), lambda l, j: (l, 0, j)),
        out_shape=jax.ShapeDtypeStruct((nl, rows, n6), F32),
        compiler_params=_cparams(("arbitrary", "arbitrary")),
        name="adaln_mod",
    )(c8, ada_w, ada_b.reshape(nl, 1, n6))
    return out[:, :bsz].reshape(nl, bsz, 6, d)


LN_ROWS = 16
LN_COLS = 512


def _ln_chunks(chunks, d, g_ref, b_ref):
    col_slices = [slice(c * LN_COLS, (c + 1) * LN_COLS) for c in range(d // LN_COLS)]
    mus = []
    for load, _ in chunks:
        tot = load(col_slices[0])
        for cols in col_slices[1:]:
            tot = tot + load(cols)
        mus.append(jnp.sum(tot, axis=-1, keepdims=True) * (1.0 / d))
    rstds = []
    for (load, _), mu in zip(chunks, mus):
        sq = None
        for cols in col_slices:
            dv = load(cols) - mu
            sq = dv * dv if sq is None else sq + dv * dv
        rstds.append(lax.rsqrt(jnp.sum(sq, axis=-1, keepdims=True) * (1.0 / d) + LN_EPS))
    for (load, emit), mu, rstd in zip(chunks, mus, rstds):
        for cols in col_slices:
            emit(cols, (load(cols) - mu) * rstd * g_ref[:, cols] + b_ref[:, cols])


def _ln_chunk(load, d, g_ref, b_ref, emit):
    _ln_chunks([(load, emit)], d, g_ref, b_ref)


def _ln_mod_phase(step, rows_per_step, src_ref, xr_ref, alpha, gate_ref, g_ref, b_ref, mod_ref, sh_row, sc_row,
                  xo_ref, ho_ref):
    d = xo_ref.shape[1]
    chunks = []
    for k in range(rows_per_step // LN_ROWS):
        src_rows = pl.ds(pl.multiple_of(step * rows_per_step + k * LN_ROWS, LN_ROWS), LN_ROWS)
        out_rows = slice(k * LN_ROWS, (k + 1) * LN_ROWS)
        if xr_ref is not None:
            for c in range(d // LN_COLS):
                cols = slice(c * LN_COLS, (c + 1) * LN_COLS)
                xo_ref[out_rows, cols] = alpha * xr_ref[out_rows, cols] + gate_ref[0, 5:6, cols] * src_ref[src_rows, cols]
            load = lambda cols, out_rows=out_rows: xo_ref[out_rows, cols]
        else:
            load = lambda cols, src_rows=src_rows: src_ref[src_rows, cols]

        def emit(cols, y, out_rows=out_rows):
            xo_ref[out_rows, cols] = y
            if ho_ref is not None:
                h = y * (1.0 + mod_ref[0, sc_row:sc_row + 1, cols]) + mod_ref[0, sh_row:sh_row + 1, cols]
                ho_ref[out_rows, cols] = h.astype(BF16)
        chunks.append((load, emit))
    _ln_chunks(chunks, d, g_ref, b_ref)


def _cast_kernel(w_ref, o_ref):
    o_ref[...] = w_ref[0].astype(o_ref.dtype)


def cast_layer_bf16(w_stack, layer, *, tr=512, tc=2048):
    _, rows, cols = w_stack.shape
    tc = min(tc, cols)
    return pl.pallas_call(
        _cast_kernel,
        grid=(rows // tr, cols // tc),
        in_specs=[pl.BlockSpec((1, tr, tc), lambda i, j: (layer, i, j))],
        out_specs=pl.BlockSpec((tr, tc), lambda i, j: (i, j)),
        out_shape=jax.ShapeDtypeStruct((rows, cols), BF16),
        compiler_params=_cparams(("arbitrary", "arbitrary")),
        name="cast_bf16",
    )(w_stack)


def _modulate_kernel(x_ref, mod_ref, o_ref):
    o_ref[...] = (x_ref[...] * (1.0 + mod_ref[0, 1:2, :]) + mod_ref[0, 0:1, :]).astype(o_ref.dtype)


def modulate(x2, mod_l, seq, *, tm=256):
    m, d = x2.shape
    return pl.pallas_call(
        _modulate_kernel,
        grid=(m // tm,),
        in_specs=[
            pl.BlockSpec((tm, d), lambda i: (i, 0)),
            pl.BlockSpec((1, 6, d), lambda i: ((i * tm) // seq, 0, 0)),
        ],
        out_specs=pl.BlockSpec((tm, d), lambda i: (i, 0)),
        out_shape=jax.ShapeDtypeStruct((m, d), BF16),
        compiler_params=_cparams(("arbitrary",)),
        name="modulate",
    )(x2, mod_l)


def _side_cast_specs(side_casts, grid):
    def linear(*ids):
        step = ids[0]
        for size, idx in zip(grid[1:], ids[1:]):
            step = step * size + idx
        return step

    in_specs, out_specs, out_shapes, n_blocks = [], [], [], []
    for arr, layer, br in side_casts:
        _, rows, cols = arr.shape
        nb = rows // br
        assert rows % br == 0 and nb <= math.prod(grid)
        blk = lambda *ids, nb=nb: jnp.minimum(linear(*ids), nb - 1)
        in_specs.append(pl.BlockSpec((1, br, cols), lambda *ids, layer=layer, blk=blk: (layer, blk(*ids), 0)))
        out_specs.append(pl.BlockSpec((br, cols), lambda *ids, blk=blk: (blk(*ids), 0)))
        out_shapes.append(jax.ShapeDtypeStruct((rows, cols), BF16))
        n_blocks.append(nb)
    return in_specs, out_specs, out_shapes, tuple(n_blocks), [a for a, _, _ in side_casts]


def _side_cast_body(grid, side_in, side_out, n_blocks):
    step = pl.program_id(0)
    for ax in range(1, len(grid)):
        step = step * grid[ax] + pl.program_id(ax)
    for src, dst, nb in zip(side_in, side_out, n_blocks):
        @pl.when(step < nb)
        def _(src=src, dst=dst):
            dst[...] = src[0].astype(dst.dtype)


def _mm_in_kernel(h_ref, w_ref, o_ref, *scratch, perm_tile):
    if perm_tile is None:
        o_ref[...] = jnp.dot(h_ref[...], w_ref[...], preferred_element_type=F32).astype(o_ref.dtype)
        return
    hp_ref, = scratch
    j = pl.program_id(1)

    @pl.when(j == 0)
    def _():
        perm = _group_perm()
        for g in range(h_ref.shape[0] // ATT_GROUP):
            rows = slice(g * ATT_GROUP, (g + 1) * ATT_GROUP)
            hp_ref[rows, :] = jnp.dot(perm, h_ref[rows, :], preferred_element_type=F32).astype(BF16)

    @pl.when(j < perm_tile)
    def _():
        o_ref[...] = jnp.dot(h_ref[...], w_ref[...], preferred_element_type=F32).astype(o_ref.dtype)

    @pl.when(j >= perm_tile)
    def _():
        o_ref[...] = jnp.dot(hp_ref[...], w_ref[...], preferred_element_type=F32).astype(o_ref.dtype)


def mm_in(h, w, *, perm_from=None, tm=1024, tn=1024):
    m, d = h.shape
    n = w.shape[1]
    perm_tile = None
    scratch = []
    if perm_from is not None:
        assert perm_from % tn == 0 and tm % ATT_GROUP == 0
        perm_tile = perm_from // tn
        scratch = [pltpu.VMEM((tm, d), BF16)]
    return pl.pallas_call(
        functools.partial(_mm_in_kernel, perm_tile=perm_tile),
        grid=(m // tm, n // tn),
        in_specs=[
            pl.BlockSpec((tm, d), lambda i, j: (i, 0)),
            pl.BlockSpec((d, tn), lambda i, j: (0, j)),
        ],
        out_specs=pl.BlockSpec((tm, tn), lambda i, j: (i, j)),
        out_shape=jax.ShapeDtypeStruct((m, n), BF16),
        scratch_shapes=scratch,
        compiler_params=_cparams(("arbitrary", "arbitrary")),
        name="mm_in",
    )(h, w)


def _mm_out_kernel(a_ref, b_ref, w_ref, x_ref, mod_ref, modp_ref, g_ref, beta_ref, xo_ref, ho_ref, acc_ref, ln_ref,
                   *, n_t, n_n, tn, k1, alpha):
    i = pl.program_id(0)
    n = pl.program_id(1)
    rows_per_step = acc_ref.shape[0] // n_n

    @pl.when((i == 0) & (n == 0))
    def _():
        ln_ref[...] = jnp.zeros(ln_ref.shape, F32)

    def ln_phase():
        _ln_mod_phase(n, rows_per_step, ln_ref, None, alpha, None, g_ref, beta_ref, modp_ref, 3, 4, xo_ref, ho_ref)

    @pl.when(i < n_t)
    def _():
        y = jnp.dot(a_ref[...], w_ref[0:k1, :], preferred_element_type=F32)
        y = y + jnp.dot(b_ref[...], w_ref[k1:, :], preferred_element_type=F32)
        cols = pl.ds(pl.multiple_of(n * tn, tn), tn)
        acc_ref[:, cols] = alpha * x_ref[...] + mod_ref[0, 2:3, cols] * y
        ln_phase()

    @pl.when(i == n_t)
    def _():
        ln_phase()

    @pl.when((i < n_t) & (n == n_n - 1))
    def _():
        ln_ref[...] = acc_ref[...]


def mm_out_ln(za, zb, w, x2, mod_l, ln_g, ln_b, seq, alpha, *, tm=512, tn=512):
    m, k1 = za.shape
    d = w.shape[1]
    n_n = d // tn
    n_t = m // tm
    rl = tm // n_n
    assert rl % LN_ROWS == 0
    cur = lambda i: jnp.minimum(i, n_t - 1)
    prv = lambda i: jnp.maximum(i - 1, 0)
    col = lambda i, n: jnp.where(i < n_t, n, n_n - 1)
    out_row = pl.BlockSpec((rl, d), lambda i, n: (jnp.where(i == 0, 0, (i - 1) * n_n + n), 0))
    return pl.pallas_call(
        functools.partial(_mm_out_kernel, n_t=n_t, n_n=n_n, tn=tn, k1=k1, alpha=alpha),
        grid=(n_t + 1, n_n),
        in_specs=[
            pl.BlockSpec((tm, k1), lambda i, n: (cur(i), 0)),
            pl.BlockSpec((tm, zb.shape[1]), lambda i, n: (cur(i), 0)),
            pl.BlockSpec((w.shape[0], tn), lambda i, n: (0, col(i, n))),
            pl.BlockSpec((tm, tn), lambda i, n: (cur(i), col(i, n))),
            pl.BlockSpec((1, 6, d), lambda i, n: ((cur(i) * tm) // seq, 0, 0)),
            pl.BlockSpec((1, 6, d), lambda i, n: ((prv(i) * tm) // seq, 0, 0)),
            pl.BlockSpec((1, d), lambda i, n: (0, 0)),
            pl.BlockSpec((1, d), lambda i, n: (0, 0)),
        ],
        out_specs=[out_row, out_row],
        out_shape=[jax.ShapeDtypeStruct((m, d), F32), jax.ShapeDtypeStruct((m, d), BF16)],
        scratch_shapes=[pltpu.VMEM((tm, d), F32), pltpu.VMEM((tm, d), F32)],
        compiler_params=_cparams(("arbitrary", "arbitrary")),
        name="mm_out_ln",
    )(za, zb, w, x2, mod_l, mod_l, ln_g.reshape(1, d), ln_b.reshape(1, d))


def _mlp_kernel(h_ref, xr_ref, modp_ref, modn_ref, w1_ref, w2_ref, g_ref, beta_ref, *rest,
                n_t, n_f, tn2, alpha, emit_h):
    if emit_h:
        xo_ref, ho_ref, acc_ref, ln_ref = rest
    else:
        xo_ref, acc_ref, ln_ref = rest
        ho_ref = None
    i = pl.program_id(0)
    f = pl.program_id(1)
    d = acc_ref.shape[1]
    rows_per_step = acc_ref.shape[0] // n_f

    @pl.when((i == 0) & (f == 0))
    def _():
        ln_ref[...] = jnp.zeros(ln_ref.shape, F32)

    @pl.when((i < n_t) & (f == 0))
    def _():
        acc_ref[...] = jnp.zeros(acc_ref.shape, F32)

    def ln_phase():
        _ln_mod_phase(f, rows_per_step, ln_ref, xr_ref, alpha, modp_ref, g_ref, beta_ref, modn_ref, 0, 1,
                      xo_ref, ho_ref)

    @pl.when(i < n_t)
    def _():
        t = jnp.dot(h_ref[...], w1_ref[...], preferred_element_type=F32)
        t = jnp.maximum(t, 0.0)
        t = (t * t).astype(BF16)
        for nb in range(d // tn2):
            cols = slice(nb * tn2, (nb + 1) * tn2)
            acc_ref[:, cols] += jnp.dot(t, w2_ref[:, cols], preferred_element_type=F32)
        ln_phase()

    @pl.when(i == n_t)
    def _():
        ln_phase()

    @pl.when((i < n_t) & (f == n_f - 1))
    def _():
        ln_ref[...] = acc_ref[...]


def mlp_ln(h, x2, mod_l, mod_next, w1, w2, ln_g, ln_b, seq, alpha, *, tm=512, tf=1024, tn2=512):
    m, d = x2.shape
    dff = w1.shape[1]
    n_f = dff // tf
    n_t = m // tm
    rl = tm // n_f
    assert rl % LN_ROWS == 0
    emit_h = mod_next is not None
    cur = lambda i: jnp.minimum(i, n_t - 1)
    prv = lambda i: jnp.maximum(i - 1, 0)
    chunk = lambda i, f: jnp.where(i < n_t, f, n_f - 1)
    out_row = pl.BlockSpec((rl, d), lambda i, f: (jnp.where(i == 0, 0, (i - 1) * n_f + f), 0))
    modspec = pl.BlockSpec((1, 6, d), lambda i, f: ((prv(i) * tm) // seq, 0, 0))
    f32_out = jax.ShapeDtypeStruct((m, d), F32)
    res = pl.pallas_call(
        functools.partial(_mlp_kernel, n_t=n_t, n_f=n_f, tn2=tn2, alpha=alpha, emit_h=emit_h),
        grid=(n_t + 1, n_f),
        in_specs=[
            pl.BlockSpec((tm, d), lambda i, f: (cur(i), 0), pipeline_mode=pl.Buffered(1)),
            pl.BlockSpec((rl, d), lambda i, f: (prv(i) * n_f + f, 0)),
            modspec,
            modspec,
            pl.BlockSpec((d, tf), lambda i, f: (0, chunk(i, f))),
            pl.BlockSpec((tf, d), lambda i, f: (chunk(i, f), 0)),
            pl.BlockSpec((1, d), lambda i, f: (0, 0)),
            pl.BlockSpec((1, d), lambda i, f: (0, 0)),
        ],
        out_specs=[out_row, out_row] if emit_h else out_row,
        out_shape=[f32_out, jax.ShapeDtypeStruct((m, d), BF16)] if emit_h else f32_out,
        scratch_shapes=[pltpu.VMEM((tm, d), F32), pltpu.VMEM((tm, d), F32)],
        compiler_params=_cparams(("arbitrary", "arbitrary")),
        name="mlp_ln",
    )(h, x2, mod_l, mod_next if emit_h else mod_l, w1, w2, ln_g.reshape(1, d), ln_b.reshape(1, d))
    return res if emit_h else (res, None)


CONV_HALO = 32


def _conv_kernel(*refs, ts, rc, grid, side_blocks):
    n_side = len(side_blocks)
    av_ref, ag_ref, w_ref, cb_ref, g_ref, beta_ref = refs[:6]
    side_in = refs[6:6 + n_side]
    o_ref = refs[6 + n_side]
    side_out = refs[7 + n_side:7 + 2 * n_side]
    glu_ref, y_ref = refs[7 + 2 * n_side:]
    _side_cast_body(grid, side_in, side_out, side_blocks)
    s = pl.program_id(1)
    ch = o_ref.shape[2]

    @pl.when(s == 0)
    def _():
        glu_ref[:, 0:CONV_HALO, :] = jnp.zeros((ch // 128, CONV_HALO, 128), F32)

    @pl.when(s > 0)
    def _():
        glu_ref[:, 0:CONV_HALO, :] = glu_ref[:, ts:ts + CONV_HALO, :]

    def glu_body(i, carry):
        r0 = pl.multiple_of(i * rc, rc)
        a = av_ref[0, pl.ds(r0, rc), :].astype(F32)
        gt = ag_ref[0, pl.ds(r0, rc), :].astype(F32)
        glu = a * jax.nn.sigmoid(gt)
        for cc in range(ch // 128):
            glu_ref[cc, pl.ds(CONV_HALO + r0, rc), :] = glu[:, cc * 128:(cc + 1) * 128]
        return carry
    lax.fori_loop(0, ts // rc, glu_body, 0)

    off = CONV_HALO - (CONV_K - 1)

    def conv_body(i, carry):
        r0 = pl.multiple_of(i * rc, rc)
        for cc in range(ch // 128):
            lanes = slice(cc * 128, (cc + 1) * 128)
            acc = jnp.zeros((rc, 128), F32)
            for j in range(CONV_K):
                acc = acc + w_ref[j:j + 1, lanes] * glu_ref[cc, pl.ds(r0 + off + j, rc), :]
            y_ref[pl.ds(r0, rc), lanes] = acc + cb_ref[0:1, lanes]
        return carry
    lax.fori_loop(0, ts // rc, conv_body, 0)

    def ln_body(i, carry):
        rows = pl.ds(pl.multiple_of(i * LN_ROWS, LN_ROWS), LN_ROWS)

        def emit(cols, z):
            o_ref[0, rows, cols] = _silu(z).astype(o_ref.dtype)
        _ln_chunk(lambda cols: y_ref[rows, cols], ch, g_ref, beta_ref, emit)
        return carry
    lax.fori_loop(0, ts // LN_ROWS, ln_body, 0, unroll=2)


def conv_branch(u3, conv_w, conv_b, ln_g, ln_b, mix, *, side_casts=(), ts=256, rc=32):
    bsz, seq, _ = u3.shape
    grid = (bsz, seq // ts)
    s_in, s_out, s_shapes, side_blocks, s_args = _side_cast_specs(side_casts, grid)
    res = pl.pallas_call(
        functools.partial(_conv_kernel, ts=ts, rc=rc, grid=grid, side_blocks=side_blocks),
        grid=grid,
        in_specs=[
            pl.BlockSpec((1, ts, mix), lambda b, s: (b, s, 0)),
            pl.BlockSpec((1, ts, mix), lambda b, s: (b, s, 1)),
            pl.BlockSpec((CONV_K, mix), lambda b, s: (0, 0)),
            pl.BlockSpec((1, mix), lambda b, s: (0, 0)),
            pl.BlockSpec((1, mix), lambda b, s: (0, 0)),
            pl.BlockSpec((1, mix), lambda b, s: (0, 0)),
        ] + s_in,
        out_specs=[pl.BlockSpec((1, ts, mix), lambda b, s: (b, s, 0))] + s_out,
        out_shape=[jax.ShapeDtypeStruct((bsz, seq, mix), BF16)] + s_shapes,
        scratch_shapes=[pltpu.VMEM((mix // 128, ts + CONV_HALO, 128), F32), pltpu.VMEM((ts, mix), F32)],
        compiler_params=_cparams(("arbitrary", "arbitrary")),
        name="conv_branch",
    )(u3, u3, conv_w, conv_b.reshape(1, mix), ln_g.reshape(1, mix), ln_b.reshape(1, mix), *s_args)
    return res[0], list(res[1:])


def _t5_bucket(dist):
    max_exact = NUM_BUCKETS // 2
    nf = jnp.maximum(dist, 1).astype(F32)
    large = max_exact + (jnp.log(nf / max_exact) / math.log(MAX_DISTANCE / max_exact)
                         * (NUM_BUCKETS - max_exact)).astype(jnp.int32)
    large = jnp.minimum(large, NUM_BUCKETS - 1)
    return jnp.where(dist < max_exact, dist, large)


ATT_GROUP = 256
ATT_RES = 16


def _natural_index(dil):
    if dil == 1:
        i = jnp.arange(ATT_GROUP)
        return ATT_GROUP, ATT_RES * (i % ATT_RES) + i // ATT_RES
    if dil == 4:
        i = jnp.arange(HEAD_DIM)
        return HEAD_DIM, (i // 64) * 64 + 4 * (i % 16) + (i % 64) // 16
    assert dil == ATT_RES
    return HEAD_DIM, jnp.arange(HEAD_DIM)


def _bucket_tile(window, dil):
    steps = window // dil
    qr, nat = _natural_index(dil)
    qn = nat[:, None] + qr
    kn = jnp.concatenate([nat, nat + qr])[None, :]
    step = qn - kn
    bucket = _t5_bucket(jnp.clip(step, 0, steps) * dil)
    valid = (step >= 0) & (step <= steps)
    return jnp.where(valid, bucket, -1).astype(jnp.int32)


def _bias_kernel(rb_ref, idx_ref, o_ref, *, heads):
    idx = idx_ref[...]
    for h in range(heads):
        acc = jnp.full(idx.shape, NEG, F32)
        for bk in range(NUM_BUCKETS):
            acc = jnp.where(idx == bk, rb_ref[bk, h], acc)
        o_ref[h] = acc


def attn_bias(rel_bias, window, dil):
    heads = rel_bias.shape[1]
    idx = _bucket_tile(window, dil)
    qr, qr2 = idx.shape
    return pl.pallas_call(
        functools.partial(_bias_kernel, heads=heads),
        grid=(1,),
        in_specs=[
            pl.BlockSpec(memory_space=pltpu.SMEM),
            pl.BlockSpec((qr, qr2), lambda i: (0, 0)),
        ],
        out_specs=pl.BlockSpec((heads, qr, qr2), lambda i: (0, 0, 0)),
        out_shape=jax.ShapeDtypeStruct((heads, qr, qr2), F32),
        compiler_params=_cparams(("arbitrary",)),
        name=f"attn_bias_d{dil}",
    )(rel_bias.astype(F32), idx)


def _group_perm():
    shift = ATT_RES.bit_length() - 1
    row = lax.broadcasted_iota(jnp.int32, (ATT_GROUP, ATT_GROUP), 0)
    col = lax.broadcasted_iota(jnp.int32, (ATT_GROUP, ATT_GROUP), 1)
    src = ((row & (ATT_RES - 1)) << shift) | (row >> shift)
    return jnp.where(col == src, 1.0, 0.0).astype(BF16)


def _attn_kernel(*refs, hb, qr, nqb, has_halo, has_prev, last, scale, grid, side_blocks):
    it = iter(refs)
    q_ref, k_ref, v_ref = next(it), next(it), next(it)
    kp_ref, vp_ref = (next(it), next(it)) if has_halo else (None, None)
    bias_ref = next(it)
    op_ref, lp_ref = (next(it), next(it)) if has_prev else (None, None)
    side_in = [next(it) for _ in side_blocks]
    o_ref = next(it)
    l_ref = None if last else next(it)
    side_out = [next(it) for _ in side_blocks]
    kbuf, vbuf, s_scr, p_scr, m_scr, d_scr, obuf, lbuf = (next(it) for _ in range(8))
    _side_cast_body(grid, side_in, side_out, side_blocks)
    t = pl.program_id(3)
    tq = qr * nqb
    wb = hb * HEAD_DIM
    nt = (((1,), (1,)), ((), ()))
    units = [(h, j) for h in range(hb) for j in range(nqb)]

    q = q_ref[...].reshape(tq, wb)
    kbuf[qr:, :] = k_ref[...].reshape(tq, wb)
    vbuf[qr:, :] = v_ref[...].reshape(tq, wb)
    if has_halo:
        kbuf[0:qr, :] = kp_ref[...].reshape(qr, wb)
        vbuf[0:qr, :] = vp_ref[...].reshape(qr, wb)
        first = t == 0
    else:
        kbuf[0:qr, :] = jnp.zeros((qr, wb), BF16)
        vbuf[0:qr, :] = jnp.zeros((qr, wb), BF16)
        first = t >= 0
    prev_cols = lax.broadcasted_iota(jnp.int32, (qr, 2 * qr), 1) < qr

    for u, (h, j) in enumerate(units):
        lanes = slice(h * HEAD_DIM, (h + 1) * HEAD_DIM)
        s = lax.dot_general(q[j * qr:(j + 1) * qr, lanes], kbuf[j * qr:(j + 2) * qr, lanes], nt,
                            preferred_element_type=F32)
        s = s * scale + bias_ref[h]
        if j == 0:
            s = jnp.where(prev_cols & first, NEG, s)
        s_scr[u] = s

    for u in range(len(units)):
        s = s_scr[u]
        m = jnp.max(s, axis=-1, keepdims=True)
        p = jnp.exp(s - m)
        m_scr[u] = jnp.broadcast_to(m, (qr, HEAD_DIM))
        d_scr[u] = jnp.broadcast_to(jnp.sum(p, axis=-1, keepdims=True), (qr, HEAD_DIM))
        p_scr[u] = p.astype(BF16)

    if has_prev:
        o_prev = op_ref[...].reshape(tq, wb).astype(F32)
        l_prev = lp_ref[...].reshape(tq, wb)
    for u, (h, j) in enumerate(units):
        lanes = slice(h * HEAD_DIM, (h + 1) * HEAD_DIM)
        rows = slice(j * qr, (j + 1) * qr)
        acc = jnp.dot(p_scr[u], vbuf[j * qr:(j + 2) * qr, lanes], preferred_element_type=F32)
        den = d_scr[u]
        o = acc / den
        lse = m_scr[u] + jnp.log(den)
        if has_prev:
            lse0 = l_prev[rows, lanes]
            mx = jnp.maximum(lse0, lse)
            w0 = jnp.exp(lse0 - mx)
            w1 = jnp.exp(lse - mx)
            tot = w0 + w1
            o = (w0 * o_prev[rows, lanes] + w1 * o) / tot
            lse = mx + jnp.log(tot)
        obuf[rows, lanes] = o
        if not last:
            lbuf[rows, lanes] = lse

    if last:
        res = jnp.dot(_group_perm(), obuf[...].astype(BF16), preferred_element_type=F32)
        o_ref[...] = res.astype(o_ref.dtype).reshape(o_ref.shape)
    else:
        o_ref[...] = obuf[...].astype(o_ref.dtype).reshape(o_ref.shape)
        l_ref[...] = lbuf[...].reshape(l_ref.shape)


def attn_pattern(u3, bias, dil, prev, *, mix, last, side_casts=(), hb=4):
    bsz, seq, _ = u3.shape
    heads = mix // HEAD_DIM
    ng = seq // ATT_GROUP
    wb = hb * HEAD_DIM
    q0, k0, v0 = (2 * mix) // wb, (3 * mix) // wb, (4 * mix) // wb
    qr = bias.shape[1]
    halo = None
    if dil == 1:
        nqb, n_res, n_t = 1, 1, ng
        view = lambda a: a.reshape(bsz, ng, ATT_GROUP, a.shape[-1])
        cur = lambda c0: pl.BlockSpec((1, 1, ATT_GROUP, wb), lambda g, b, r, t: (b, t, 0, c0 + g))
        halo = lambda c0: pl.BlockSpec((1, 1, ATT_GROUP, wb),
                                       lambda g, b, r, t: (b, jnp.maximum(t - 1, 0), 0, c0 + g))
    elif dil == 4:
        nqb, n_res, gt = 2, 4, 4
        n_t = ng // gt
        view = lambda a: a.reshape(bsz, ng, 4, 4, ATT_RES, a.shape[-1])
        cur = lambda c0: pl.BlockSpec((1, gt, 4, 1, ATT_RES, wb), lambda g, b, r, t: (b, t, 0, r, 0, c0 + g))
        halo = lambda c0: pl.BlockSpec((1, gt // 2, 4, 1, ATT_RES, wb),
                                       lambda g, b, r, t: (b, jnp.maximum(2 * t - 1, 0), 0, r, 0, c0 + g))
    else:
        assert dil == ATT_RES and ng * ATT_RES == 2 * HEAD_DIM
        nqb, n_res, n_t = 2, ATT_RES, 1
        view = lambda a: a.reshape(bsz, ng, ATT_RES, ATT_RES, a.shape[-1])
        cur = lambda c0: pl.BlockSpec((1, ng, 1, ATT_RES, wb), lambda g, b, r, t: (b, 0, r, 0, c0 + g))
    has_halo = halo is not None
    tq = qr * nqb
    uv = view(u3)
    in_specs = [cur(q0), cur(k0), cur(v0)]
    args = [uv, uv, uv]
    if has_halo:
        in_specs += [halo(k0), halo(v0)]
        args += [uv, uv]
    in_specs.append(pl.BlockSpec((hb, qr, 2 * qr), lambda g, b, r, t: (g, 0, 0)))
    args.append(bias)
    if prev is not None:
        in_specs += [cur(0), cur(0)]
        args += [view(prev[0]), view(prev[1])]
    grid = (heads // hb, bsz, n_res, n_t)
    s_in, s_out, s_shapes, side_blocks, s_args = _side_cast_specs(side_casts, grid)
    in_specs += s_in
    args += s_args
    if last:
        assert dil == 1
        out_specs = [pl.BlockSpec((1, ATT_GROUP, wb), lambda g, b, r, t: (b, t, g))]
        out_shape = [jax.ShapeDtypeStruct((bsz, seq, mix), BF16)]
    else:
        out_specs = [cur(0), cur(0)]
        out_shape = [jax.eval_shape(view, jax.ShapeDtypeStruct((bsz, seq, mix), dt)) for dt in (BF16, F32)]
    n_main = len(out_specs)
    n_u = hb * nqb
    res = pl.pallas_call(
        functools.partial(_attn_kernel, hb=hb, qr=qr, nqb=nqb, has_halo=has_halo, has_prev=prev is not None,
                          last=last, scale=HEAD_DIM ** -0.5, grid=grid, side_blocks=side_blocks),
        grid=grid,
        in_specs=in_specs,
        out_specs=out_specs + s_out,
        out_shape=out_shape + s_shapes,
        scratch_shapes=[
            pltpu.VMEM((tq + qr, wb), BF16), pltpu.VMEM((tq + qr, wb), BF16),
            pltpu.VMEM((n_u, qr, 2 * qr), F32), pltpu.VMEM((n_u, qr, 2 * qr), BF16),
            pltpu.VMEM((n_u, qr, HEAD_DIM), F32), pltpu.VMEM((n_u, qr, HEAD_DIM), F32),
            pltpu.VMEM((tq, wb), F32), pltpu.VMEM((tq, wb), F32),
        ],
        compiler_params=_cparams(("arbitrary",) * 4),
        name=f"attn_d{dil}",
    )(*args)
    sides = list(res[n_main:])
    if last:
        return res[0], sides
    return (res[0].reshape(bsz, seq, mix), res[1].reshape(bsz, seq, mix)), sides


def dilated_attention(u3, rel_bias, mix, side_casts=()):
    order = sorted(DSW_PATTERNS, key=lambda wd: wd[1] == 1)
    prev = None
    sides = []
    for gi, (window, dil) in enumerate(order):
        assert window // dil == HEAD_DIM
        prev, side = attn_pattern(u3, attn_bias(rel_bias, window, dil), dil, prev, mix=mix,
                                  last=gi == len(order) - 1, side_casts=side_casts[gi] if side_casts else ())
        sides += side
    return prev, sides


def _split3(v):
    hi = v.astype(BF16)
    r1 = v - hi.astype(F32)
    mid = r1.astype(BF16)
    lo = (r1 - mid.astype(F32)).astype(BF16)
    return hi, mid, lo


def _scaled(v, expo, mask):
    return jnp.where(mask, v * jnp.exp(jnp.where(mask, expo, 0.0)), 0.0)


def _hgrn_kernel(*refs, hb, ts, layer, grid, side_blocks):
    n_side = len(side_blocks)
    cq_ref, cf_ref, ci_ref, cg_ref, lb_ref, ng_ref = refs[:6]
    side_in = refs[6:6 + n_side]
    o_ref = refs[6 + n_side]
    side_out = refs[7 + n_side:7 + 2 * n_side]
    st_ref, q_s, kk_s, b_s, a_s = refs[7 + 2 * n_side:]
    _side_cast_body(grid, side_in, side_out, side_blocks)
    s = pl.program_id(2)
    c_len = HGRN_CHUNK
    sub = 16
    sub8 = 8
    nt = (((1,), (1,)), ((), ()))
    tn_ = (((0,), (0,)), ((), ()))

    @pl.when(s == 0)
    def _():
        st_ref[...] = jnp.zeros(st_ref.shape, F32)

    row = lax.broadcasted_iota(jnp.int32, (c_len, c_len), 0)
    col = lax.broadcasted_iota(jnp.int32, (c_len, c_len), 1)
    tri = jnp.where(row >= col, 1.0, 0.0).astype(BF16)
    r64 = lax.broadcasted_iota(jnp.int32, (c_len, HEAD_DIM), 0)
    half = c_len // 2
    mask_b = (((row >= sub) & (row < half) & (col < sub))
              | ((row >= half + sub) & (col >= half) & (col < half + sub)))
    mask_c = ((row >> 4) == (col >> 4)) & ((row & (sub - 1)) >= sub8) & ((col & (sub - 1)) < sub8)
    lrow = lax.broadcasted_iota(jnp.int32, (lb_ref.shape[0], HEAD_DIM), 0)

    def chunk_body(ci, carry):
        r0 = pl.multiple_of(ci * c_len, c_len)
        rows = pl.ds(r0, c_len)
        head_lanes = [slice(h * HEAD_DIM, (h + 1) * HEAD_DIM) for h in range(hb)]
        for h, lanes in enumerate(head_lanes):
            lg = lb_ref[:, lanes]
            pe = jnp.exp(lg - jnp.max(lg, axis=0, keepdims=True))
            lb = (jnp.sum(jnp.where((lrow >= 1) & (lrow <= layer), pe, 0.0), axis=0, keepdims=True)
                  / jnp.sum(pe, axis=0, keepdims=True))
            f = lb + (1.0 - lb) * jax.nn.sigmoid(cf_ref[0, rows, lanes].astype(F32))
            hi, mid, lo = _split3(jnp.log(f))
            q_s[h] = _silu(cq_ref[0, rows, lanes].astype(F32))
            kk_s[h] = 1.0 - f
            b_s[h] = (jnp.dot(tri, hi, preferred_element_type=F32)
                      + jnp.dot(tri, mid, preferred_element_type=F32)
                      + jnp.dot(tri, lo, preferred_element_type=F32))

        for h in range(hb):
            q, kk, b = q_s[h], kk_s[h], b_s[h]
            b_a = b_s[h, half - 1:half, :]
            qa = _scaled(q, b - b_a, r64 >= half)
            ka = _scaled(kk, b_a - b, r64 < half)
            attn = lax.dot_general(qa.astype(BF16), ka.astype(BF16), nt, preferred_element_type=F32)
            b_r = jnp.where(r64 < half, b_s[h, sub - 1:sub, :], b_s[h, half + sub - 1:half + sub, :])
            qsel = ((r64 >= sub) & (r64 < half)) | (r64 >= half + sub)
            ksel = (r64 < sub) | ((r64 >= half) & (r64 < half + sub))
            qbm = _scaled(q, b - b_r, qsel)
            kbm = _scaled(kk, b_r - b, ksel)
            attn_b = lax.dot_general(qbm.astype(BF16), kbm.astype(BF16), nt, preferred_element_type=F32)
            b_c = b_s[h, sub8 - 1:sub8, :]
            for a in range(1, c_len // sub):
                b_c = jnp.where(r64 >= a * sub, b_s[h, a * sub + sub8 - 1:a * sub + sub8, :], b_c)
            upper = (r64 & (sub - 1)) >= sub8
            qcm = _scaled(q, b - b_c, upper)
            kcm = _scaled(kk, b_c - b, jnp.logical_not(upper))
            attn_c = lax.dot_general(qcm.astype(BF16), kcm.astype(BF16), nt, preferred_element_type=F32)
            a_s[h, :, 0:c_len] = attn + jnp.where(mask_b, attn_b, 0.0) + jnp.where(mask_c, attn_c, 0.0)

        for h in range(hb):
            for jb in range(c_len // sub8):
                blk = slice(jb * sub8, (jb + 1) * sub8)
                qt = q_s[h, blk, :]
                bt = b_s[h, blk, :]
                for si in range(sub8):
                    r = jb * sub8 + si
                    e = jnp.exp(bt - b_s[h, r:r + 1, :])
                    a_s[h, blk, r:r + 1] = jnp.sum(qt * kk_s[h, r:r + 1, :] * e, axis=-1, keepdims=True)

        for h, lanes in enumerate(head_lanes):
            q, kk, b = q_s[h], kk_s[h], b_s[h]
            v = ci_ref[0, rows, lanes]
            b_last = b_s[h, c_len - 1:c_len, :]
            st_t = st_ref[h]
            inter = lax.dot_general((q * jnp.exp(b)).astype(BF16), st_t.astype(BF16), nt,
                                    preferred_element_type=F32)
            attn = jnp.where(row >= col, a_s[h, :, 0:c_len], 0.0)
            o = inter + jnp.dot(attn.astype(BF16), v, preferred_element_type=F32)
            kd = (kk * jnp.exp(b_last - b)).astype(BF16)
            st_ref[h] = st_t * jnp.exp(b_last) + lax.dot_general(v, kd, tn_, preferred_element_type=F32)
            ms = jnp.mean(o * o, axis=-1, keepdims=True)
            o = o * lax.rsqrt(ms + LN_EPS) * ng_ref[0:1, lanes]
            o = o * _silu(cg_ref[0, rows, lanes].astype(F32))
            o_ref[0, rows, lanes] = o.astype(o_ref.dtype)
        return carry
    lax.fori_loop(0, ts // c_len, chunk_body, 0)


def hgrn_branch(u3, lb_logits, layer, norm_g, mix, *, side_casts=(), hb=4, ts=256):
    bsz, seq, _ = u3.shape
    heads = mix // HEAD_DIM
    wb = hb * HEAD_DIM
    nb = mix // wb
    grid = (bsz, heads // hb, seq // ts)
    s_in, s_out, s_shapes, side_blocks, s_args = _side_cast_specs(side_casts, grid)

    def col(k):
        return pl.BlockSpec((1, ts, wb), lambda b, g, s: (b, s, k * nb + g))

    vec = pl.BlockSpec((1, wb), lambda b, g, s: (0, g))
    res = pl.pallas_call(
        functools.partial(_hgrn_kernel, hb=hb, ts=ts, layer=layer, grid=grid, side_blocks=side_blocks),
        grid=grid,
        in_specs=[col(0), col(1), col(2), col(3),
                  pl.BlockSpec((lb_logits.shape[0], wb), lambda b, g, s: (0, g)), vec] + s_in,
        out_specs=[pl.BlockSpec((1, ts, wb), lambda b, g, s: (b, s, g))] + s_out,
        out_shape=[jax.ShapeDtypeStruct((bsz, seq, mix), BF16)] + s_shapes,
        scratch_shapes=[pltpu.VMEM((hb, HEAD_DIM, HEAD_DIM), F32)]
        + [pltpu.VMEM((hb, HGRN_CHUNK, HEAD_DIM), F32)] * 4,
        compiler_params=_cparams(("arbitrary",) * 3),
        name="hgrn",
    )(u3, u3, u3, u3, lb_logits.astype(F32), norm_g.reshape(1, mix), *s_args)
    return res[0], list(res[1:])


POOL_HALO = 16


def _pool_kernel(dp_ref, pw_ref, ps_ref, o_ref, x_ref, p_ref, *, ts, rc):
    s = pl.program_id(1)
    ch = o_ref.shape[2]
    grp = ch // len(POOL_WINDOWS)

    @pl.when(s == 0)
    def _():
        x_ref[:, 0:POOL_HALO, :] = jnp.zeros((ch // 128, POOL_HALO, 128), F32)

    @pl.when(s > 0)
    def _():
        x_ref[:, 0:POOL_HALO, :] = x_ref[:, ts:ts + POOL_HALO, :]

    def load_body(i, carry):
        r0 = pl.multiple_of(i * rc, rc)
        xv = dp_ref[0, pl.ds(r0, rc), :].astype(F32)
        for cc in range(ch // 128):
            x_ref[cc, pl.ds(POOL_HALO + r0, rc), :] = xv[:, cc * 128:(cc + 1) * 128]
        return carry
    lax.fori_loop(0, ts // rc, load_body, 0)

    def pool_body(i, carry):
        r0 = pl.multiple_of(i * rc, rc)
        pos = s * ts + r0 + lax.broadcasted_iota(jnp.int32, (rc, 1), 0)
        for cc in range(ch // 128):
            w = POOL_WINDOWS[(cc * 128) // grp]
            cur = x_ref[cc, pl.ds(POOL_HALO + r0, rc), :]
            tot = cur
            for j in range(1, w):
                tot = tot + x_ref[cc, pl.ds(POOL_HALO + r0 - j, rc), :]
            cnt = jnp.minimum(pos + 1, w).astype(F32)
            p_ref[pl.ds(r0, rc), cc * 128:(cc + 1) * 128] = (tot / cnt - cur).astype(BF16)
        return carry
    lax.fori_loop(0, ts // rc, pool_body, 0)

    for gi in range(len(POOL_WINDOWS)):
        lanes = slice(gi * grp, (gi + 1) * grp)
        y = jnp.dot(p_ref[:, lanes], pw_ref[gi], preferred_element_type=F32)
        o_ref[0, :, lanes] = (y * ps_ref[0:1, lanes]).astype(o_ref.dtype)


def pool_branch(u3, pool_w, pool_scale, mix, *, ts=512, rc=32):
    bsz, seq, _ = u3.shape
    ng, grp, _ = pool_w.shape
    return pl.pallas_call(
        functools.partial(_pool_kernel, ts=ts, rc=rc),
        grid=(bsz, seq // ts),
        in_specs=[
            pl.BlockSpec((1, ts, mix), lambda b, s: (b, s, 4)),
            pl.BlockSpec((ng, grp, grp), lambda b, s: (0, 0, 0)),
            pl.BlockSpec((1, mix), lambda b, s: (0, 0)),
        ],
        out_specs=pl.BlockSpec((1, ts, mix), lambda b, s: (b, s, 0)),
        out_shape=jax.ShapeDtypeStruct((bsz, seq, mix), BF16),
        scratch_shapes=[pltpu.VMEM((mix // 128, ts + POOL_HALO, 128), F32), pltpu.VMEM((ts, mix), BF16)],
        compiler_params=_cparams(("arbitrary", "arbitrary")),
        name="pool_branch",
    )(u3, pool_w.astype(BF16), pool_scale.reshape(1, mix))


def kernel(x, c, ada_w, ada_b, w_in, w_out, ln_g, ln_b, mlp_w1, mlp_w2, conv_w, conv_b, conv_ln_g, conv_ln_b,
           rel_bias, hgrn_lb_logits, hgrn_norm_g, pool_w, pool_scale):
    bsz, seq, d = x.shape
    depth = ada_w.shape[0]
    mix = d // 2
    alpha = (2.0 * depth) ** 0.25
    m = bsz * seq

    mod = adaln_mod(c, ada_w, ada_b)
    x2 = x.reshape(m, d)
    h = modulate(x2, mod[0], seq)
    w_in_b = cast_layer_bf16(w_in, 0)
    for l in range(depth):
        u = mm_in(h, w_in_b, perm_from=2 * mix if l % 2 == 0 else None)
        u3 = u.reshape(bsz, seq, 5 * mix)
        if l % 2 == 0:
            e = l // 2
            conv_sides = [(mlp_w2, l, 256)] + ([(w_in, l + 1, 64)] if l + 1 < depth else [])
            za, sides = conv_branch(u3, conv_w[e], conv_b[e], conv_ln_g[e], conv_ln_b[e], mix,
                                    side_casts=conv_sides)
            w2_b = sides[0]
            if l + 1 < depth:
                w_in_b = sides[1]
            zb, (w_out_b, w1_b) = dilated_attention(
                u3, rel_bias, mix, side_casts=[[], [(w_out, l, 16)], [(mlp_w1, l, 16)]])
        else:
            if l + 1 < depth:
                w_in_b = cast_layer_bf16(w_in, l + 1)
            o = l // 2
            za, (w_out_b, w1_b, w2_b) = hgrn_branch(
                u3, hgrn_lb_logits, l, hgrn_norm_g[o], mix,
                side_casts=[(w_out, l, 16), (mlp_w1, l, 16), (mlp_w2, l, 64)])
            zb = pool_branch(u3, pool_w[o], pool_scale[o], mix)
        x2, h2 = mm_out_ln(za.reshape(m, mix), zb.reshape(m, mix), w_out_b, x2, mod[l],
                           ln_g[l, 0], ln_b[l, 0], seq, alpha)
        x2, h = mlp_ln(h2, x2, mod[l], mod[l + 1] if l + 1 < depth else None,
                       w1_b, w2_b, ln_g[l, 1], ln_b[l, 1], seq, alpha)
    return x2.reshape(bsz, seq, d)
```

```python
import functools
import math

import jax
import jax.numpy as jnp
from jax import lax
from jax.experimental import pallas as pl
from jax.experimental.pallas import tpu as pltpu

F32 = jnp.float32
BF16 = jnp.bfloat16

HEAD_DIM = 128
CONV_K = 31
DSW_PATTERNS = ((128, 1), (512, 4), (2048, 16))
NUM_BUCKETS = 32
MAX_DISTANCE = 2048
HGRN_CHUNK = 64
POOL_WINDOWS = (2, 4, 8, 16)
LN_EPS = 1e-5
NEG = -1e30

V7X_VMEM_BYTES = 64 * 1024 * 1024
VMEM_LIMIT = V7X_VMEM_BYTES - 2 * 1024 * 1024


def _cparams(sem):
    return pltpu.CompilerParams(dimension_semantics=sem, vmem_limit_bytes=VMEM_LIMIT)


def _silu(v):
    return v * jax.nn.sigmoid(v)


def _mod_kernel(c_ref, w_ref, b_ref, o_ref):
    cs = _silu(c_ref[...]).astype(BF16)
    w = w_ref[0].astype(BF16)
    o_ref[0] = jnp.dot(cs, w, preferred_element_type=F32) + b_ref[0]


def adaln_mod(c, ada_w, ada_b, *, tn=512):
    nl, d, n6 = ada_w.shape
    bsz = c.shape[0]
    rows = 8
    c8 = jnp.zeros((rows, d), F32).at[:bsz].set(c)
    out = pl.pallas_call(
        _mod_kernel,
        grid=(nl, n6 // tn),
        in_specs=[
            pl.BlockSpec((rows, d), lambda l, j: (0, 0)),
            pl.BlockSpec((1, d, tn), lambda l, j: (l, 0, j)),
            pl.BlockSpec((1, 1, tn), lambda l, j: (l, 0, j)),
        ],
        out_specs=pl.BlockSpec((1, rows, tn), lambda l, j: (l, 0, j)),
        out_shape=jax.ShapeDtypeStruct((nl, rows, n6), F32),
        compiler_params=_cparams(("arbitrary", "arbitrary")),
        name="adaln_mod",
    )(c8, ada_w, ada_b.reshape(nl, 1, n6))
    return out[:, :bsz].reshape(nl, bsz, 6, d)


LN_ROWS = 16
LN_COLS = 512


def _ln_chunks(chunks, d, g_ref, b_ref):
    col_slices = [slice(c * LN_COLS, (c + 1) * LN_COLS) for c in range(d // LN_COLS)]
    mus = []
    for load, _ in chunks:
        tot = load(col_slices[0])
        for cols in col_slices[1:]:
            tot = tot + load(cols)
        mus.append(jnp.sum(tot, axis=-1, keepdims=True) * (1.0 / d))
    rstds = []
    for (load, _), mu in zip(chunks, mus):
        sq = None
        for cols in col_slices:
            dv = load(cols) - mu
            sq = dv * dv if sq is None else sq + dv * dv
        rstds.append(lax.rsqrt(jnp.sum(sq, axis=-1, keepdims=True) * (1.0 / d) + LN_EPS))
    for (load, emit), mu, rstd in zip(chunks, mus, rstds):
        for cols in col_slices:
            emit(cols, (load(cols) - mu) * rstd * g_ref[:, cols] + b_ref[:, cols])


def _ln_chunk(load, d, g_ref, b_ref, emit):
    _ln_chunks([(load, emit)], d, g_ref, b_ref)


def _ln_mod_phase(step, rows_per_step, src_ref, xr_ref, alpha, gate_ref, g_ref, b_ref, mod_ref, sh_row, sc_row,
                  xo_ref, ho_ref):
    d = xo_ref.shape[1]
    chunks = []
    for k in range(rows_per_step // LN_ROWS):
        src_rows = pl.ds(pl.multiple_of(step * rows_per_step + k * LN_ROWS, LN_ROWS), LN_ROWS)
        out_rows = slice(k * LN_ROWS, (k + 1) * LN_ROWS)
        if xr_ref is not None:
            for c in range(d // LN_COLS):
                cols = slice(c * LN_COLS, (c + 1) * LN_COLS)
                xo_ref[out_rows, cols] = alpha * xr_ref[out_rows, cols] + gate_ref[0, 5:6, cols] * src_ref[src_rows, cols]
            load = lambda cols, out_rows=out_rows: xo_ref[out_rows, cols]
        else:
            load = lambda cols, src_rows=src_rows: src_ref[src_rows, cols]

        def emit(cols, y, out_rows=out_rows):
            xo_ref[out_rows, cols] = y
            if ho_ref is not None:
                h = y * (1.0 + mod_ref[0, sc_row:sc_row + 1, cols]) + mod_ref[0, sh_row:sh_row + 1, cols]
                ho_ref[out_rows, cols] = h.astype(BF16)
        chunks.append((load, emit))
    _ln_chunks(chunks, d, g_ref, b_ref)


def _cast_kernel(w_ref, o_ref):
    o_ref[...] = w_ref[0].astype(o_ref.dtype)


def cast_layer_bf16(w_stack, layer, *, tr=512, tc=2048):
    _, rows, cols = w_stack.shape
    tc = min(tc, cols)
    return pl.pallas_call(
        _cast_kernel,
        grid=(rows // tr, cols // tc),
        in_specs=[pl.BlockSpec((1, tr, tc), lambda i, j: (layer, i, j))],
        out_specs=pl.BlockSpec((tr, tc), lambda i, j: (i, j)),
        out_shape=jax.ShapeDtypeStruct((rows, cols), BF16),
        compiler_params=_cparams(("arbitrary", "arbitrary")),
        name="cast_bf16",
    )(w_stack)


def _modulate_kernel(x_ref, mod_ref, o_ref):
    o_ref[...] = (x_ref[...] * (1.0 + mod_ref[0, 1:2, :]) + mod_ref[0, 0:1, :]).astype(o_ref.dtype)


def modulate(x2, mod_l, seq, *, tm=256):
    m, d = x2.shape
    return pl.pallas_call(
        _modulate_kernel,
        grid=(m // tm,),
        in_specs=[
            pl.BlockSpec((tm, d), lambda i: (i, 0)),
            pl.BlockSpec((1, 6, d), lambda i: ((i * tm) // seq, 0, 0)),
        ],
        out_specs=pl.BlockSpec((tm, d), lambda i: (i, 0)),
        out_shape=jax.ShapeDtypeStruct((m, d), BF16),
        compiler_params=_cparams(("arbitrary",)),
        name="modulate",
    )(x2, mod_l)


def _side_cast_specs(side_casts, grid):
    def linear(*ids):
        step = ids[0]
        for size, idx in zip(grid[1:], ids[1:]):
            step = step * size + idx
        return step

    in_specs, out_specs, out_shapes, n_blocks = [], [], [], []
    for arr, layer, br in side_casts:
        _, rows, cols = arr.shape
        nb = rows // br
        assert rows % br == 0 and nb <= math.prod(grid)
        blk = lambda *ids, nb=nb: jnp.minimum(linear(*ids), nb - 1)
        in_specs.append(pl.BlockSpec((1, br, cols), lambda *ids, layer=layer, blk=blk: (layer, blk(*ids), 0)))
        out_specs.append(pl.BlockSpec((br, cols), lambda *ids, blk=blk: (blk(*ids), 0)))
        out_shapes.append(jax.ShapeDtypeStruct((rows, cols), BF16))
        n_blocks.append(nb)
    return in_specs, out_specs, out_shapes, tuple(n_blocks), [a for a, _, _ in side_casts]


def _side_cast_body(grid, side_in, side_out, n_blocks):
    step = pl.program_id(0)
    for ax in range(1, len(grid)):
        step = step * grid[ax] + pl.program_id(ax)
    for src, dst, nb in zip(side_in, side_out, n_blocks):
        @pl.when(step < nb)
        def _(src=src, dst=dst):
            dst[...] = src[0].astype(dst.dtype)


def _mm_in_kernel(h_ref, w_ref, o_ref, *scratch, perm_tile):
    if perm_tile is None:
        o_ref[...] = jnp.dot(h_ref[...], w_ref[...], preferred_element_type=F32).astype(o_ref.dtype)
        return
    hp_ref, = scratch
    j = pl.program_id(1)

    @pl.when(j == 0)
    def _():
        perm = _group_perm()
        for g in range(h_ref.shape[0] // ATT_GROUP):
            rows = slice(g * ATT_GROUP, (g + 1) * ATT_GROUP)
            hp_ref[rows, :] = jnp.dot(perm, h_ref[rows, :], preferred_element_type=F32).astype(BF16)

    @pl.when(j < perm_tile)
    def _():
        o_ref[...] = jnp.dot(h_ref[...], w_ref[...], preferred_element_type=F32).astype(o_ref.dtype)

    @pl.when(j >= perm_tile)
    def _():
        o_ref[...] = jnp.dot(hp_ref[...], w_ref[...], preferred_element_type=F32).astype(o_ref.dtype)


def mm_in(h, w, *, perm_from=None, tm=1024, tn=1024):
    m, d = h.shape
    n = w.shape[1]
    perm_tile = None
    scratch = []
    if perm_from is not None:
        assert perm_from % tn == 0 and tm % ATT_GROUP == 0
        perm_tile = perm_from // tn
        scratch = [pltpu.VMEM((tm, d), BF16)]
    return pl.pallas_call(
        functools.partial(_mm_in_kernel, perm_tile=perm_tile),
        grid=(m // tm, n // tn),
        in_specs=[
            pl.BlockSpec((tm, d), lambda i, j: (i, 0)),
            pl.BlockSpec((d, tn), lambda i, j: (0, j)),
        ],
        out_specs=pl.BlockSpec((tm, tn), lambda i, j: (i, j)),
        out_shape=jax.ShapeDtypeStruct((m, n), BF16),
        scratch_shapes=scratch,
        compiler_params=_cparams(("arbitrary", "arbitrary")),
        name="mm_in",
    )(h, w)


def _mm_out_kernel(a_ref, b_ref, w_ref, x_ref, mod_ref, modp_ref, g_ref, beta_ref, xo_ref, ho_ref, acc_ref, ln_ref,
                   *, n_t, n_n, tn, k1, alpha):
    i = pl.program_id(0)
    n = pl.program_id(1)
    rows_per_step = acc_ref.shape[0] // n_n

    @pl.when((i == 0) & (n == 0))
    def _():
        ln_ref[...] = jnp.zeros(ln_ref.shape, F32)

    def ln_phase():
        _ln_mod_phase(n, rows_per_step, ln_ref, None, alpha, None, g_ref, beta_ref, modp_ref, 3, 4, xo_ref, ho_ref)

    @pl.when(i < n_t)
    def _():
        y = jnp.dot(a_ref[...], w_ref[0:k1, :], preferred_element_type=F32)
        y = y + jnp.dot(b_ref[...], w_ref[k1:, :], preferred_element_type=F32)
        cols = pl.ds(pl.multiple_of(n * tn, tn), tn)
        acc_ref[:, cols] = alpha * x_ref[...] + mod_ref[0, 2:3, cols] * y
        ln_phase()

    @pl.when(i == n_t)
    def _():
        ln_phase()

    @pl.when((i < n_t) & (n == n_n - 1))
    def _():
        ln_ref[...] = acc_ref[...]


def mm_out_ln(za, zb, w, x2, mod_l, ln_g, ln_b, seq, alpha, *, tm=512, tn=512):
    m, k1 = za.shape
    d = w.shape[1]
    n_n = d // tn
    n_t = m // tm
    rl = tm // n_n
    assert rl % LN_ROWS == 0
    cur = lambda i: jnp.minimum(i, n_t - 1)
    prv = lambda i: jnp.maximum(i - 1, 0)
    col = lambda i, n: jnp.where(i < n_t, n, n_n - 1)
    out_row = pl.BlockSpec((rl, d), lambda i, n: (jnp.where(i == 0, 0, (i - 1) * n_n + n), 0))
    return pl.pallas_call(
        functools.partial(_mm_out_kernel, n_t=n_t, n_n=n_n, tn=tn, k1=k1, alpha=alpha),
        grid=(n_t + 1, n_n),
        in_specs=[
            pl.BlockSpec((tm, k1), lambda i, n: (cur(i), 0)),
            pl.BlockSpec((tm, zb.shape[1]), lambda i, n: (cur(i), 0)),
            pl.BlockSpec((w.shape[0], tn), lambda i, n: (0, col(i, n))),
            pl.BlockSpec((tm, tn), lambda i, n: (cur(i), col(i, n))),
            pl.BlockSpec((1, 6, d), lambda i, n: ((cur(i) * tm) // seq, 0, 0)),
            pl.BlockSpec((1, 6, d), lambda i, n: ((prv(i) * tm) // seq, 0, 0)),
            pl.BlockSpec((1, d), lambda i, n: (0, 0)),
            pl.BlockSpec((1, d), lambda i, n: (0, 0)),
        ],
        out_specs=[out_row, out_row],
        out_shape=[jax.ShapeDtypeStruct((m, d), F32), jax.ShapeDtypeStruct((m, d), BF16)],
        scratch_shapes=[pltpu.VMEM((tm, d), F32), pltpu.VMEM((tm, d), F32)],
        compiler_params=_cparams(("arbitrary", "arbitrary")),
        name="mm_out_ln",
    )(za, zb, w, x2, mod_l, mod_l, ln_g.reshape(1, d), ln_b.reshape(1, d))


def _mlp_kernel(h_ref, xr_ref, modp_ref, modn_ref, w1_ref, w2_ref, g_ref, beta_ref, *rest,
                n_t, n_f, tn2, alpha, emit_h):
    if emit_h:
        xo_ref, ho_ref, acc_ref, ln_ref = rest
    else:
        xo_ref, acc_ref, ln_ref = rest
        ho_ref = None
    i = pl.program_id(0)
    f = pl.program_id(1)
    d = acc_ref.shape[1]
    rows_per_step = acc_ref.shape[0] // n_f

    @pl.when((i == 0) & (f == 0))
    def _():
        ln_ref[...] = jnp.zeros(ln_ref.shape, F32)

    @pl.when((i < n_t) & (f == 0))
    def _():
        acc_ref[...] = jnp.zeros(acc_ref.shape, F32)

    def ln_phase():
        _ln_mod_phase(f, rows_per_step, ln_ref, xr_ref, alpha, modp_ref, g_ref, beta_ref, modn_ref, 0, 1,
                      xo_ref, ho_ref)

    @pl.when(i < n_t)
    def _():
        t = jnp.dot(h_ref[...], w1_ref[...], preferred_element_type=F32)
        t = jnp.maximum(t, 0.0)
        t = (t * t).astype(BF16)
        for nb in range(d // tn2):
            cols = slice(nb * tn2, (nb + 1) * tn2)
            acc_ref[:, cols] += jnp.dot(t, w2_ref[:, cols], preferred_element_type=F32)
        ln_phase()

    @pl.when(i == n_t)
    def _():
        ln_phase()

    @pl.when((i < n_t) & (f == n_f - 1))
    def _():
        ln_ref[...] = acc_ref[...]


def mlp_ln(h, x2, mod_l, mod_next, w1, w2, ln_g, ln_b, seq, alpha, *, tm=512, tf=1024, tn2=512):
    m, d = x2.shape
    dff = w1.shape[1]
    n_f = dff // tf
    n_t = m // tm
    rl = tm // n_f
    assert rl % LN_ROWS == 0
    emit_h = mod_next is not None
    cur = lambda i: jnp.minimum(i, n_t - 1)
    prv = lambda i: jnp.maximum(i - 1, 0)
    chunk = lambda i, f: jnp.where(i < n_t, f, n_f - 1)
    out_row = pl.BlockSpec((rl, d), lambda i, f: (jnp.where(i == 0, 0, (i - 1) * n_f + f), 0))
    modspec = pl.BlockSpec((1, 6, d), lambda i, f: ((prv(i) * tm) // seq, 0, 0))
    f32_out = jax.ShapeDtypeStruct((m, d), F32)
    res = pl.pallas_call(
        functools.partial(_mlp_kernel, n_t=n_t, n_f=n_f, tn2=tn2, alpha=alpha, emit_h=emit_h),
        grid=(n_t + 1, n_f),
        in_specs=[
            pl.BlockSpec((tm, d), lambda i, f: (cur(i), 0), pipeline_mode=pl.Buffered(1)),
            pl.BlockSpec((rl, d), lambda i, f: (prv(i) * n_f + f, 0)),
            modspec,
            modspec,
            pl.BlockSpec((d, tf), lambda i, f: (0, chunk(i, f))),
            pl.BlockSpec((tf, d), lambda i, f: (chunk(i, f), 0)),
            pl.BlockSpec((1, d), lambda i, f: (0, 0)),
            pl.BlockSpec((1, d), lambda i, f: (0, 0)),
        ],
        out_specs=[out_row, out_row] if emit_h else out_row,
        out_shape=[f32_out, jax.ShapeDtypeStruct((m, d), BF16)] if emit_h else f32_out,
        scratch_shapes=[pltpu.VMEM((tm, d), F32), pltpu.VMEM((tm, d), F32)],
        compiler_params=_cparams(("arbitrary", "arbitrary")),
        name="mlp_ln",
    )(h, x2, mod_l, mod_next if emit_h else mod_l, w1, w2, ln_g.reshape(1, d), ln_b.reshape(1, d))
    return res if emit_h else (res, None)


CONV_HALO = 32


def _conv_kernel(*refs, ts, rc, grid, side_blocks):
    n_side = len(side_blocks)
    av_ref, ag_ref, w_ref, cb_ref, g_ref, beta_ref = refs[:6]
    side_in = refs[6:6 + n_side]
    o_ref = refs[6 + n_side]
    side_out = refs[7 + n_side:7 + 2 * n_side]
    glu_ref, y_ref = refs[7 + 2 * n_side:]
    _side_cast_body(grid, side_in, side_out, side_blocks)
    s = pl.program_id(1)
    ch = o_ref.shape[2]

    @pl.when(s == 0)
    def _():
        glu_ref[:, 0:CONV_HALO, :] = jnp.zeros((ch // 128, CONV_HALO, 128), F32)

    @pl.when(s > 0)
    def _():
        glu_ref[:, 0:CONV_HALO, :] = glu_ref[:, ts:ts + CONV_HALO, :]

    def glu_body(i, carry):
        r0 = pl.multiple_of(i * rc, rc)
        a = av_ref[0, pl.ds(r0, rc), :].astype(F32)
        gt = ag_ref[0, pl.ds(r0, rc), :].astype(F32)
        glu = a * jax.nn.sigmoid(gt)
        for cc in range(ch // 128):
            glu_ref[cc, pl.ds(CONV_HALO + r0, rc), :] = glu[:, cc * 128:(cc + 1) * 128]
        return carry
    lax.fori_loop(0, ts // rc, glu_body, 0)

    off = CONV_HALO - (CONV_K - 1)

    def conv_body(i, carry):
        r0 = pl.multiple_of(i * rc, rc)
        for cc in range(ch // 128):
            lanes = slice(cc * 128, (cc + 1) * 128)
            acc = jnp.zeros((rc, 128), F32)
            for j in range(CONV_K):
                acc = acc + w_ref[j:j + 1, lanes] * glu_ref[cc, pl.ds(r0 + off + j, rc), :]
            y_ref[pl.ds(r0, rc), lanes] = acc + cb_ref[0:1, lanes]
        return carry
    lax.fori_loop(0, ts // rc, conv_body, 0)

    def ln_body(i, carry):
        rows = pl.ds(pl.multiple_of(i * LN_ROWS, LN_ROWS), LN_ROWS)

        def emit(cols, z):
            o_ref[0, rows, cols] = _silu(z).astype(o_ref.dtype)
        _ln_chunk(lambda cols: y_ref[rows, cols], ch, g_ref, beta_ref, emit)
        return carry
    lax.fori_loop(0, ts // LN_ROWS, ln_body, 0, unroll=2)


def conv_branch(u3, conv_w, conv_b, ln_g, ln_b, mix, *, side_casts=(), ts=256, rc=32):
    bsz, seq, _ = u3.shape
    grid = (bsz, seq // ts)
    s_in, s_out, s_shapes, side_blocks, s_args = _side_cast_specs(side_casts, grid)
    res = pl.pallas_call(
        functools.partial(_conv_kernel, ts=ts, rc=rc, grid=grid, side_blocks=side_blocks),
        grid=grid,
        in_specs=[
            pl.BlockSpec((1, ts, mix), lambda b, s: (b, s, 0)),
            pl.BlockSpec((1, ts, mix), lambda b, s: (b, s, 1)),
            pl.BlockSpec((CONV_K, mix), lambda b, s: (0, 0)),
            pl.BlockSpec((1, mix), lambda b, s: (0, 0)),
            pl.BlockSpec((1, mix), lambda b, s: (0, 0)),
            pl.BlockSpec((1, mix), lambda b, s: (0, 0)),
        ] + s_in,
        out_specs=[pl.BlockSpec((1, ts, mix), lambda b, s: (b, s, 0))] + s_out,
        out_shape=[jax.ShapeDtypeStruct((bsz, seq, mix), BF16)] + s_shapes,
        scratch_shapes=[pltpu.VMEM((mix // 128, ts + CONV_HALO, 128), F32), pltpu.VMEM((ts, mix), F32)],
        compiler_params=_cparams(("arbitrary", "arbitrary")),
        name="conv_branch",
    )(u3, u3, conv_w, conv_b.reshape(1, mix), ln_g.reshape(1, mix), ln_b.reshape(1, mix), *s_args)
    return res[0], list(res[1:])


def _t5_bucket(dist):
    max_exact = NUM_BUCKETS // 2
    nf = jnp.maximum(dist, 1).astype(F32)
    large = max_exact + (jnp.log(nf / max_exact) / math.log(MAX_DISTANCE / max_exact)
                         * (NUM_BUCKETS - max_exact)).astype(jnp.int32)
    large = jnp.minimum(large, NUM_BUCKETS - 1)
    return jnp.where(dist < max_exact, dist, large)


ATT_GROUP = 256
ATT_RES = 16


def _natural_index(dil):
    if dil == 1:
        i = jnp.arange(ATT_GROUP)
        return ATT_GROUP, ATT_RES * (i % ATT_RES) + i // ATT_RES
    if dil == 4:
        i = jnp.arange(HEAD_DIM)
        return HEAD_DIM, (i // 64) * 64 + 4 * (i % 16) + (i % 64) // 16
    assert dil == ATT_RES
    return HEAD_DIM, jnp.arange(HEAD_DIM)


def _bucket_tile(window, dil):
    steps = window // dil
    qr, nat = _natural_index(dil)
    qn = nat[:, None] + qr
    kn = jnp.concatenate([nat, nat + qr])[None, :]
    step = qn - kn
    bucket = _t5_bucket(jnp.clip(step, 0, steps) * dil)
    valid = (step >= 0) & (step <= steps)
    return jnp.where(valid, bucket, -1).astype(jnp.int32)


def _bias_kernel(rb_ref, idx_ref, o_ref, *, heads):
    idx = idx_ref[...]
    for h in range(heads):
        acc = jnp.full(idx.shape, NEG, F32)
        for bk in range(NUM_BUCKETS):
            acc = jnp.where(idx == bk, rb_ref[bk, h], acc)
        o_ref[h] = acc


def attn_bias(rel_bias, window, dil):
    heads = rel_bias.shape[1]
    idx = _bucket_tile(window, dil)
    qr, qr2 = idx.shape
    return pl.pallas_call(
        functools.partial(_bias_kernel, heads=heads),
        grid=(1,),
        in_specs=[
            pl.BlockSpec(memory_space=pltpu.SMEM),
            pl.BlockSpec((qr, qr2), lambda i: (0, 0)),
        ],
        out_specs=pl.BlockSpec((heads, qr, qr2), lambda i: (0, 0, 0)),
        out_shape=jax.ShapeDtypeStruct((heads, qr, qr2), F32),
        compiler_params=_cparams(("arbitrary",)),
        name=f"attn_bias_d{dil}",
    )(rel_bias.astype(F32), idx)


def _group_perm():
    shift = ATT_RES.bit_length() - 1
    row = lax.broadcasted_iota(jnp.int32, (ATT_GROUP, ATT_GROUP), 0)
    col = lax.broadcasted_iota(jnp.int32, (ATT_GROUP, ATT_GROUP), 1)
    src = ((row & (ATT_RES - 1)) << shift) | (row >> shift)
    return jnp.where(col == src, 1.0, 0.0).astype(BF16)


def _attn_kernel(*refs, hb, qr, nqb, has_halo, has_prev, last, scale, grid, side_blocks):
    it = iter(refs)
    q_ref, k_ref, v_ref = next(it), next(it), next(it)
    kp_ref, vp_ref = (next(it), next(it)) if has_halo else (None, None)
    bias_ref = next(it)
    op_ref, lp_ref = (next(it), next(it)) if has_prev else (None, None)
    side_in = [next(it) for _ in side_blocks]
    o_ref = next(it)
    l_ref = None if last else next(it)
    side_out = [next(it) for _ in side_blocks]
    kbuf, vbuf, s_scr, p_scr, m_scr, d_scr, obuf, lbuf = (next(it) for _ in range(8))
    _side_cast_body(grid, side_in, side_out, side_blocks)
    t = pl.program_id(3)
    tq = qr * nqb
    wb = hb * HEAD_DIM
    nt = (((1,), (1,)), ((), ()))
    units = [(h, j) for h in range(hb) for j in range(nqb)]

    q = q_ref[...].reshape(tq, wb)
    kbuf[qr:, :] = k_ref[...].reshape(tq, wb)
    vbuf[qr:, :] = v_ref[...].reshape(tq, wb)
    if has_halo:
        kbuf[0:qr, :] = kp_ref[...].reshape(qr, wb)
        vbuf[0:qr, :] = vp_ref[...].reshape(qr, wb)
        first = t == 0
    else:
        kbuf[0:qr, :] = jnp.zeros((qr, wb), BF16)
        vbuf[0:qr, :] = jnp.zeros((qr, wb), BF16)
        first = t >= 0
    prev_cols = lax.broadcasted_iota(jnp.int32, (qr, 2 * qr), 1) < qr

    for u, (h, j) in enumerate(units):
        lanes = slice(h * HEAD_DIM, (h + 1) * HEAD_DIM)
        s = lax.dot_general(q[j * qr:(j + 1) * qr, lanes], kbuf[j * qr:(j + 2) * qr, lanes], nt,
                            preferred_element_type=F32)
        s = s * scale + bias_ref[h]
        if j == 0:
            s = jnp.where(prev_cols & first, NEG, s)
        s_scr[u] = s

    for u in range(len(units)):
        s = s_scr[u]
        m = jnp.max(s, axis=-1, keepdims=True)
        p = jnp.exp(s - m)
        m_scr[u] = jnp.broadcast_to(m, (qr, HEAD_DIM))
        d_scr[u] = jnp.broadcast_to(jnp.sum(p, axis=-1, keepdims=True), (qr, HEAD_DIM))
        p_scr[u] = p.astype(BF16)

    if has_prev:
        o_prev = op_ref[...].reshape(tq, wb).astype(F32)
        l_prev = lp_ref[...].reshape(tq, wb)
    for u, (h, j) in enumerate(units):
        lanes = slice(h * HEAD_DIM, (h + 1) * HEAD_DIM)
        rows = slice(j * qr, (j + 1) * qr)
        acc = jnp.dot(p_scr[u], vbuf[j * qr:(j + 2) * qr, lanes], preferred_element_type=F32)
        den = d_scr[u]
        o = acc / den
        lse = m_scr[u] + jnp.log(den)
        if has_prev:
            lse0 = l_prev[rows, lanes]
            mx = jnp.maximum(lse0, lse)
            w0 = jnp.exp(lse0 - mx)
            w1 = jnp.exp(lse - mx)
            tot = w0 + w1
            o = (w0 * o_prev[rows, lanes] + w1 * o) / tot
            lse = mx + jnp.log(tot)
        obuf[rows, lanes] = o
        if not last:
            lbuf[rows, lanes] = lse

    if last:
        res = jnp.dot(_group_perm(), obuf[...].astype(BF16), preferred_element_type=F32)
        o_ref[...] = res.astype(o_ref.dtype).reshape(o_ref.shape)
    else:
        o_ref[...] = obuf[...].astype(o_ref.dtype).reshape(o_ref.shape)
        l_ref[...] = lbuf[...].reshape(l_ref.shape)


def attn_pattern(u3, bias, dil, prev, *, mix, last, side_casts=(), hb=8):
    bsz, seq, _ = u3.shape
    heads = mix // HEAD_DIM
    ng = seq // ATT_GROUP
    wb = hb * HEAD_DIM
    q0, k0, v0 = (2 * mix) // wb, (3 * mix) // wb, (4 * mix) // wb
    qr = bias.shape[1]
    halo = None
    if dil == 1:
        nqb, n_res, n_t = 1, 1, ng
        view = lambda a: a.reshape(bsz, ng, ATT_GROUP, a.shape[-1])
        cur = lambda c0: pl.BlockSpec((1, 1, ATT_GROUP, wb), lambda g, b, r, t: (b, t, 0, c0 + g))
        halo = lambda c0: pl.BlockSpec((1, 1, ATT_GROUP, wb),
                                       lambda g, b, r, t: (b, jnp.maximum(t - 1, 0), 0, c0 + g))
    elif dil == 4:
        nqb, n_res, gt = 2, 4, 4
        n_t = ng // gt
        view = lambda a: a.reshape(bsz, ng, 4, 4, ATT_RES, a.shape[-1])
        cur = lambda c0: pl.BlockSpec((1, gt, 4, 1, ATT_RES, wb), lambda g, b, r, t: (b, t, 0, r, 0, c0 + g))
        halo = lambda c0: pl.BlockSpec((1, gt // 2, 4, 1, ATT_RES, wb),
                                       lambda g, b, r, t: (b, jnp.maximum(2 * t - 1, 0), 0, r, 0, c0 + g))
    else:
        assert dil == ATT_RES and ng * ATT_RES == 2 * HEAD_DIM
        nqb, n_res, n_t = 2, ATT_RES, 1
        view = lambda a: a.reshape(bsz, ng, ATT_RES, ATT_RES, a.shape[-1])
        cur = lambda c0: pl.BlockSpec((1, ng, 1, ATT_RES, wb), lambda g, b, r, t: (b, 0, r, 0, c0 + g))
    has_halo = halo is not None
    tq = qr * nqb
    uv = view(u3)
    in_specs = [cur(q0), cur(k0), cur(v0)]
    args = [uv, uv, uv]
    if has_halo:
        in_specs += [halo(k0), halo(v0)]
        args += [uv, uv]
    in_specs.append(pl.BlockSpec((hb, qr, 2 * qr), lambda g, b, r, t: (g, 0, 0)))
    args.append(bias)
    if prev is not None:
        in_specs += [cur(0), cur(0)]
        args += [view(prev[0]), view(prev[1])]
    grid = (heads // hb, bsz, n_res, n_t)
    s_in, s_out, s_shapes, side_blocks, s_args = _side_cast_specs(side_casts, grid)
    in_specs += s_in
    args += s_args
    if last:
        assert dil == 1
        out_specs = [pl.BlockSpec((1, ATT_GROUP, wb), lambda g, b, r, t: (b, t, g))]
        out_shape = [jax.ShapeDtypeStruct((bsz, seq, mix), BF16)]
    else:
        out_specs = [cur(0), cur(0)]
        out_shape = [jax.eval_shape(view, jax.ShapeDtypeStruct((bsz, seq, mix), dt)) for dt in (BF16, F32)]
    n_main = len(out_specs)
    n_u = hb * nqb
    res = pl.pallas_call(
        functools.partial(_attn_kernel, hb=hb, qr=qr, nqb=nqb, has_halo=has_halo, has_prev=prev is not None,
                          last=last, scale=HEAD_DIM ** -0.5, grid=grid, side_blocks=side_blocks),
        grid=grid,
        in_specs=in_specs,
        out_specs=out_specs + s_out,
        out_shape=out_shape + s_shapes,
        scratch_shapes=[
            pltpu.VMEM((tq + qr, wb), BF16), pltpu.VMEM((tq + qr, wb), BF16),
            pltpu.VMEM((n_u, qr, 2 * qr), F32), pltpu.VMEM((n_u, qr, 2 * qr), BF16),
            pltpu.VMEM((n_u, qr, HEAD_DIM), F32), pltpu.VMEM((n_u, qr, HEAD_DIM), F32),
            pltpu.VMEM((tq, wb), F32), pltpu.VMEM((tq, wb), F32),
        ],
        compiler_params=_cparams(("arbitrary",) * 4),
        name=f"attn_d{dil}",
    )(*args)
    sides = list(res[n_main:])
    if last:
        return res[0], sides
    return (res[0].reshape(bsz, seq, mix), res[1].reshape(bsz, seq, mix)), sides


def dilated_attention(u3, rel_bias, mix, side_casts=()):
    order = sorted(DSW_PATTERNS, key=lambda wd: wd[1] == 1)
    prev = None
    sides = []
    for gi, (window, dil) in enumerate(order):
        assert window // dil == HEAD_DIM
        prev, side = attn_pattern(u3, attn_bias(rel_bias, window, dil), dil, prev, mix=mix,
                                  last=gi == len(order) - 1, side_casts=side_casts[gi] if side_casts else ())
        sides += side
    return prev, sides


def _split3(v):
    hi = v.astype(BF16)
    r1 = v - hi.astype(F32)
    mid = r1.astype(BF16)
    lo = (r1 - mid.astype(F32)).astype(BF16)
    return hi, mid, lo


def _scaled(v, expo, mask):
    return jnp.where(mask, v * jnp.exp(jnp.where(mask, expo, 0.0)), 0.0)


def _hgrn_kernel(*refs, hb, ts, layer, grid, side_blocks):
    n_side = len(side_blocks)
    cq_ref, cf_ref, ci_ref, cg_ref, lb_ref, ng_ref = refs[:6]
    side_in = refs[6:6 + n_side]
    o_ref = refs[6 + n_side]
    side_out = refs[7 + n_side:7 + 2 * n_side]
    st_ref, q_s, kk_s, b_s, a_s = refs[7 + 2 * n_side:]
    _side_cast_body(grid, side_in, side_out, side_blocks)
    s = pl.program_id(2)
    c_len = HGRN_CHUNK
    sub = 16
    sub8 = 8
    nt = (((1,), (1,)), ((), ()))
    tn_ = (((0,), (0,)), ((), ()))

    @pl.when(s == 0)
    def _():
        st_ref[...] = jnp.zeros(st_ref.shape, F32)

    row = lax.broadcasted_iota(jnp.int32, (c_len, c_len), 0)
    col = lax.broadcasted_iota(jnp.int32, (c_len, c_len), 1)
    tri = jnp.where(row >= col, 1.0, 0.0).astype(BF16)
    r64 = lax.broadcasted_iota(jnp.int32, (c_len, HEAD_DIM), 0)
    half = c_len // 2
    mask_b = (((row >= sub) & (row < half) & (col < sub))
              | ((row >= half + sub) & (col >= half) & (col < half + sub)))
    mask_c = ((row >> 4) == (col >> 4)) & ((row & (sub - 1)) >= sub8) & ((col & (sub - 1)) < sub8)
    lrow = lax.broadcasted_iota(jnp.int32, (lb_ref.shape[0], HEAD_DIM), 0)

    def chunk_body(ci, carry):
        r0 = pl.multiple_of(ci * c_len, c_len)
        rows = pl.ds(r0, c_len)
        head_lanes = [slice(h * HEAD_DIM, (h + 1) * HEAD_DIM) for h in range(hb)]
        for h, lanes in enumerate(head_lanes):
            lg = lb_ref[:, lanes]
            pe = jnp.exp(lg - jnp.max(lg, axis=0, keepdims=True))
            lb = (jnp.sum(jnp.where((lrow >= 1) & (lrow <= layer), pe, 0.0), axis=0, keepdims=True)
                  / jnp.sum(pe, axis=0, keepdims=True))
            f = lb + (1.0 - lb) * jax.nn.sigmoid(cf_ref[0, rows, lanes].astype(F32))
            hi, mid, lo = _split3(jnp.log(f))
            q_s[h] = _silu(cq_ref[0, rows, lanes].astype(F32))
            kk_s[h] = 1.0 - f
            b_s[h] = (jnp.dot(tri, hi, preferred_element_type=F32)
                      + jnp.dot(tri, mid, preferred_element_type=F32)
                      + jnp.dot(tri, lo, preferred_element_type=F32))

        for h in range(hb):
            q, kk, b = q_s[h], kk_s[h], b_s[h]
            b_a = b_s[h, half - 1:half, :]
            qa = _scaled(q, b - b_a, r64 >= half)
            ka = _scaled(kk, b_a - b, r64 < half)
            attn = lax.dot_general(qa.astype(BF16), ka.astype(BF16), nt, preferred_element_type=F32)
            b_r = jnp.where(r64 < half, b_s[h, sub - 1:sub, :], b_s[h, half + sub - 1:half + sub, :])
            qsel = ((r64 >= sub) & (r64 < half)) | (r64 >= half + sub)
            ksel = (r64 < sub) | ((r64 >= half) & (r64 < half + sub))
            qbm = _scaled(q, b - b_r, qsel)
            kbm = _scaled(kk, b_r - b, ksel)
            attn_b = lax.dot_general(qbm.astype(BF16), kbm.astype(BF16), nt, preferred_element_type=F32)
            b_c = b_s[h, sub8 - 1:sub8, :]
            for a in range(1, c_len // sub):
                b_c = jnp.where(r64 >= a * sub, b_s[h, a * sub + sub8 - 1:a * sub + sub8, :], b_c)
            upper = (r64 & (sub - 1)) >= sub8
            qcm = _scaled(q, b - b_c, upper)
            kcm = _scaled(kk, b_c - b, jnp.logical_not(upper))
            attn_c = lax.dot_general(qcm.astype(BF16), kcm.astype(BF16), nt, preferred_element_type=F32)
            a_s[h, :, 0:c_len] = attn + jnp.where(mask_b, attn_b, 0.0) + jnp.where(mask_c, attn_c, 0.0)

        for h in range(hb):
            for jb in range(c_len // sub8):
                blk = slice(jb * sub8, (jb + 1) * sub8)
                qt = q_s[h, blk, :]
                bt = b_s[h, blk, :]
                for si in range(sub8):
                    r = jb * sub8 + si
                    e = jnp.exp(bt - b_s[h, r:r + 1, :])
                    a_s[h, blk, r:r + 1] = jnp.sum(qt * kk_s[h, r:r + 1, :] * e, axis=-1, keepdims=True)

        for h, lanes in enumerate(head_lanes):
            q, kk, b = q_s[h], kk_s[h], b_s[h]
            v = ci_ref[0, rows, lanes]
            b_last = b_s[h, c_len - 1:c_len, :]
            st_t = st_ref[h]
            inter = lax.dot_general((q * jnp.exp(b)).astype(BF16), st_t.astype(BF16), nt,
                                    preferred_element_type=F32)
            attn = jnp.where(row >= col, a_s[h, :, 0:c_len], 0.0)
            o = inter + jnp.dot(attn.astype(BF16), v, preferred_element_type=F32)
            kd = (kk * jnp.exp(b_last - b)).astype(BF16)
            st_ref[h] = st_t * jnp.exp(b_last) + lax.dot_general(v, kd, tn_, preferred_element_type=F32)
            ms = jnp.mean(o * o, axis=-1, keepdims=True)
            o = o * lax.rsqrt(ms + LN_EPS) * ng_ref[0:1, lanes]
            o = o * _silu(cg_ref[0, rows, lanes].astype(F32))
            o_ref[0, rows, lanes] = o.astype(o_ref.dtype)
        return carry
    lax.fori_loop(0, ts // c_len, chunk_body, 0)


def hgrn_branch(u3, lb_logits, layer, norm_g, mix, *, side_casts=(), hb=8, ts=256):
    bsz, seq, _ = u3.shape
    heads = mix // HEAD_DIM
    wb = hb * HEAD_DIM
    nb = mix // wb
    grid = (bsz, heads // hb, seq // ts)
    s_in, s_out, s_shapes, side_blocks, s_args = _side_cast_specs(side_casts, grid)

    def col(k):
        return pl.BlockSpec((1, ts, wb), lambda b, g, s: (b, s, k * nb + g))

    vec = pl.BlockSpec((1, wb), lambda b, g, s: (0, g))
    res = pl.pallas_call(
        functools.partial(_hgrn_kernel, hb=hb, ts=ts, layer=layer, grid=grid, side_blocks=side_blocks),
        grid=grid,
        in_specs=[col(0), col(1), col(2), col(3),
                  pl.BlockSpec((lb_logits.shape[0], wb), lambda b, g, s: (0, g)), vec] + s_in,
        out_specs=[pl.BlockSpec((1, ts, wb), lambda b, g, s: (b, s, g))] + s_out,
        out_shape=[jax.ShapeDtypeStruct((bsz, seq, mix), BF16)] + s_shapes,
        scratch_shapes=[pltpu.VMEM((hb, HEAD_DIM, HEAD_DIM), F32)]
        + [pltpu.VMEM((hb, HGRN_CHUNK, HEAD_DIM), F32)] * 4,
        compiler_params=_cparams(("arbitrary",) * 3),
        name="hgrn",
    )(u3, u3, u3, u3, lb_logits.astype(F32), norm_g.reshape(1, mix), *s_args)
    return res[0], list(res[1:])


POOL_HALO = 16


def _pool_kernel(dp_ref, pw_ref, ps_ref, o_ref, x_ref, p_ref, *, ts, rc):
    s = pl.program_id(1)
    ch = o_ref.shape[2]
    grp = ch // len(POOL_WINDOWS)

    @pl.when(s == 0)
    def _():
        x_ref[:, 0:POOL_HALO, :] = jnp.zeros((ch // 128, POOL_HALO, 128), F32)

    @pl.when(s > 0)
    def _():
        x_ref[:, 0:POOL_HALO, :] = x_ref[:, ts:ts + POOL_HALO, :]

    def load_body(i, carry):
        r0 = pl.multiple_of(i * rc, rc)
        xv = dp_ref[0, pl.ds(r0, rc), :].astype(F32)
        for cc in range(ch // 128):
            x_ref[cc, pl.ds(POOL_HALO + r0, rc), :] = xv[:, cc * 128:(cc + 1) * 128]
        return carry
    lax.fori_loop(0, ts // rc, load_body, 0)

    def pool_body(i, carry):
        r0 = pl.multiple_of(i * rc, rc)
        pos = s * ts + r0 + lax.broadcasted_iota(jnp.int32, (rc, 1), 0)
        for cc in range(ch // 128):
            w = POOL_WINDOWS[(cc * 128) // grp]
            cur = x_ref[cc, pl.ds(POOL_HALO + r0, rc), :]
            tot = cur
            for j in range(1, w):
                tot = tot + x_ref[cc, pl.ds(POOL_HALO + r0 - j, rc), :]
            cnt = jnp.minimum(pos + 1, w).astype(F32)
            p_ref[pl.ds(r0, rc), cc * 128:(cc + 1) * 128] = (tot / cnt - cur).astype(BF16)
        return carry
    lax.fori_loop(0, ts // rc, pool_body, 0)

    for gi in range(len(POOL_WINDOWS)):
        lanes = slice(gi * grp, (gi + 1) * grp)
        y = jnp.dot(p_ref[:, lanes], pw_ref[gi], preferred_element_type=F32)
        o_ref[0, :, lanes] = (y * ps_ref[0:1, lanes]).astype(o_ref.dtype)


def pool_branch(u3, pool_w, pool_scale, mix, *, ts=512, rc=32):
    bsz, seq, _ = u3.shape
    ng, grp, _ = pool_w.shape
    return pl.pallas_call(
        functools.partial(_pool_kernel, ts=ts, rc=rc),
        grid=(bsz, seq // ts),
        in_specs=[
            pl.BlockSpec((1, ts, mix), lambda b, s: (b, s, 4)),
            pl.BlockSpec((ng, grp, grp), lambda b, s: (0, 0, 0)),
            pl.BlockSpec((1, mix), lambda b, s: (0, 0)),
        ],
        out_specs=pl.BlockSpec((1, ts, mix), lambda b, s: (b, s, 0)),
        out_shape=jax.ShapeDtypeStruct((bsz, seq, mix), BF16),
        scratch_shapes=[pltpu.VMEM((mix // 128, ts + POOL_HALO, 128), F32), pltpu.VMEM((ts, mix), BF16)],
        compiler_params=_cparams(("arbitrary", "arbitrary")),
        name="pool_branch",
    )(u3, pool_w.astype(BF16), pool_scale.reshape(1, mix))


def kernel(x, c, ada_w, ada_b, w_in, w_out, ln_g, ln_b, mlp_w1, mlp_w2, conv_w, conv_b, conv_ln_g, conv_ln_b,
           rel_bias, hgrn_lb_logits, hgrn_norm_g, pool_w, pool_scale):
    bsz, seq, d = x.shape
    depth = ada_w.shape[0]
    mix = d // 2
    alpha = (2.0 * depth) ** 0.25
    m = bsz * seq

    mod = adaln_mod(c, ada_w, ada_b)
    x2 = x.reshape(m, d)
    h = modulate(x2, mod[0], seq)
    w_in_b = cast_layer_bf16(w_in, 0)
    for l in range(depth):
        u = mm_in(h, w_in_b, perm_from=2 * mix if l % 2 == 0 else None)
        u3 = u.reshape(bsz, seq, 5 * mix)
        if l % 2 == 0:
            e = l // 2
            conv_sides = [(mlp_w2, l, 256)] + ([(w_in, l + 1, 64)] if l + 1 < depth else [])
            za, sides = conv_branch(u3, conv_w[e], conv_b[e], conv_ln_g[e], conv_ln_b[e], mix,
                                    side_casts=conv_sides)
            w2_b = sides[0]
            if l + 1 < depth:
                w_in_b = sides[1]
            zb, (w_out_b, w1_b) = dilated_attention(
                u3, rel_bias, mix, side_casts=[[], [(w_out, l, 32)], [(mlp_w1, l, 32)]])
        else:
            if l + 1 < depth:
                w_in_b = cast_layer_bf16(w_in, l + 1)
            o = l // 2
            za, (w_out_b, w1_b, w2_b) = hgrn_branch(
                u3, hgrn_lb_logits, l, hgrn_norm_g[o], mix,
                side_casts=[(w_out, l, 32), (mlp_w1, l, 32), (mlp_w2, l, 128)])
            zb = pool_branch(u3, pool_w[o], pool_scale[o], mix)
        x2, h2 = mm_out_ln(za.reshape(m, mix), zb.reshape(m, mix), w_out_b, x2, mod[l],
                           ln_g[l, 0], ln_b[l, 0], seq, alpha)
        x2, h = mlp_ln(h2, x2, mod[l], mod[l + 1] if l + 1 < depth else None,
                       w1_b, w2_b, ln_g[l, 1], ln_b[l, 1], seq, alpha)
    return x2.reshape(bsz, seq, d)
```

```python
import functools
import math

import jax
import jax.numpy as jnp
from jax import lax
from jax.experimental import pallas as pl
from jax.experimental.pallas import tpu as pltpu

F32 = jnp.float32
BF16 = jnp.bfloat16

HEAD_DIM = 128
CONV_K = 31
DSW_PATTERNS = ((128, 1), (512, 4), (2048, 16))
NUM_BUCKETS = 32
MAX_DISTANCE = 2048
HGRN_CHUNK = 64
POOL_WINDOWS = (2, 4, 8, 16)
LN_EPS = 1e-5
NEG = -1e30

V7X_VMEM_BYTES = 64 * 1024 * 1024
VMEM_LIMIT = V7X_VMEM_BYTES - 2 * 1024 * 1024


def _cparams(sem):
    return pltpu.CompilerParams(dimension_semantics=sem, vmem_limit_bytes=VMEM_LIMIT)


def _silu(v):
    return v * jax.nn.sigmoid(v)


def _mod_kernel(c_ref, w_ref, b_ref, o_ref):
    cs = _silu(c_ref[...]).astype(BF16)
    w = w_ref[0].astype(BF16)
    o_ref[0] = jnp.dot(cs, w, preferred_element_type=F32) + b_ref[0]


def adaln_mod(c, ada_w, ada_b, *, tn=512):
    nl, d, n6 = ada_w.shape
    bsz = c.shape[0]
    rows = 8
    c8 = jnp.zeros((rows, d), F32).at[:bsz].set(c)
    out = pl.pallas_call(
        _mod_kernel,
        grid=(nl, n6 // tn),
        in_specs=[
            pl.BlockSpec((rows, d), lambda l, j: (0, 0)),
            pl.BlockSpec((1, d, tn), lambda l, j: (l, 0, j)),
            pl.BlockSpec((1, 1, tn), lambda l, j: (l, 0, j)),
        ],
        out_specs=pl.BlockSpec((1, rows, tn), lambda l, j: (l, 0, j)),
        out_shape=jax.ShapeDtypeStruct((nl, rows, n6), F32),
        compiler_params=_cparams(("arbitrary", "arbitrary")),
        name="adaln_mod",
    )(c8, ada_w, ada_b.reshape(nl, 1, n6))
    return out[:, :bsz].reshape(nl, bsz, 6, d)


LN_ROWS = 16
LN_COLS = 512


def _ln_chunks(chunks, d, g_ref, b_ref):
    col_slices = [slice(c * LN_COLS, (c + 1) * LN_COLS) for c in range(d // LN_COLS)]
    mus = []
    for load, _ in chunks:
        tot = load(col_slices[0])
        for cols in col_slices[1:]:
            tot = tot + load(cols)
        mus.append(jnp.sum(tot, axis=-1, keepdims=True) * (1.0 / d))
    rstds = []
    for (load, _), mu in zip(chunks, mus):
        sq = None
        for cols in col_slices:
            dv = load(cols) - mu
            sq = dv * dv if sq is None else sq + dv * dv
        rstds.append(lax.rsqrt(jnp.sum(sq, axis=-1, keepdims=True) * (1.0 / d) + LN_EPS))
    for (load, emit), mu, rstd in zip(chunks, mus, rstds):
        for cols in col_slices:
            emit(cols, (load(cols) - mu) * rstd * g_ref[:, cols] + b_ref[:, cols])


def _ln_chunk(load, d, g_ref, b_ref, emit):
    _ln_chunks([(load, emit)], d, g_ref, b_ref)


def _ln_mod_phase(step, rows_per_step, src_ref, xr_ref, alpha, gate_ref, g_ref, b_ref, mod_ref, sh_row, sc_row,
                  xo_ref, ho_ref):
    d = xo_ref.shape[1]
    chunks = []
    for k in range(rows_per_step // LN_ROWS):
        src_rows = pl.ds(pl.multiple_of(step * rows_per_step + k * LN_ROWS, LN_ROWS), LN_ROWS)
        out_rows = slice(k * LN_ROWS, (k + 1) * LN_ROWS)
        if xr_ref is not None:
            for c in range(d // LN_COLS):
                cols = slice(c * LN_COLS, (c + 1) * LN_COLS)
                xo_ref[out_rows, cols] = alpha * xr_ref[out_rows, cols] + gate_ref[0, 5:6, cols] * src_ref[src_rows, cols]
            load = lambda cols, out_rows=out_rows: xo_ref[out_rows, cols]
        else:
            load = lambda cols, src_rows=src_rows: src_ref[src_rows, cols]

        def emit(cols, y, out_rows=out_rows):
            xo_ref[out_rows, cols] = y
            if ho_ref is not None:
                h = y * (1.0 + mod_ref[0, sc_row:sc_row + 1, cols]) + mod_ref[0, sh_row:sh_row + 1, cols]
                ho_ref[out_rows, cols] = h.astype(BF16)
        chunks.append((load, emit))
    _ln_chunks(chunks, d, g_ref, b_ref)


def _cast_kernel(w_ref, o_ref):
    o_ref[...] = w_ref[0].astype(o_ref.dtype)


def cast_layer_bf16(w_stack, layer, *, tr=512, tc=2048):
    _, rows, cols = w_stack.shape
    tc = min(tc, cols)
    return pl.pallas_call(
        _cast_kernel,
        grid=(rows // tr, cols // tc),
        in_specs=[pl.BlockSpec((1, tr, tc), lambda i, j: (layer, i, j))],
        out_specs=pl.BlockSpec((tr, tc), lambda i, j: (i, j)),
        out_shape=jax.ShapeDtypeStruct((rows, cols), BF16),
        compiler_params=_cparams(("arbitrary", "arbitrary")),
        name="cast_bf16",
    )(w_stack)


def _modulate_kernel(x_ref, mod_ref, o_ref):
    o_ref[...] = (x_ref[...] * (1.0 + mod_ref[0, 1:2, :]) + mod_ref[0, 0:1, :]).astype(o_ref.dtype)


def modulate(x2, mod_l, seq, *, tm=256):
    m, d = x2.shape
    return pl.pallas_call(
        _modulate_kernel,
        grid=(m // tm,),
        in_specs=[
            pl.BlockSpec((tm, d), lambda i: (i, 0)),
            pl.BlockSpec((1, 6, d), lambda i: ((i * tm) // seq, 0, 0)),
        ],
        out_specs=pl.BlockSpec((tm, d), lambda i: (i, 0)),
        out_shape=jax.ShapeDtypeStruct((m, d), BF16),
        compiler_params=_cparams(("arbitrary",)),
        name="modulate",
    )(x2, mod_l)


def _side_cast_specs(side_casts, grid):
    def linear(*ids):
        step = ids[0]
        for size, idx in zip(grid[1:], ids[1:]):
            step = step * size + idx
        return step

    in_specs, out_specs, out_shapes, n_blocks = [], [], [], []
    for arr, layer, br in side_casts:
        _, rows, cols = arr.shape
        nb = rows // br
        assert rows % br == 0 and nb <= math.prod(grid)
        blk = lambda *ids, nb=nb: jnp.minimum(linear(*ids), nb - 1)
        in_specs.append(pl.BlockSpec((1, br, cols), lambda *ids, layer=layer, blk=blk: (layer, blk(*ids), 0)))
        out_specs.append(pl.BlockSpec((br, cols), lambda *ids, blk=blk: (blk(*ids), 0)))
        out_shapes.append(jax.ShapeDtypeStruct((rows, cols), BF16))
        n_blocks.append(nb)
    return in_specs, out_specs, out_shapes, tuple(n_blocks), [a for a, _, _ in side_casts]


def _side_cast_body(grid, side_in, side_out, n_blocks):
    step = pl.program_id(0)
    for ax in range(1, len(grid)):
        step = step * grid[ax] + pl.program_id(ax)
    for src, dst, nb in zip(side_in, side_out, n_blocks):
        @pl.when(step < nb)
        def _(src=src, dst=dst):
            dst[...] = src[0].astype(dst.dtype)


def _mm_in_kernel(h_ref, w_ref, o_ref, *scratch, perm_tile):
    if perm_tile is None:
        o_ref[...] = jnp.dot(h_ref[...], w_ref[...], preferred_element_type=F32).astype(o_ref.dtype)
        return
    hp_ref, = scratch
    j = pl.program_id(1)

    @pl.when(j == 0)
    def _():
        perm = _group_perm()
        for g in range(h_ref.shape[0] // ATT_GROUP):
            rows = slice(g * ATT_GROUP, (g + 1) * ATT_GROUP)
            hp_ref[rows, :] = jnp.dot(perm, h_ref[rows, :], preferred_element_type=F32).astype(BF16)

    @pl.when(j < perm_tile)
    def _():
        o_ref[...] = jnp.dot(h_ref[...], w_ref[...], preferred_element_type=F32).astype(o_ref.dtype)

    @pl.when(j >= perm_tile)
    def _():
        o_ref[...] = jnp.dot(hp_ref[...], w_ref[...], preferred_element_type=F32).astype(o_ref.dtype)


def mm_in(h, w, *, perm_from=None, tm=1024, tn=1024):
    m, d = h.shape
    n = w.shape[1]
    perm_tile = None
    scratch = []
    if perm_from is not None:
        assert perm_from % tn == 0 and tm % ATT_GROUP == 0
        perm_tile = perm_from // tn
        scratch = [pltpu.VMEM((tm, d), BF16)]
    return pl.pallas_call(
        functools.partial(_mm_in_kernel, perm_tile=perm_tile),
        grid=(m // tm, n // tn),
        in_specs=[
            pl.BlockSpec((tm, d), lambda i, j: (i, 0)),
            pl.BlockSpec((d, tn), lambda i, j: (0, j)),
        ],
        out_specs=pl.BlockSpec((tm, tn), lambda i, j: (i, j)),
        out_shape=jax.ShapeDtypeStruct((m, n), BF16),
        scratch_shapes=scratch,
        compiler_params=_cparams(("arbitrary", "arbitrary")),
        name="mm_in",
    )(h, w)


def _mm_out_kernel(a_ref, b_ref, w_ref, x_ref, mod_ref, modp_ref, g_ref, beta_ref, xo_ref, ho_ref, acc_ref, ln_ref,
                   *, n_t, n_n, tn, k1, alpha):
    i = pl.program_id(0)
    n = pl.program_id(1)
    rows_per_step = acc_ref.shape[0] // n_n

    @pl.when((i == 0) & (n == 0))
    def _():
        ln_ref[...] = jnp.zeros(ln_ref.shape, F32)

    def ln_phase():
        _ln_mod_phase(n, rows_per_step, ln_ref, None, alpha, None, g_ref, beta_ref, modp_ref, 3, 4, xo_ref, ho_ref)

    @pl.when(i < n_t)
    def _():
        y = jnp.dot(a_ref[...], w_ref[0:k1, :], preferred_element_type=F32)
        y = y + jnp.dot(b_ref[...], w_ref[k1:, :], preferred_element_type=F32)
        cols = pl.ds(pl.multiple_of(n * tn, tn), tn)
        acc_ref[:, cols] = alpha * x_ref[...] + mod_ref[0, 2:3, cols] * y
        ln_phase()

    @pl.when(i == n_t)
    def _():
        ln_phase()

    @pl.when((i < n_t) & (n == n_n - 1))
    def _():
        ln_ref[...] = acc_ref[...]


def mm_out_ln(za, zb, w, x2, mod_l, ln_g, ln_b, seq, alpha, *, tm=512, tn=512):
    m, k1 = za.shape
    d = w.shape[1]
    n_n = d // tn
    n_t = m // tm
    rl = tm // n_n
    assert rl % LN_ROWS == 0
    cur = lambda i: jnp.minimum(i, n_t - 1)
    prv = lambda i: jnp.maximum(i - 1, 0)
    col = lambda i, n: jnp.where(i < n_t, n, n_n - 1)
    out_row = pl.BlockSpec((rl, d), lambda i, n: (jnp.where(i == 0, 0, (i - 1) * n_n + n), 0))
    return pl.pallas_call(
        functools.partial(_mm_out_kernel, n_t=n_t, n_n=n_n, tn=tn, k1=k1, alpha=alpha),
        grid=(n_t + 1, n_n),
        in_specs=[
            pl.BlockSpec((tm, k1), lambda i, n: (cur(i), 0)),
            pl.BlockSpec((tm, zb.shape[1]), lambda i, n: (cur(i), 0)),
            pl.BlockSpec((w.shape[0], tn), lambda i, n: (0, col(i, n))),
            pl.BlockSpec((tm, tn), lambda i, n: (cur(i), col(i, n))),
            pl.BlockSpec((1, 6, d), lambda i, n: ((cur(i) * tm) // seq, 0, 0)),
            pl.BlockSpec((1, 6, d), lambda i, n: ((prv(i) * tm) // seq, 0, 0)),
            pl.BlockSpec((1, d), lambda i, n: (0, 0)),
            pl.BlockSpec((1, d), lambda i, n: (0, 0)),
        ],
        out_specs=[out_row, out_row],
        out_shape=[jax.ShapeDtypeStruct((m, d), F32), jax.ShapeDtypeStruct((m, d), BF16)],
        scratch_shapes=[pltpu.VMEM((tm, d), F32), pltpu.VMEM((tm, d), F32)],
        compiler_params=_cparams(("arbitrary", "arbitrary")),
        name="mm_out_ln",
    )(za, zb, w, x2, mod_l, mod_l, ln_g.reshape(1, d), ln_b.reshape(1, d))


def _mlp_kernel(h_ref, xr_ref, modp_ref, modn_ref, w1_ref, w2_ref, g_ref, beta_ref, *rest,
                n_t, n_f, tn2, alpha, emit_h):
    if emit_h:
        xo_ref, ho_ref, acc_ref, ln_ref = rest
    else:
        xo_ref, acc_ref, ln_ref = rest
        ho_ref = None
    i = pl.program_id(0)
    f = pl.program_id(1)
    d = acc_ref.shape[1]
    rows_per_step = acc_ref.shape[0] // n_f

    @pl.when((i == 0) & (f == 0))
    def _():
        ln_ref[...] = jnp.zeros(ln_ref.shape, F32)

    @pl.when((i < n_t) & (f == 0))
    def _():
        acc_ref[...] = jnp.zeros(acc_ref.shape, F32)

    def ln_phase():
        _ln_mod_phase(f, rows_per_step, ln_ref, xr_ref, alpha, modp_ref, g_ref, beta_ref, modn_ref, 0, 1,
                      xo_ref, ho_ref)

    @pl.when(i < n_t)
    def _():
        t = jnp.dot(h_ref[...], w1_ref[...], preferred_element_type=F32)
        t = jnp.maximum(t, 0.0)
        t = (t * t).astype(BF16)
        for nb in range(d // tn2):
            cols = slice(nb * tn2, (nb + 1) * tn2)
            acc_ref[:, cols] += jnp.dot(t, w2_ref[:, cols], preferred_element_type=F32)
        ln_phase()

    @pl.when(i == n_t)
    def _():
        ln_phase()

    @pl.when((i < n_t) & (f == n_f - 1))
    def _():
        ln_ref[...] = acc_ref[...]


def mlp_ln(h, x2, mod_l, mod_next, w1, w2, ln_g, ln_b, seq, alpha, *, tm=512, tf=1024, tn2=512):
    m, d = x2.shape
    dff = w1.shape[1]
    n_f = dff // tf
    n_t = m // tm
    rl = tm // n_f
    assert rl % LN_ROWS == 0
    emit_h = mod_next is not None
    cur = lambda i: jnp.minimum(i, n_t - 1)
    prv = lambda i: jnp.maximum(i - 1, 0)
    chunk = lambda i, f: jnp.where(i < n_t, f, n_f - 1)
    out_row = pl.BlockSpec((rl, d), lambda i, f: (jnp.where(i == 0, 0, (i - 1) * n_f + f), 0))
    modspec = pl.BlockSpec((1, 6, d), lambda i, f: ((prv(i) * tm) // seq, 0, 0))
    f32_out = jax.ShapeDtypeStruct((m, d), F32)
    res = pl.pallas_call(
        functools.partial(_mlp_kernel, n_t=n_t, n_f=n_f, tn2=tn2, alpha=alpha, emit_h=emit_h),
        grid=(n_t + 1, n_f),
        in_specs=[
            pl.BlockSpec((tm, d), lambda i, f: (cur(i), 0), pipeline_mode=pl.Buffered(1)),
            pl.BlockSpec((rl, d), lambda i, f: (prv(i) * n_f + f, 0)),
            modspec,
            modspec,
            pl.BlockSpec((d, tf), lambda i, f: (0, chunk(i, f))),
            pl.BlockSpec((tf, d), lambda i, f: (chunk(i, f), 0)),
            pl.BlockSpec((1, d), lambda i, f: (0, 0)),
            pl.BlockSpec((1, d), lambda i, f: (0, 0)),
        ],
        out_specs=[out_row, out_row] if emit_h else out_row,
        out_shape=[f32_out, jax.ShapeDtypeStruct((m, d), BF16)] if emit_h else f32_out,
        scratch_shapes=[pltpu.VMEM((tm, d), F32), pltpu.VMEM((tm, d), F32)],
        compiler_params=_cparams(("arbitrary", "arbitrary")),
        name="mlp_ln",
    )(h, x2, mod_l, mod_next if emit_h else mod_l, w1, w2, ln_g.reshape(1, d), ln_b.reshape(1, d))
    return res if emit_h else (res, None)


CONV_HALO = 32


def _conv_kernel(*refs, ts, rc, grid, side_blocks):
    n_side = len(side_blocks)
    av_ref, ag_ref, w_ref, cb_ref, g_ref, beta_ref = refs[:6]
    side_in = refs[6:6 + n_side]
    o_ref = refs[6 + n_side]
    side_out = refs[7 + n_side:7 + 2 * n_side]
    glu_ref, y_ref = refs[7 + 2 * n_side:]
    _side_cast_body(grid, side_in, side_out, side_blocks)
    s = pl.program_id(1)
    ch = o_ref.shape[2]

    @pl.when(s == 0)
    def _():
        glu_ref[:, 0:CONV_HALO, :] = jnp.zeros((ch // 128, CONV_HALO, 128), F32)

    @pl.when(s > 0)
    def _():
        glu_ref[:, 0:CONV_HALO, :] = glu_ref[:, ts:ts + CONV_HALO, :]

    def glu_body(i, carry):
        r0 = pl.multiple_of(i * rc, rc)
        a = av_ref[0, pl.ds(r0, rc), :].astype(F32)
        gt = ag_ref[0, pl.ds(r0, rc), :].astype(F32)
        glu = a * jax.nn.sigmoid(gt)
        for cc in range(ch // 128):
            glu_ref[cc, pl.ds(CONV_HALO + r0, rc), :] = glu[:, cc * 128:(cc + 1) * 128]
        return carry
    lax.fori_loop(0, ts // rc, glu_body, 0)

    off = CONV_HALO - (CONV_K - 1)

    def conv_body(i, carry):
        r0 = pl.multiple_of(i * rc, rc)
        for cc in range(ch // 128):
            lanes = slice(cc * 128, (cc + 1) * 128)
            acc = jnp.zeros((rc, 128), F32)
            for j in range(CONV_K):
                acc = acc + w_ref[j:j + 1, lanes] * glu_ref[cc, pl.ds(r0 + off + j, rc), :]
            y_ref[pl.ds(r0, rc), lanes] = acc + cb_ref[0:1, lanes]
        return carry
    lax.fori_loop(0, ts // rc, conv_body, 0)

    ln_group = 4

    def ln_body(i, carry):
        chunks = []
        for k in range(ln_group):
            rows = pl.ds(pl.multiple_of((i * ln_group + k) * LN_ROWS, LN_ROWS), LN_ROWS)

            def emit(cols, z, rows=rows):
                o_ref[0, rows, cols] = _silu(z).astype(o_ref.dtype)
            chunks.append((lambda cols, rows=rows: y_ref[rows, cols], emit))
        _ln_chunks(chunks, ch, g_ref, beta_ref)
        return carry
    lax.fori_loop(0, ts // (LN_ROWS * ln_group), ln_body, 0)


def conv_branch(u3, conv_w, conv_b, ln_g, ln_b, mix, *, side_casts=(), ts=256, rc=32):
    bsz, seq, _ = u3.shape
    grid = (bsz, seq // ts)
    s_in, s_out, s_shapes, side_blocks, s_args = _side_cast_specs(side_casts, grid)
    res = pl.pallas_call(
        functools.partial(_conv_kernel, ts=ts, rc=rc, grid=grid, side_blocks=side_blocks),
        grid=grid,
        in_specs=[
            pl.BlockSpec((1, ts, mix), lambda b, s: (b, s, 0)),
            pl.BlockSpec((1, ts, mix), lambda b, s: (b, s, 1)),
            pl.BlockSpec((CONV_K, mix), lambda b, s: (0, 0)),
            pl.BlockSpec((1, mix), lambda b, s: (0, 0)),
            pl.BlockSpec((1, mix), lambda b, s: (0, 0)),
            pl.BlockSpec((1, mix), lambda b, s: (0, 0)),
        ] + s_in,
        out_specs=[pl.BlockSpec((1, ts, mix), lambda b, s: (b, s, 0))] + s_out,
        out_shape=[jax.ShapeDtypeStruct((bsz, seq, mix), BF16)] + s_shapes,
        scratch_shapes=[pltpu.VMEM((mix // 128, ts + CONV_HALO, 128), F32), pltpu.VMEM((ts, mix), F32)],
        compiler_params=_cparams(("arbitrary", "arbitrary")),
        name="conv_branch",
    )(u3, u3, conv_w, conv_b.reshape(1, mix), ln_g.reshape(1, mix), ln_b.reshape(1, mix), *s_args)
    return res[0], list(res[1:])


def _t5_bucket(dist):
    max_exact = NUM_BUCKETS // 2
    nf = jnp.maximum(dist, 1).astype(F32)
    large = max_exact + (jnp.log(nf / max_exact) / math.log(MAX_DISTANCE / max_exact)
                         * (NUM_BUCKETS - max_exact)).astype(jnp.int32)
    large = jnp.minimum(large, NUM_BUCKETS - 1)
    return jnp.where(dist < max_exact, dist, large)


ATT_GROUP = 256
ATT_RES = 16


def _natural_index(dil):
    if dil == 1:
        i = jnp.arange(ATT_GROUP)
        return ATT_GROUP, ATT_RES * (i % ATT_RES) + i // ATT_RES
    if dil == 4:
        i = jnp.arange(HEAD_DIM)
        return HEAD_DIM, (i // 64) * 64 + 4 * (i % 16) + (i % 64) // 16
    assert dil == ATT_RES
    return HEAD_DIM, jnp.arange(HEAD_DIM)


def _bucket_tile(window, dil):
    steps = window // dil
    qr, nat = _natural_index(dil)
    qn = nat[:, None] + qr
    kn = jnp.concatenate([nat, nat + qr])[None, :]
    step = qn - kn
    bucket = _t5_bucket(jnp.clip(step, 0, steps) * dil)
    valid = (step >= 0) & (step <= steps)
    return jnp.where(valid, bucket, -1).astype(jnp.int32)


def _bias_kernel(rb_ref, idx_ref, o_ref, *, heads):
    idx = idx_ref[...]
    for h in range(heads):
        acc = jnp.full(idx.shape, NEG, F32)
        for bk in range(NUM_BUCKETS):
            acc = jnp.where(idx == bk, rb_ref[bk, h], acc)
        o_ref[h] = acc


def attn_bias(rel_bias, window, dil):
    heads = rel_bias.shape[1]
    idx = _bucket_tile(window, dil)
    qr, qr2 = idx.shape
    return pl.pallas_call(
        functools.partial(_bias_kernel, heads=heads),
        grid=(1,),
        in_specs=[
            pl.BlockSpec(memory_space=pltpu.SMEM),
            pl.BlockSpec((qr, qr2), lambda i: (0, 0)),
        ],
        out_specs=pl.BlockSpec((heads, qr, qr2), lambda i: (0, 0, 0)),
        out_shape=jax.ShapeDtypeStruct((heads, qr, qr2), F32),
        compiler_params=_cparams(("arbitrary",)),
        name=f"attn_bias_d{dil}",
    )(rel_bias.astype(F32), idx)


def _group_perm():
    shift = ATT_RES.bit_length() - 1
    row = lax.broadcasted_iota(jnp.int32, (ATT_GROUP, ATT_GROUP), 0)
    col = lax.broadcasted_iota(jnp.int32, (ATT_GROUP, ATT_GROUP), 1)
    src = ((row & (ATT_RES - 1)) << shift) | (row >> shift)
    return jnp.where(col == src, 1.0, 0.0).astype(BF16)


def _attn_kernel(*refs, hb, qr, nqb, has_halo, has_prev, last, scale, grid, side_blocks):
    it = iter(refs)
    q_ref, k_ref, v_ref = next(it), next(it), next(it)
    kp_ref, vp_ref = (next(it), next(it)) if has_halo else (None, None)
    bias_ref = next(it)
    op_ref, lp_ref = (next(it), next(it)) if has_prev else (None, None)
    side_in = [next(it) for _ in side_blocks]
    o_ref = next(it)
    l_ref = None if last else next(it)
    side_out = [next(it) for _ in side_blocks]
    kbuf, vbuf, s_scr, p_scr, m_scr, d_scr, obuf, lbuf = (next(it) for _ in range(8))
    _side_cast_body(grid, side_in, side_out, side_blocks)
    t = pl.program_id(3)
    tq = qr * nqb
    wb = hb * HEAD_DIM
    nt = (((1,), (1,)), ((), ()))
    units = [(h, j) for h in range(hb) for j in range(nqb)]

    q = q_ref[...].reshape(tq, wb)
    kbuf[qr:, :] = k_ref[...].reshape(tq, wb)
    vbuf[qr:, :] = v_ref[...].reshape(tq, wb)
    if has_halo:
        kbuf[0:qr, :] = kp_ref[...].reshape(qr, wb)
        vbuf[0:qr, :] = vp_ref[...].reshape(qr, wb)
        first = t == 0
    else:
        kbuf[0:qr, :] = jnp.zeros((qr, wb), BF16)
        vbuf[0:qr, :] = jnp.zeros((qr, wb), BF16)
        first = t >= 0
    prev_cols = lax.broadcasted_iota(jnp.int32, (qr, 2 * qr), 1) < qr

    for u, (h, j) in enumerate(units):
        lanes = slice(h * HEAD_DIM, (h + 1) * HEAD_DIM)
        s = lax.dot_general(q[j * qr:(j + 1) * qr, lanes], kbuf[j * qr:(j + 2) * qr, lanes], nt,
                            preferred_element_type=F32)
        s = s * scale + bias_ref[h]
        if j == 0:
            s = jnp.where(prev_cols & first, NEG, s)
        s_scr[u] = s

    for u in range(len(units)):
        s = s_scr[u]
        m = jnp.max(s, axis=-1, keepdims=True)
        p = jnp.exp(s - m)
        m_scr[u] = jnp.broadcast_to(m, (qr, HEAD_DIM))
        d_scr[u] = jnp.broadcast_to(jnp.sum(p, axis=-1, keepdims=True), (qr, HEAD_DIM))
        p_scr[u] = p.astype(BF16)

    if has_prev:
        o_prev = op_ref[...].reshape(tq, wb).astype(F32)
        l_prev = lp_ref[...].reshape(tq, wb)
    for u, (h, j) in enumerate(units):
        lanes = slice(h * HEAD_DIM, (h + 1) * HEAD_DIM)
        rows = slice(j * qr, (j + 1) * qr)
        acc = jnp.dot(p_scr[u], vbuf[j * qr:(j + 2) * qr, lanes], preferred_element_type=F32)
        den = d_scr[u]
        o = acc / den
        lse = m_scr[u] + jnp.log(den)
        if has_prev:
            lse0 = l_prev[rows, lanes]
            mx = jnp.maximum(lse0, lse)
            w0 = jnp.exp(lse0 - mx)
            w1 = jnp.exp(lse - mx)
            tot = w0 + w1
            o = (w0 * o_prev[rows, lanes] + w1 * o) / tot
            lse = mx + jnp.log(tot)
        obuf[rows, lanes] = o
        if not last:
            lbuf[rows, lanes] = lse

    if last:
        res = jnp.dot(_group_perm(), obuf[...].astype(BF16), preferred_element_type=F32)
        o_ref[...] = res.astype(o_ref.dtype).reshape(o_ref.shape)
    else:
        o_ref[...] = obuf[...].astype(o_ref.dtype).reshape(o_ref.shape)
        l_ref[...] = lbuf[...].reshape(l_ref.shape)


def attn_pattern(u3, bias, dil, prev, *, mix, last, side_casts=(), hb=16):
    bsz, seq, _ = u3.shape
    heads = mix // HEAD_DIM
    ng = seq // ATT_GROUP
    wb = hb * HEAD_DIM
    q0, k0, v0 = (2 * mix) // wb, (3 * mix) // wb, (4 * mix) // wb
    qr = bias.shape[1]
    halo = None
    if dil == 1:
        nqb, n_res, n_t = 1, 1, ng
        view = lambda a: a.reshape(bsz, ng, ATT_GROUP, a.shape[-1])
        cur = lambda c0: pl.BlockSpec((1, 1, ATT_GROUP, wb), lambda g, b, r, t: (b, t, 0, c0 + g))
        halo = lambda c0: pl.BlockSpec((1, 1, ATT_GROUP, wb),
                                       lambda g, b, r, t: (b, jnp.maximum(t - 1, 0), 0, c0 + g))
    elif dil == 4:
        nqb, n_res, gt = 2, 4, 4
        n_t = ng // gt
        view = lambda a: a.reshape(bsz, ng, 4, 4, ATT_RES, a.shape[-1])
        cur = lambda c0: pl.BlockSpec((1, gt, 4, 1, ATT_RES, wb), lambda g, b, r, t: (b, t, 0, r, 0, c0 + g))
        halo = lambda c0: pl.BlockSpec((1, gt // 2, 4, 1, ATT_RES, wb),
                                       lambda g, b, r, t: (b, jnp.maximum(2 * t - 1, 0), 0, r, 0, c0 + g))
    else:
        assert dil == ATT_RES and ng * ATT_RES == 2 * HEAD_DIM
        nqb, n_res, n_t = 2, ATT_RES, 1
        view = lambda a: a.reshape(bsz, ng, ATT_RES, ATT_RES, a.shape[-1])
        cur = lambda c0: pl.BlockSpec((1, ng, 1, ATT_RES, wb), lambda g, b, r, t: (b, 0, r, 0, c0 + g))
    has_halo = halo is not None
    tq = qr * nqb
    uv = view(u3)
    in_specs = [cur(q0), cur(k0), cur(v0)]
    args = [uv, uv, uv]
    if has_halo:
        in_specs += [halo(k0), halo(v0)]
        args += [uv, uv]
    in_specs.append(pl.BlockSpec((hb, qr, 2 * qr), lambda g, b, r, t: (g, 0, 0)))
    args.append(bias)
    if prev is not None:
        in_specs += [cur(0), cur(0)]
        args += [view(prev[0]), view(prev[1])]
    grid = (heads // hb, bsz, n_res, n_t)
    s_in, s_out, s_shapes, side_blocks, s_args = _side_cast_specs(side_casts, grid)
    in_specs += s_in
    args += s_args
    if last:
        assert dil == 1
        out_specs = [pl.BlockSpec((1, ATT_GROUP, wb), lambda g, b, r, t: (b, t, g))]
        out_shape = [jax.ShapeDtypeStruct((bsz, seq, mix), BF16)]
    else:
        out_specs = [cur(0), cur(0)]
        out_shape = [jax.eval_shape(view, jax.ShapeDtypeStruct((bsz, seq, mix), dt)) for dt in (BF16, F32)]
    n_main = len(out_specs)
    n_u = hb * nqb
    res = pl.pallas_call(
        functools.partial(_attn_kernel, hb=hb, qr=qr, nqb=nqb, has_halo=has_halo, has_prev=prev is not None,
                          last=last, scale=HEAD_DIM ** -0.5, grid=grid, side_blocks=side_blocks),
        grid=grid,
        in_specs=in_specs,
        out_specs=out_specs + s_out,
        out_shape=out_shape + s_shapes,
        scratch_shapes=[
            pltpu.VMEM((tq + qr, wb), BF16), pltpu.VMEM((tq + qr, wb), BF16),
            pltpu.VMEM((n_u, qr, 2 * qr), F32), pltpu.VMEM((n_u, qr, 2 * qr), BF16),
            pltpu.VMEM((n_u, qr, HEAD_DIM), F32), pltpu.VMEM((n_u, qr, HEAD_DIM), F32),
            pltpu.VMEM((tq, wb), F32), pltpu.VMEM((tq, wb), F32),
        ],
        compiler_params=_cparams(("arbitrary",) * 4),
        name=f"attn_d{dil}",
    )(*args)
    sides = list(res[n_main:])
    if last:
        return res[0], sides
    return (res[0].reshape(bsz, seq, mix), res[1].reshape(bsz, seq, mix)), sides


def dilated_attention(u3, rel_bias, mix, side_casts=()):
    order = sorted(DSW_PATTERNS, key=lambda wd: wd[1] == 1)
    prev = None
    sides = []
    for gi, (window, dil) in enumerate(order):
        assert window // dil == HEAD_DIM
        prev, side = attn_pattern(u3, attn_bias(rel_bias, window, dil), dil, prev, mix=mix,
                                  last=gi == len(order) - 1, side_casts=side_casts[gi] if side_casts else ())
        sides += side
    return prev, sides


def _split3(v):
    hi = v.astype(BF16)
    r1 = v - hi.astype(F32)
    mid = r1.astype(BF16)
    lo = (r1 - mid.astype(F32)).astype(BF16)
    return hi, mid, lo


def _scaled(v, expo, mask):
    return jnp.where(mask, v * jnp.exp(jnp.where(mask, expo, 0.0)), 0.0)


def _hgrn_kernel(*refs, hb, ts, layer, grid, side_blocks):
    n_side = len(side_blocks)
    cq_ref, cf_ref, ci_ref, cg_ref, lb_ref, ng_ref = refs[:6]
    side_in = refs[6:6 + n_side]
    o_ref = refs[6 + n_side]
    side_out = refs[7 + n_side:7 + 2 * n_side]
    st_ref, q_s, kk_s, b_s, a_s = refs[7 + 2 * n_side:]
    _side_cast_body(grid, side_in, side_out, side_blocks)
    s = pl.program_id(2)
    c_len = HGRN_CHUNK
    sub = 16
    sub8 = 8
    nt = (((1,), (1,)), ((), ()))
    tn_ = (((0,), (0,)), ((), ()))

    @pl.when(s == 0)
    def _():
        st_ref[...] = jnp.zeros(st_ref.shape, F32)

    row = lax.broadcasted_iota(jnp.int32, (c_len, c_len), 0)
    col = lax.broadcasted_iota(jnp.int32, (c_len, c_len), 1)
    tri = jnp.where(row >= col, 1.0, 0.0).astype(BF16)
    r64 = lax.broadcasted_iota(jnp.int32, (c_len, HEAD_DIM), 0)
    half = c_len // 2
    mask_b = (((row >= sub) & (row < half) & (col < sub))
              | ((row >= half + sub) & (col >= half) & (col < half + sub)))
    mask_c = ((row >> 4) == (col >> 4)) & ((row & (sub - 1)) >= sub8) & ((col & (sub - 1)) < sub8)
    lrow = lax.broadcasted_iota(jnp.int32, (lb_ref.shape[0], HEAD_DIM), 0)

    def chunk_body(ci, carry):
        r0 = pl.multiple_of(ci * c_len, c_len)
        rows = pl.ds(r0, c_len)
        head_lanes = [slice(h * HEAD_DIM, (h + 1) * HEAD_DIM) for h in range(hb)]
        for h, lanes in enumerate(head_lanes):
            lg = lb_ref[:, lanes]
            pe = jnp.exp(lg - jnp.max(lg, axis=0, keepdims=True))
            lb = (jnp.sum(jnp.where((lrow >= 1) & (lrow <= layer), pe, 0.0), axis=0, keepdims=True)
                  / jnp.sum(pe, axis=0, keepdims=True))
            f = lb + (1.0 - lb) * jax.nn.sigmoid(cf_ref[0, rows, lanes].astype(F32))
            hi, mid, lo = _split3(jnp.log(f))
            q_s[h] = _silu(cq_ref[0, rows, lanes].astype(F32))
            kk_s[h] = 1.0 - f
            b_s[h] = (jnp.dot(tri, hi, preferred_element_type=F32)
                      + jnp.dot(tri, mid, preferred_element_type=F32)
                      + jnp.dot(tri, lo, preferred_element_type=F32))

        for h in range(hb):
            q, kk, b = q_s[h], kk_s[h], b_s[h]
            b_a = b_s[h, half - 1:half, :]
            qa = _scaled(q, b - b_a, r64 >= half)
            ka = _scaled(kk, b_a - b, r64 < half)
            attn = lax.dot_general(qa.astype(BF16), ka.astype(BF16), nt, preferred_element_type=F32)
            b_r = jnp.where(r64 < half, b_s[h, sub - 1:sub, :], b_s[h, half + sub - 1:half + sub, :])
            qsel = ((r64 >= sub) & (r64 < half)) | (r64 >= half + sub)
            ksel = (r64 < sub) | ((r64 >= half) & (r64 < half + sub))
            qbm = _scaled(q, b - b_r, qsel)
            kbm = _scaled(kk, b_r - b, ksel)
            attn_b = lax.dot_general(qbm.astype(BF16), kbm.astype(BF16), nt, preferred_element_type=F32)
            b_c = b_s[h, sub8 - 1:sub8, :]
            for a in range(1, c_len // sub):
                b_c = jnp.where(r64 >= a * sub, b_s[h, a * sub + sub8 - 1:a * sub + sub8, :], b_c)
            upper = (r64 & (sub - 1)) >= sub8
            qcm = _scaled(q, b - b_c, upper)
            kcm = _scaled(kk, b_c - b, jnp.logical_not(upper))
            attn_c = lax.dot_general(qcm.astype(BF16), kcm.astype(BF16), nt, preferred_element_type=F32)
            a_s[h, :, 0:c_len] = attn + jnp.where(mask_b, attn_b, 0.0) + jnp.where(mask_c, attn_c, 0.0)

        for h in range(hb):
            for jb in range(c_len // sub8):
                blk = slice(jb * sub8, (jb + 1) * sub8)
                qt = q_s[h, blk, :]
                bt = b_s[h, blk, :]
                for si in range(sub8):
                    r = jb * sub8 + si
                    e = jnp.exp(bt - b_s[h, r:r + 1, :])
                    a_s[h, blk, r:r + 1] = jnp.sum(qt * kk_s[h, r:r + 1, :] * e, axis=-1, keepdims=True)

        for h, lanes in enumerate(head_lanes):
            q, kk, b = q_s[h], kk_s[h], b_s[h]
            v = ci_ref[0, rows, lanes]
            b_last = b_s[h, c_len - 1:c_len, :]
            st_t = st_ref[h]
            inter = lax.dot_general((q * jnp.exp(b)).astype(BF16), st_t.astype(BF16), nt,
                                    preferred_element_type=F32)
            attn = jnp.where(row >= col, a_s[h, :, 0:c_len], 0.0)
            o = inter + jnp.dot(attn.astype(BF16), v, preferred_element_type=F32)
            kd = (kk * jnp.exp(b_last - b)).astype(BF16)
            st_ref[h] = st_t * jnp.exp(b_last) + lax.dot_general(v, kd, tn_, preferred_element_type=F32)
            ms = jnp.mean(o * o, axis=-1, keepdims=True)
            o = o * lax.rsqrt(ms + LN_EPS) * ng_ref[0:1, lanes]
            o = o * _silu(cg_ref[0, rows, lanes].astype(F32))
            o_ref[0, rows, lanes] = o.astype(o_ref.dtype)
        return carry
    lax.fori_loop(0, ts // c_len, chunk_body, 0)


def hgrn_branch(u3, lb_logits, layer, norm_g, mix, *, side_casts=(), hb=16, ts=256):
    bsz, seq, _ = u3.shape
    heads = mix // HEAD_DIM
    wb = hb * HEAD_DIM
    nb = mix // wb
    grid = (bsz, heads // hb, seq // ts)
    s_in, s_out, s_shapes, side_blocks, s_args = _side_cast_specs(side_casts, grid)

    def col(k):
        return pl.BlockSpec((1, ts, wb), lambda b, g, s: (b, s, k * nb + g))

    vec = pl.BlockSpec((1, wb), lambda b, g, s: (0, g))
    res = pl.pallas_call(
        functools.partial(_hgrn_kernel, hb=hb, ts=ts, layer=layer, grid=grid, side_blocks=side_blocks),
        grid=grid,
        in_specs=[col(0), col(1), col(2), col(3),
                  pl.BlockSpec((lb_logits.shape[0], wb), lambda b, g, s: (0, g)), vec] + s_in,
        out_specs=[pl.BlockSpec((1, ts, wb), lambda b, g, s: (b, s, g))] + s_out,
        out_shape=[jax.ShapeDtypeStruct((bsz, seq, mix), BF16)] + s_shapes,
        scratch_shapes=[pltpu.VMEM((hb, HEAD_DIM, HEAD_DIM), F32)]
        + [pltpu.VMEM((hb, HGRN_CHUNK, HEAD_DIM), F32)] * 4,
        compiler_params=_cparams(("arbitrary",) * 3),
        name="hgrn",
    )(u3, u3, u3, u3, lb_logits.astype(F32), norm_g.reshape(1, mix), *s_args)
    return res[0], list(res[1:])


POOL_HALO = 16


def _pool_kernel(dp_ref, pw_ref, ps_ref, o_ref, x_ref, p_ref, *, ts, rc):
    s = pl.program_id(1)
    ch = o_ref.shape[2]
    grp = ch // len(POOL_WINDOWS)

    @pl.when(s == 0)
    def _():
        x_ref[:, 0:POOL_HALO, :] = jnp.zeros((ch // 128, POOL_HALO, 128), F32)

    @pl.when(s > 0)
    def _():
        x_ref[:, 0:POOL_HALO, :] = x_ref[:, ts:ts + POOL_HALO, :]

    def load_body(i, carry):
        r0 = pl.multiple_of(i * rc, rc)
        xv = dp_ref[0, pl.ds(r0, rc), :].astype(F32)
        for cc in range(ch // 128):
            x_ref[cc, pl.ds(POOL_HALO + r0, rc), :] = xv[:, cc * 128:(cc + 1) * 128]
        return carry
    lax.fori_loop(0, ts // rc, load_body, 0)

    def pool_body(i, carry):
        r0 = pl.multiple_of(i * rc, rc)
        pos = s * ts + r0 + lax.broadcasted_iota(jnp.int32, (rc, 1), 0)
        for cc in range(ch // 128):
            w = POOL_WINDOWS[(cc * 128) // grp]
            cur = x_ref[cc, pl.ds(POOL_HALO + r0, rc), :]
            tot = cur
            for j in range(1, w):
                tot = tot + x_ref[cc, pl.ds(POOL_HALO + r0 - j, rc), :]
            cnt = jnp.minimum(pos + 1, w).astype(F32)
            p_ref[pl.ds(r0, rc), cc * 128:(cc + 1) * 128] = (tot / cnt - cur).astype(BF16)
        return carry
    lax.fori_loop(0, ts // rc, pool_body, 0)

    for gi in range(len(POOL_WINDOWS)):
        lanes = slice(gi * grp, (gi + 1) * grp)
        y = jnp.dot(p_ref[:, lanes], pw_ref[gi], preferred_element_type=F32)
        o_ref[0, :, lanes] = (y * ps_ref[0:1, lanes]).astype(o_ref.dtype)


def pool_branch(u3, pool_w, pool_scale, mix, *, ts=512, rc=32):
    bsz, seq, _ = u3.shape
    ng, grp, _ = pool_w.shape
    return pl.pallas_call(
        functools.partial(_pool_kernel, ts=ts, rc=rc),
        grid=(bsz, seq // ts),
        in_specs=[
            pl.BlockSpec((1, ts, mix), lambda b, s: (b, s, 4)),
            pl.BlockSpec((ng, grp, grp), lambda b, s: (0, 0, 0)),
            pl.BlockSpec((1, mix), lambda b, s: (0, 0)),
        ],
        out_specs=pl.BlockSpec((1, ts, mix), lambda b, s: (b, s, 0)),
        out_shape=jax.ShapeDtypeStruct((bsz, seq, mix), BF16),
        scratch_shapes=[pltpu.VMEM((mix // 128, ts + POOL_HALO, 128), F32), pltpu.VMEM((ts, mix), BF16)],
        compiler_params=_cparams(("arbitrary", "arbitrary")),
        name="pool_branch",
    )(u3, pool_w.astype(BF16), pool_scale.reshape(1, mix))


def kernel(x, c, ada_w, ada_b, w_in, w_out, ln_g, ln_b, mlp_w1, mlp_w2, conv_w, conv_b, conv_ln_g, conv_ln_b,
           rel_bias, hgrn_lb_logits, hgrn_norm_g, pool_w, pool_scale):
    bsz, seq, d = x.shape
    depth = ada_w.shape[0]
    mix = d // 2
    alpha = (2.0 * depth) ** 0.25
    m = bsz * seq

    mod = adaln_mod(c, ada_w, ada_b)
    x2 = x.reshape(m, d)
    h = modulate(x2, mod[0], seq)
    w_in_b = cast_layer_bf16(w_in, 0)
    for l in range(depth):
        u = mm_in(h, w_in_b, perm_from=2 * mix if l % 2 == 0 else None)
        u3 = u.reshape(bsz, seq, 5 * mix)
        if l % 2 == 0:
            e = l // 2
            conv_sides = [(mlp_w2, l, 256)] + ([(w_in, l + 1, 64)] if l + 1 < depth else [])
            za, sides = conv_branch(u3, conv_w[e], conv_b[e], conv_ln_g[e], conv_ln_b[e], mix,
                                    side_casts=conv_sides)
            w2_b = sides[0]
            if l + 1 < depth:
                w_in_b = sides[1]
            zb, (w_out_b, w1_b) = dilated_attention(
                u3, rel_bias, mix, side_casts=[[], [(w_out, l, 64)], [(mlp_w1, l, 64)]])
        else:
            if l + 1 < depth:
                w_in_b = cast_layer_bf16(w_in, l + 1)
            o = l // 2
            za, (w_out_b, w1_b, w2_b) = hgrn_branch(
                u3, hgrn_lb_logits, l, hgrn_norm_g[o], mix,
                side_casts=[(w_out, l, 64), (mlp_w1, l, 64), (mlp_w2, l, 256)])
            zb = pool_branch(u3, pool_w[o], pool_scale[o], mix)
        x2, h2 = mm_out_ln(za.reshape(m, mix), zb.reshape(m, mix), w_out_b, x2, mod[l],
                           ln_g[l, 0], ln_b[l, 0], seq, alpha)
        x2, h = mlp_ln(h2, x2, mod[l], mod[l + 1] if l + 1 < depth else None,
                       w1_b, w2_b, ln_g[l, 1], ln_b[l, 1], seq, alpha)
    return x2.reshape(bsz, seq, d)
```

```python
import functools
import math

import jax
import jax.numpy as jnp
from jax import lax
from jax.experimental import pallas as pl
from jax.experimental.pallas import tpu as pltpu

F32 = jnp.float32
BF16 = jnp.bfloat16

HEAD_DIM = 128
CONV_K = 31
DSW_PATTERNS = ((128, 1), (512, 4), (2048, 16))
NUM_BUCKETS = 32
MAX_DISTANCE = 2048
HGRN_CHUNK = 64
POOL_WINDOWS = (2, 4, 8, 16)
LN_EPS = 1e-5
NEG = -1e30

V7X_VMEM_BYTES = 64 * 1024 * 1024
VMEM_LIMIT = V7X_VMEM_BYTES - 2 * 1024 * 1024


def _cparams(sem):
    return pltpu.CompilerParams(dimension_semantics=sem, vmem_limit_bytes=VMEM_LIMIT)


def _silu(v):
    return v * jax.nn.sigmoid(v)


def _mod_kernel(c_ref, w_ref, b_ref, o_ref):
    cs = _silu(c_ref[...]).astype(BF16)
    w = w_ref[0].astype(BF16)
    o_ref[0] = jnp.dot(cs, w, preferred_element_type=F32) + b_ref[0]


def adaln_mod(c, ada_w, ada_b, *, tn=512):
    nl, d, n6 = ada_w.shape
    bsz = c.shape[0]
    rows = 8
    c8 = jnp.zeros((rows, d), F32).at[:bsz].set(c)
    out = pl.pallas_call(
        _mod_kernel,
        grid=(nl, n6 // tn),
        in_specs=[
            pl.BlockSpec((rows, d), lambda l, j: (0, 0)),
            pl.BlockSpec((1, d, tn), lambda l, j: (l, 0, j)),
            pl.BlockSpec((1, 1, tn), lambda l, j: (l, 0, j)),
        ],
        out_specs=pl.BlockSpec((1, rows, tn), lambda l, j: (l, 0, j)),
        out_shape=jax.ShapeDtypeStruct((nl, rows, n6), F32),
        compiler_params=_cparams(("arbitrary", "arbitrary")),
        name="adaln_mod",
    )(c8, ada_w, ada_b.reshape(nl, 1, n6))
    return out[:, :bsz].reshape(nl, bsz, 6, d)


LN_ROWS = 16
LN_COLS = 512


def _ln_chunks(chunks, d, g_ref, b_ref):
    col_slices = [slice(c * LN_COLS, (c + 1) * LN_COLS) for c in range(d // LN_COLS)]
    mus = []
    for load, _ in chunks:
        tot = load(col_slices[0])
        for cols in col_slices[1:]:
            tot = tot + load(cols)
        mus.append(jnp.sum(tot, axis=-1, keepdims=True) * (1.0 / d))
    rstds = []
    for (load, _), mu in zip(chunks, mus):
        sq = None
        for cols in col_slices:
            dv = load(cols) - mu
            sq = dv * dv if sq is None else sq + dv * dv
        rstds.append(lax.rsqrt(jnp.sum(sq, axis=-1, keepdims=True) * (1.0 / d) + LN_EPS))
    for (load, emit), mu, rstd in zip(chunks, mus, rstds):
        for cols in col_slices:
            emit(cols, (load(cols) - mu) * rstd * g_ref[:, cols] + b_ref[:, cols])


def _ln_chunk(load, d, g_ref, b_ref, emit):
    _ln_chunks([(load, emit)], d, g_ref, b_ref)


def _ln_mod_phase(step, rows_per_step, src_ref, xr_ref, alpha, gate_ref, g_ref, b_ref, mod_ref, sh_row, sc_row,
                  xo_ref, ho_ref):
    d = xo_ref.shape[1]
    chunks = []
    for k in range(rows_per_step // LN_ROWS):
        src_rows = pl.ds(pl.multiple_of(step * rows_per_step + k * LN_ROWS, LN_ROWS), LN_ROWS)
        out_rows = slice(k * LN_ROWS, (k + 1) * LN_ROWS)
        if xr_ref is not None:
            for c in range(d // LN_COLS):
                cols = slice(c * LN_COLS, (c + 1) * LN_COLS)
                xo_ref[out_rows, cols] = alpha * xr_ref[out_rows, cols] + gate_ref[0, 5:6, cols] * src_ref[src_rows, cols]
            load = lambda cols, out_rows=out_rows: xo_ref[out_rows, cols]
        else:
            load = lambda cols, src_rows=src_rows: src_ref[src_rows, cols]

        def emit(cols, y, out_rows=out_rows):
            xo_ref[out_rows, cols] = y
            if ho_ref is not None:
                h = y * (1.0 + mod_ref[0, sc_row:sc_row + 1, cols]) + mod_ref[0, sh_row:sh_row + 1, cols]
                ho_ref[out_rows, cols] = h.astype(BF16)
        chunks.append((load, emit))
    _ln_chunks(chunks, d, g_ref, b_ref)


def _cast_kernel(w_ref, o_ref):
    o_ref[...] = w_ref[0].astype(o_ref.dtype)


def cast_layer_bf16(w_stack, layer, *, tr=512, tc=2048):
    _, rows, cols = w_stack.shape
    tc = min(tc, cols)
    return pl.pallas_call(
        _cast_kernel,
        grid=(rows // tr, cols // tc),
        in_specs=[pl.BlockSpec((1, tr, tc), lambda i, j: (layer, i, j))],
        out_specs=pl.BlockSpec((tr, tc), lambda i, j: (i, j)),
        out_shape=jax.ShapeDtypeStruct((rows, cols), BF16),
        compiler_params=_cparams(("arbitrary", "arbitrary")),
        name="cast_bf16",
    )(w_stack)


def _modulate_kernel(x_ref, mod_ref, o_ref):
    o_ref[...] = (x_ref[...] * (1.0 + mod_ref[0, 1:2, :]) + mod_ref[0, 0:1, :]).astype(o_ref.dtype)


def modulate(x2, mod_l, seq, *, tm=256):
    m, d = x2.shape
    return pl.pallas_call(
        _modulate_kernel,
        grid=(m // tm,),
        in_specs=[
            pl.BlockSpec((tm, d), lambda i: (i, 0)),
            pl.BlockSpec((1, 6, d), lambda i: ((i * tm) // seq, 0, 0)),
        ],
        out_specs=pl.BlockSpec((tm, d), lambda i: (i, 0)),
        out_shape=jax.ShapeDtypeStruct((m, d), BF16),
        compiler_params=_cparams(("arbitrary",)),
        name="modulate",
    )(x2, mod_l)


def _side_cast_specs(side_casts, grid):
    def linear(*ids):
        step = ids[0]
        for size, idx in zip(grid[1:], ids[1:]):
            step = step * size + idx
        return step

    in_specs, out_specs, out_shapes, n_blocks = [], [], [], []
    for arr, layer, br in side_casts:
        _, rows, cols = arr.shape
        nb = rows // br
        assert rows % br == 0 and nb <= math.prod(grid)
        blk = lambda *ids, nb=nb: jnp.minimum(linear(*ids), nb - 1)
        in_specs.append(pl.BlockSpec((1, br, cols), lambda *ids, layer=layer, blk=blk: (layer, blk(*ids), 0)))
        out_specs.append(pl.BlockSpec((br, cols), lambda *ids, blk=blk: (blk(*ids), 0)))
        out_shapes.append(jax.ShapeDtypeStruct((rows, cols), BF16))
        n_blocks.append(nb)
    return in_specs, out_specs, out_shapes, tuple(n_blocks), [a for a, _, _ in side_casts]


def _side_cast_body(grid, side_in, side_out, n_blocks):
    step = pl.program_id(0)
    for ax in range(1, len(grid)):
        step = step * grid[ax] + pl.program_id(ax)
    for src, dst, nb in zip(side_in, side_out, n_blocks):
        @pl.when(step < nb)
        def _(src=src, dst=dst):
            dst[...] = src[0].astype(dst.dtype)


def _mm_in_kernel(h_ref, w_ref, o_ref, *scratch, perm_tile):
    if perm_tile is None:
        o_ref[...] = jnp.dot(h_ref[...], w_ref[...], preferred_element_type=F32).astype(o_ref.dtype)
        return
    hp_ref, = scratch
    j = pl.program_id(1)

    @pl.when(j == 0)
    def _():
        perm = _group_perm()
        for g in range(h_ref.shape[0] // ATT_GROUP):
            rows = slice(g * ATT_GROUP, (g + 1) * ATT_GROUP)
            hp_ref[rows, :] = jnp.dot(perm, h_ref[rows, :], preferred_element_type=F32).astype(BF16)

    @pl.when(j < perm_tile)
    def _():
        o_ref[...] = jnp.dot(h_ref[...], w_ref[...], preferred_element_type=F32).astype(o_ref.dtype)

    @pl.when(j >= perm_tile)
    def _():
        o_ref[...] = jnp.dot(hp_ref[...], w_ref[...], preferred_element_type=F32).astype(o_ref.dtype)


def mm_in(h, w, *, perm_from=None, tm=1024, tn=1024):
    m, d = h.shape
    n = w.shape[1]
    perm_tile = None
    scratch = []
    if perm_from is not None:
        assert perm_from % tn == 0 and tm % ATT_GROUP == 0
        perm_tile = perm_from // tn
        scratch = [pltpu.VMEM((tm, d), BF16)]
    return pl.pallas_call(
        functools.partial(_mm_in_kernel, perm_tile=perm_tile),
        grid=(m // tm, n // tn),
        in_specs=[
            pl.BlockSpec((tm, d), lambda i, j: (i, 0)),
            pl.BlockSpec((d, tn), lambda i, j: (0, j)),
        ],
        out_specs=pl.BlockSpec((tm, tn), lambda i, j: (i, j)),
        out_shape=jax.ShapeDtypeStruct((m, n), BF16),
        scratch_shapes=scratch,
        compiler_params=_cparams(("arbitrary", "arbitrary")),
        name="mm_in",
    )(h, w)


def _mm_out_kernel(a_ref, b_ref, w_ref, x_ref, mod_ref, modp_ref, g_ref, beta_ref, xo_ref, ho_ref, acc_ref, ln_ref,
                   *, n_t, n_n, tn, k1, alpha):
    i = pl.program_id(0)
    n = pl.program_id(1)
    rows_per_step = acc_ref.shape[0] // n_n

    @pl.when((i == 0) & (n == 0))
    def _():
        ln_ref[...] = jnp.zeros(ln_ref.shape, F32)

    def ln_phase():
        _ln_mod_phase(n, rows_per_step, ln_ref, None, alpha, None, g_ref, beta_ref, modp_ref, 3, 4, xo_ref, ho_ref)

    @pl.when(i < n_t)
    def _():
        y = jnp.dot(a_ref[...], w_ref[0:k1, :], preferred_element_type=F32)
        y = y + jnp.dot(b_ref[...], w_ref[k1:, :], preferred_element_type=F32)
        cols = pl.ds(pl.multiple_of(n * tn, tn), tn)
        acc_ref[:, cols] = alpha * x_ref[...] + mod_ref[0, 2:3, cols] * y
        ln_phase()

    @pl.when(i == n_t)
    def _():
        ln_phase()

    @pl.when((i < n_t) & (n == n_n - 1))
    def _():
        ln_ref[...] = acc_ref[...]


def mm_out_ln(za, zb, w, x2, mod_l, ln_g, ln_b, seq, alpha, *, tm=512, tn=1024):
    m, k1 = za.shape
    d = w.shape[1]
    n_n = d // tn
    n_t = m // tm
    rl = tm // n_n
    assert rl % LN_ROWS == 0
    cur = lambda i: jnp.minimum(i, n_t - 1)
    prv = lambda i: jnp.maximum(i - 1, 0)
    col = lambda i, n: jnp.where(i < n_t, n, n_n - 1)
    out_row = pl.BlockSpec((rl, d), lambda i, n: (jnp.where(i == 0, 0, (i - 1) * n_n + n), 0))
    return pl.pallas_call(
        functools.partial(_mm_out_kernel, n_t=n_t, n_n=n_n, tn=tn, k1=k1, alpha=alpha),
        grid=(n_t + 1, n_n),
        in_specs=[
            pl.BlockSpec((tm, k1), lambda i, n: (cur(i), 0)),
            pl.BlockSpec((tm, zb.shape[1]), lambda i, n: (cur(i), 0)),
            pl.BlockSpec((w.shape[0], tn), lambda i, n: (0, col(i, n))),
            pl.BlockSpec((tm, tn), lambda i, n: (cur(i), col(i, n))),
            pl.BlockSpec((1, 6, d), lambda i, n: ((cur(i) * tm) // seq, 0, 0)),
            pl.BlockSpec((1, 6, d), lambda i, n: ((prv(i) * tm) // seq, 0, 0)),
            pl.BlockSpec((1, d), lambda i, n: (0, 0)),
            pl.BlockSpec((1, d), lambda i, n: (0, 0)),
        ],
        out_specs=[out_row, out_row],
        out_shape=[jax.ShapeDtypeStruct((m, d), F32), jax.ShapeDtypeStruct((m, d), BF16)],
        scratch_shapes=[pltpu.VMEM((tm, d), F32), pltpu.VMEM((tm, d), F32)],
        compiler_params=_cparams(("arbitrary", "arbitrary")),
        name="mm_out_ln",
    )(za, zb, w, x2, mod_l, mod_l, ln_g.reshape(1, d), ln_b.reshape(1, d))


def _mlp_kernel(h_ref, xr_ref, modp_ref, modn_ref, w1_ref, w2_ref, g_ref, beta_ref, *rest,
                n_t, n_f, tn2, alpha, emit_h):
    if emit_h:
        xo_ref, ho_ref, acc_ref, ln_ref = rest
    else:
        xo_ref, acc_ref, ln_ref = rest
        ho_ref = None
    i = pl.program_id(0)
    f = pl.program_id(1)
    d = acc_ref.shape[1]
    rows_per_step = acc_ref.shape[0] // n_f

    @pl.when((i == 0) & (f == 0))
    def _():
        ln_ref[...] = jnp.zeros(ln_ref.shape, F32)

    @pl.when((i < n_t) & (f == 0))
    def _():
        acc_ref[...] = jnp.zeros(acc_ref.shape, F32)

    def ln_phase():
        _ln_mod_phase(f, rows_per_step, ln_ref, xr_ref, alpha, modp_ref, g_ref, beta_ref, modn_ref, 0, 1,
                      xo_ref, ho_ref)

    @pl.when(i < n_t)
    def _():
        t = jnp.dot(h_ref[...], w1_ref[...], preferred_element_type=F32)
        t = jnp.maximum(t, 0.0)
        t = (t * t).astype(BF16)
        for nb in range(d // tn2):
            cols = slice(nb * tn2, (nb + 1) * tn2)
            acc_ref[:, cols] += jnp.dot(t, w2_ref[:, cols], preferred_element_type=F32)
        ln_phase()

    @pl.when(i == n_t)
    def _():
        ln_phase()

    @pl.when((i < n_t) & (f == n_f - 1))
    def _():
        ln_ref[...] = acc_ref[...]


def mlp_ln(h, x2, mod_l, mod_next, w1, w2, ln_g, ln_b, seq, alpha, *, tm=512, tf=1024, tn2=512):
    m, d = x2.shape
    dff = w1.shape[1]
    n_f = dff // tf
    n_t = m // tm
    rl = tm // n_f
    assert rl % LN_ROWS == 0
    emit_h = mod_next is not None
    cur = lambda i: jnp.minimum(i, n_t - 1)
    prv = lambda i: jnp.maximum(i - 1, 0)
    chunk = lambda i, f: jnp.where(i < n_t, f, n_f - 1)
    out_row = pl.BlockSpec((rl, d), lambda i, f: (jnp.where(i == 0, 0, (i - 1) * n_f + f), 0))
    modspec = pl.BlockSpec((1, 6, d), lambda i, f: ((prv(i) * tm) // seq, 0, 0))
    f32_out = jax.ShapeDtypeStruct((m, d), F32)
    res = pl.pallas_call(
        functools.partial(_mlp_kernel, n_t=n_t, n_f=n_f, tn2=tn2, alpha=alpha, emit_h=emit_h),
        grid=(n_t + 1, n_f),
        in_specs=[
            pl.BlockSpec((tm, d), lambda i, f: (cur(i), 0), pipeline_mode=pl.Buffered(1)),
            pl.BlockSpec((rl, d), lambda i, f: (prv(i) * n_f + f, 0)),
            modspec,
            modspec,
            pl.BlockSpec((d, tf), lambda i, f: (0, chunk(i, f))),
            pl.BlockSpec((tf, d), lambda i, f: (chunk(i, f), 0)),
            pl.BlockSpec((1, d), lambda i, f: (0, 0)),
            pl.BlockSpec((1, d), lambda i, f: (0, 0)),
        ],
        out_specs=[out_row, out_row] if emit_h else out_row,
        out_shape=[f32_out, jax.ShapeDtypeStruct((m, d), BF16)] if emit_h else f32_out,
        scratch_shapes=[pltpu.VMEM((tm, d), F32), pltpu.VMEM((tm, d), F32)],
        compiler_params=_cparams(("arbitrary", "arbitrary")),
        name="mlp_ln",
    )(h, x2, mod_l, mod_next if emit_h else mod_l, w1, w2, ln_g.reshape(1, d), ln_b.reshape(1, d))
    return res if emit_h else (res, None)


CONV_HALO = 32


def _conv_kernel(*refs, ts, rc, grid, side_blocks):
    n_side = len(side_blocks)
    av_ref, ag_ref, w_ref, cb_ref, g_ref, beta_ref = refs[:6]
    side_in = refs[6:6 + n_side]
    o_ref = refs[6 + n_side]
    side_out = refs[7 + n_side:7 + 2 * n_side]
    glu_ref, y_ref = refs[7 + 2 * n_side:]
    _side_cast_body(grid, side_in, side_out, side_blocks)
    s = pl.program_id(1)
    ch = o_ref.shape[2]

    @pl.when(s == 0)
    def _():
        glu_ref[:, 0:CONV_HALO, :] = jnp.zeros((ch // 128, CONV_HALO, 128), F32)

    @pl.when(s > 0)
    def _():
        glu_ref[:, 0:CONV_HALO, :] = glu_ref[:, ts:ts + CONV_HALO, :]

    def glu_body(i, carry):
        r0 = pl.multiple_of(i * rc, rc)
        a = av_ref[0, pl.ds(r0, rc), :].astype(F32)
        gt = ag_ref[0, pl.ds(r0, rc), :].astype(F32)
        glu = a * jax.nn.sigmoid(gt)
        for cc in range(ch // 128):
            glu_ref[cc, pl.ds(CONV_HALO + r0, rc), :] = glu[:, cc * 128:(cc + 1) * 128]
        return carry
    lax.fori_loop(0, ts // rc, glu_body, 0)

    off = CONV_HALO - (CONV_K - 1)

    def conv_body(i, carry):
        r0 = pl.multiple_of(i * rc, rc)
        for cc in range(ch // 128):
            lanes = slice(cc * 128, (cc + 1) * 128)
            acc = jnp.zeros((rc, 128), F32)
            for j in range(CONV_K):
                acc = acc + w_ref[j:j + 1, lanes] * glu_ref[cc, pl.ds(r0 + off + j, rc), :]
            y_ref[pl.ds(r0, rc), lanes] = acc + cb_ref[0:1, lanes]
        return carry
    lax.fori_loop(0, ts // rc, conv_body, 0)

    ln_group = 4

    def ln_body(i, carry):
        chunks = []
        for k in range(ln_group):
            rows = pl.ds(pl.multiple_of((i * ln_group + k) * LN_ROWS, LN_ROWS), LN_ROWS)

            def emit(cols, z, rows=rows):
                o_ref[0, rows, cols] = _silu(z).astype(o_ref.dtype)
            chunks.append((lambda cols, rows=rows: y_ref[rows, cols], emit))
        _ln_chunks(chunks, ch, g_ref, beta_ref)
        return carry
    lax.fori_loop(0, ts // (LN_ROWS * ln_group), ln_body, 0)


def conv_branch(u3, conv_w, conv_b, ln_g, ln_b, mix, *, side_casts=(), ts=256, rc=32):
    bsz, seq, _ = u3.shape
    grid = (bsz, seq // ts)
    s_in, s_out, s_shapes, side_blocks, s_args = _side_cast_specs(side_casts, grid)
    res = pl.pallas_call(
        functools.partial(_conv_kernel, ts=ts, rc=rc, grid=grid, side_blocks=side_blocks),
        grid=grid,
        in_specs=[
            pl.BlockSpec((1, ts, mix), lambda b, s: (b, s, 0)),
            pl.BlockSpec((1, ts, mix), lambda b, s: (b, s, 1)),
            pl.BlockSpec((CONV_K, mix), lambda b, s: (0, 0)),
            pl.BlockSpec((1, mix), lambda b, s: (0, 0)),
            pl.BlockSpec((1, mix), lambda b, s: (0, 0)),
            pl.BlockSpec((1, mix), lambda b, s: (0, 0)),
        ] + s_in,
        out_specs=[pl.BlockSpec((1, ts, mix), lambda b, s: (b, s, 0))] + s_out,
        out_shape=[jax.ShapeDtypeStruct((bsz, seq, mix), BF16)] + s_shapes,
        scratch_shapes=[pltpu.VMEM((mix // 128, ts + CONV_HALO, 128), F32), pltpu.VMEM((ts, mix), F32)],
        compiler_params=_cparams(("arbitrary", "arbitrary")),
        name="conv_branch",
    )(u3, u3, conv_w, conv_b.reshape(1, mix), ln_g.reshape(1, mix), ln_b.reshape(1, mix), *s_args)
    return res[0], list(res[1:])


def _t5_bucket(dist):
    max_exact = NUM_BUCKETS // 2
    nf = jnp.maximum(dist, 1).astype(F32)
    large = max_exact + (jnp.log(nf / max_exact) / math.log(MAX_DISTANCE / max_exact)
                         * (NUM_BUCKETS - max_exact)).astype(jnp.int32)
    large = jnp.minimum(large, NUM_BUCKETS - 1)
    return jnp.where(dist < max_exact, dist, large)


ATT_GROUP = 256
ATT_RES = 16


def _natural_index(dil):
    if dil == 1:
        i = jnp.arange(ATT_GROUP)
        return ATT_GROUP, ATT_RES * (i % ATT_RES) + i // ATT_RES
    if dil == 4:
        i = jnp.arange(HEAD_DIM)
        return HEAD_DIM, (i // 64) * 64 + 4 * (i % 16) + (i % 64) // 16
    assert dil == ATT_RES
    return HEAD_DIM, jnp.arange(HEAD_DIM)


def _bucket_tile(window, dil):
    steps = window // dil
    qr, nat = _natural_index(dil)
    qn = nat[:, None] + qr
    kn = jnp.concatenate([nat, nat + qr])[None, :]
    step = qn - kn
    bucket = _t5_bucket(jnp.clip(step, 0, steps) * dil)
    valid = (step >= 0) & (step <= steps)
    return jnp.where(valid, bucket, -1).astype(jnp.int32)


def _bias_kernel(rb_ref, idx_ref, o_ref, *, heads):
    idx = idx_ref[...]
    for h in range(heads):
        acc = jnp.full(idx.shape, NEG, F32)
        for bk in range(NUM_BUCKETS):
            acc = jnp.where(idx == bk, rb_ref[bk, h], acc)
        o_ref[h] = acc


def attn_bias(rel_bias, window, dil):
    heads = rel_bias.shape[1]
    idx = _bucket_tile(window, dil)
    qr, qr2 = idx.shape
    return pl.pallas_call(
        functools.partial(_bias_kernel, heads=heads),
        grid=(1,),
        in_specs=[
            pl.BlockSpec(memory_space=pltpu.SMEM),
            pl.BlockSpec((qr, qr2), lambda i: (0, 0)),
        ],
        out_specs=pl.BlockSpec((heads, qr, qr2), lambda i: (0, 0, 0)),
        out_shape=jax.ShapeDtypeStruct((heads, qr, qr2), F32),
        compiler_params=_cparams(("arbitrary",)),
        name=f"attn_bias_d{dil}",
    )(rel_bias.astype(F32), idx)


def _group_perm():
    shift = ATT_RES.bit_length() - 1
    row = lax.broadcasted_iota(jnp.int32, (ATT_GROUP, ATT_GROUP), 0)
    col = lax.broadcasted_iota(jnp.int32, (ATT_GROUP, ATT_GROUP), 1)
    src = ((row & (ATT_RES - 1)) << shift) | (row >> shift)
    return jnp.where(col == src, 1.0, 0.0).astype(BF16)


def _attn_kernel(*refs, hb, qr, nqb, has_halo, has_prev, last, scale, grid, side_blocks):
    it = iter(refs)
    q_ref, k_ref, v_ref = next(it), next(it), next(it)
    kp_ref, vp_ref = (next(it), next(it)) if has_halo else (None, None)
    bias_ref = next(it)
    op_ref, lp_ref = (next(it), next(it)) if has_prev else (None, None)
    side_in = [next(it) for _ in side_blocks]
    o_ref = next(it)
    l_ref = None if last else next(it)
    side_out = [next(it) for _ in side_blocks]
    kbuf, vbuf, s_scr, p_scr, m_scr, d_scr, obuf, lbuf = (next(it) for _ in range(8))
    _side_cast_body(grid, side_in, side_out, side_blocks)
    t = pl.program_id(3)
    tq = qr * nqb
    wb = hb * HEAD_DIM
    nt = (((1,), (1,)), ((), ()))
    units = [(h, j) for h in range(hb) for j in range(nqb)]

    q = q_ref[...].reshape(tq, wb)
    kbuf[qr:, :] = k_ref[...].reshape(tq, wb)
    vbuf[qr:, :] = v_ref[...].reshape(tq, wb)
    if has_halo:
        kbuf[0:qr, :] = kp_ref[...].reshape(qr, wb)
        vbuf[0:qr, :] = vp_ref[...].reshape(qr, wb)
        first = t == 0
    else:
        kbuf[0:qr, :] = jnp.zeros((qr, wb), BF16)
        vbuf[0:qr, :] = jnp.zeros((qr, wb), BF16)
        first = t >= 0
    prev_cols = lax.broadcasted_iota(jnp.int32, (qr, 2 * qr), 1) < qr

    for u, (h, j) in enumerate(units):
        lanes = slice(h * HEAD_DIM, (h + 1) * HEAD_DIM)
        s = lax.dot_general(q[j * qr:(j + 1) * qr, lanes], kbuf[j * qr:(j + 2) * qr, lanes], nt,
                            preferred_element_type=F32)
        s = s * scale + bias_ref[h]
        if j == 0:
            s = jnp.where(prev_cols & first, NEG, s)
        s_scr[u] = s

    for u in range(len(units)):
        s = s_scr[u]
        m = jnp.max(s, axis=-1, keepdims=True)
        p = jnp.exp(s - m)
        m_scr[u] = jnp.broadcast_to(m, (qr, HEAD_DIM))
        d_scr[u] = jnp.broadcast_to(jnp.sum(p, axis=-1, keepdims=True), (qr, HEAD_DIM))
        p_scr[u] = p.astype(BF16)

    if has_prev:
        o_prev = op_ref[...].reshape(tq, wb).astype(F32)
        l_prev = lp_ref[...].reshape(tq, wb)
    for u, (h, j) in enumerate(units):
        lanes = slice(h * HEAD_DIM, (h + 1) * HEAD_DIM)
        rows = slice(j * qr, (j + 1) * qr)
        acc = jnp.dot(p_scr[u], vbuf[j * qr:(j + 2) * qr, lanes], preferred_element_type=F32)
        den = d_scr[u]
        o = acc / den
        lse = m_scr[u] + jnp.log(den)
        if has_prev:
            lse0 = l_prev[rows, lanes]
            mx = jnp.maximum(lse0, lse)
            w0 = jnp.exp(lse0 - mx)
            w1 = jnp.exp(lse - mx)
            tot = w0 + w1
            o = (w0 * o_prev[rows, lanes] + w1 * o) / tot
            lse = mx + jnp.log(tot)
        obuf[rows, lanes] = o
        if not last:
            lbuf[rows, lanes] = lse

    if last:
        res = jnp.dot(_group_perm(), obuf[...].astype(BF16), preferred_element_type=F32)
        o_ref[...] = res.astype(o_ref.dtype).reshape(o_ref.shape)
    else:
        o_ref[...] = obuf[...].astype(o_ref.dtype).reshape(o_ref.shape)
        l_ref[...] = lbuf[...].reshape(l_ref.shape)


def attn_pattern(u3, bias, dil, prev, *, mix, last, side_casts=(), hb=16):
    bsz, seq, _ = u3.shape
    heads = mix // HEAD_DIM
    ng = seq // ATT_GROUP
    wb = hb * HEAD_DIM
    q0, k0, v0 = (2 * mix) // wb, (3 * mix) // wb, (4 * mix) // wb
    qr = bias.shape[1]
    halo = None
    if dil == 1:
        nqb, n_res, n_t = 1, 1, ng
        view = lambda a: a.reshape(bsz, ng, ATT_GROUP, a.shape[-1])
        cur = lambda c0: pl.BlockSpec((1, 1, ATT_GROUP, wb), lambda g, b, r, t: (b, t, 0, c0 + g))
        halo = lambda c0: pl.BlockSpec((1, 1, ATT_GROUP, wb),
                                       lambda g, b, r, t: (b, jnp.maximum(t - 1, 0), 0, c0 + g))
    elif dil == 4:
        nqb, n_res, gt = 2, 4, 4
        n_t = ng // gt
        view = lambda a: a.reshape(bsz, ng, 4, 4, ATT_RES, a.shape[-1])
        cur = lambda c0: pl.BlockSpec((1, gt, 4, 1, ATT_RES, wb), lambda g, b, r, t: (b, t, 0, r, 0, c0 + g))
        halo = lambda c0: pl.BlockSpec((1, gt // 2, 4, 1, ATT_RES, wb),
                                       lambda g, b, r, t: (b, jnp.maximum(2 * t - 1, 0), 0, r, 0, c0 + g))
    else:
        assert dil == ATT_RES and ng * ATT_RES == 2 * HEAD_DIM
        nqb, n_res, n_t = 2, ATT_RES, 1
        view = lambda a: a.reshape(bsz, ng, ATT_RES, ATT_RES, a.shape[-1])
        cur = lambda c0: pl.BlockSpec((1, ng, 1, ATT_RES, wb), lambda g, b, r, t: (b, 0, r, 0, c0 + g))
    has_halo = halo is not None
    tq = qr * nqb
    uv = view(u3)
    in_specs = [cur(q0), cur(k0), cur(v0)]
    args = [uv, uv, uv]
    if has_halo:
        in_specs += [halo(k0), halo(v0)]
        args += [uv, uv]
    in_specs.append(pl.BlockSpec((hb, qr, 2 * qr), lambda g, b, r, t: (g, 0, 0)))
    args.append(bias)
    if prev is not None:
        in_specs += [cur(0), cur(0)]
        args += [view(prev[0]), view(prev[1])]
    grid = (heads // hb, bsz, n_res, n_t)
    s_in, s_out, s_shapes, side_blocks, s_args = _side_cast_specs(side_casts, grid)
    in_specs += s_in
    args += s_args
    if last:
        assert dil == 1
        out_specs = [pl.BlockSpec((1, ATT_GROUP, wb), lambda g, b, r, t: (b, t, g))]
        out_shape = [jax.ShapeDtypeStruct((bsz, seq, mix), BF16)]
    else:
        out_specs = [cur(0), cur(0)]
        out_shape = [jax.eval_shape(view, jax.ShapeDtypeStruct((bsz, seq, mix), dt)) for dt in (BF16, F32)]
    n_main = len(out_specs)
    n_u = hb * nqb
    res = pl.pallas_call(
        functools.partial(_attn_kernel, hb=hb, qr=qr, nqb=nqb, has_halo=has_halo, has_prev=prev is not None,
                          last=last, scale=HEAD_DIM ** -0.5, grid=grid, side_blocks=side_blocks),
        grid=grid,
        in_specs=in_specs,
        out_specs=out_specs + s_out,
        out_shape=out_shape + s_shapes,
        scratch_shapes=[
            pltpu.VMEM((tq + qr, wb), BF16), pltpu.VMEM((tq + qr, wb), BF16),
            pltpu.VMEM((n_u, qr, 2 * qr), F32), pltpu.VMEM((n_u, qr, 2 * qr), BF16),
            pltpu.VMEM((n_u, qr, HEAD_DIM), F32), pltpu.VMEM((n_u, qr, HEAD_DIM), F32),
            pltpu.VMEM((tq, wb), F32), pltpu.VMEM((tq, wb), F32),
        ],
        compiler_params=_cparams(("arbitrary",) * 4),
        name=f"attn_d{dil}",
    )(*args)
    sides = list(res[n_main:])
    if last:
        return res[0], sides
    return (res[0].reshape(bsz, seq, mix), res[1].reshape(bsz, seq, mix)), sides


def dilated_attention(u3, rel_bias, mix, side_casts=()):
    order = sorted(DSW_PATTERNS, key=lambda wd: wd[1] == 1)
    prev = None
    sides = []
    for gi, (window, dil) in enumerate(order):
        assert window // dil == HEAD_DIM
        prev, side = attn_pattern(u3, attn_bias(rel_bias, window, dil), dil, prev, mix=mix,
                                  last=gi == len(order) - 1, side_casts=side_casts[gi] if side_casts else ())
        sides += side
    return prev, sides


def _split3(v):
    hi = v.astype(BF16)
    r1 = v - hi.astype(F32)
    mid = r1.astype(BF16)
    lo = (r1 - mid.astype(F32)).astype(BF16)
    return hi, mid, lo


def _scaled(v, expo, mask):
    return jnp.where(mask, v * jnp.exp(jnp.where(mask, expo, 0.0)), 0.0)


def _hgrn_kernel(*refs, hb, ts, layer, grid, side_blocks):
    n_side = len(side_blocks)
    cq_ref, cf_ref, ci_ref, cg_ref, lb_ref, ng_ref = refs[:6]
    side_in = refs[6:6 + n_side]
    o_ref = refs[6 + n_side]
    side_out = refs[7 + n_side:7 + 2 * n_side]
    st_ref, q_s, kk_s, b_s, a_s = refs[7 + 2 * n_side:]
    _side_cast_body(grid, side_in, side_out, side_blocks)
    s = pl.program_id(2)
    c_len = HGRN_CHUNK
    sub = 16
    sub8 = 8
    nt = (((1,), (1,)), ((), ()))
    tn_ = (((0,), (0,)), ((), ()))

    @pl.when(s == 0)
    def _():
        st_ref[...] = jnp.zeros(st_ref.shape, F32)

    row = lax.broadcasted_iota(jnp.int32, (c_len, c_len), 0)
    col = lax.broadcasted_iota(jnp.int32, (c_len, c_len), 1)
    tri = jnp.where(row >= col, 1.0, 0.0).astype(BF16)
    r64 = lax.broadcasted_iota(jnp.int32, (c_len, HEAD_DIM), 0)
    half = c_len // 2
    mask_b = (((row >= sub) & (row < half) & (col < sub))
              | ((row >= half + sub) & (col >= half) & (col < half + sub)))
    mask_c = ((row >> 4) == (col >> 4)) & ((row & (sub - 1)) >= sub8) & ((col & (sub - 1)) < sub8)
    lrow = lax.broadcasted_iota(jnp.int32, (lb_ref.shape[0], HEAD_DIM), 0)

    def chunk_body(ci, carry):
        r0 = pl.multiple_of(ci * c_len, c_len)
        rows = pl.ds(r0, c_len)
        head_lanes = [slice(h * HEAD_DIM, (h + 1) * HEAD_DIM) for h in range(hb)]
        for h, lanes in enumerate(head_lanes):
            lg = lb_ref[:, lanes]
            pe = jnp.exp(lg - jnp.max(lg, axis=0, keepdims=True))
            lb = (jnp.sum(jnp.where((lrow >= 1) & (lrow <= layer), pe, 0.0), axis=0, keepdims=True)
                  / jnp.sum(pe, axis=0, keepdims=True))
            f = lb + (1.0 - lb) * jax.nn.sigmoid(cf_ref[0, rows, lanes].astype(F32))
            hi, mid, lo = _split3(jnp.log(f))
            q_s[h] = _silu(cq_ref[0, rows, lanes].astype(F32))
            kk_s[h] = 1.0 - f
            b_s[h] = (jnp.dot(tri, hi, preferred_element_type=F32)
                      + jnp.dot(tri, mid, preferred_element_type=F32)
                      + jnp.dot(tri, lo, preferred_element_type=F32))

        for h in range(hb):
            q, kk, b = q_s[h], kk_s[h], b_s[h]
            b_a = b_s[h, half - 1:half, :]
            qa = _scaled(q, b - b_a, r64 >= half)
            ka = _scaled(kk, b_a - b, r64 < half)
            attn = lax.dot_general(qa.astype(BF16), ka.astype(BF16), nt, preferred_element_type=F32)
            b_r = jnp.where(r64 < half, b_s[h, sub - 1:sub, :], b_s[h, half + sub - 1:half + sub, :])
            qsel = ((r64 >= sub) & (r64 < half)) | (r64 >= half + sub)
            ksel = (r64 < sub) | ((r64 >= half) & (r64 < half + sub))
            qbm = _scaled(q, b - b_r, qsel)
            kbm = _scaled(kk, b_r - b, ksel)
            attn_b = lax.dot_general(qbm.astype(BF16), kbm.astype(BF16), nt, preferred_element_type=F32)
            b_c = b_s[h, sub8 - 1:sub8, :]
            for a in range(1, c_len // sub):
                b_c = jnp.where(r64 >= a * sub, b_s[h, a * sub + sub8 - 1:a * sub + sub8, :], b_c)
            upper = (r64 & (sub - 1)) >= sub8
            qcm = _scaled(q, b - b_c, upper)
            kcm = _scaled(kk, b_c - b, jnp.logical_not(upper))
            attn_c = lax.dot_general(qcm.astype(BF16), kcm.astype(BF16), nt, preferred_element_type=F32)
            a_s[h, :, 0:c_len] = attn + jnp.where(mask_b, attn_b, 0.0) + jnp.where(mask_c, attn_c, 0.0)

        for h in range(hb):
            for jb in range(c_len // sub8):
                blk = slice(jb * sub8, (jb + 1) * sub8)
                qt = q_s[h, blk, :]
                bt = b_s[h, blk, :]
                for si in range(sub8):
                    r = jb * sub8 + si
                    e = jnp.exp(bt - b_s[h, r:r + 1, :])
                    a_s[h, blk, r:r + 1] = jnp.sum(qt * kk_s[h, r:r + 1, :] * e, axis=-1, keepdims=True)

        for h, lanes in enumerate(head_lanes):
            q, kk, b = q_s[h], kk_s[h], b_s[h]
            v = ci_ref[0, rows, lanes]
            b_last = b_s[h, c_len - 1:c_len, :]
            st_t = st_ref[h]
            inter = lax.dot_general((q * jnp.exp(b)).astype(BF16), st_t.astype(BF16), nt,
                                    preferred_element_type=F32)
            attn = jnp.where(row >= col, a_s[h, :, 0:c_len], 0.0)
            o = inter + jnp.dot(attn.astype(BF16), v, preferred_element_type=F32)
            kd = (kk * jnp.exp(b_last - b)).astype(BF16)
            st_ref[h] = st_t * jnp.exp(b_last) + lax.dot_general(v, kd, tn_, preferred_element_type=F32)
            ms = jnp.mean(o * o, axis=-1, keepdims=True)
            o = o * lax.rsqrt(ms + LN_EPS) * ng_ref[0:1, lanes]
            o = o * _silu(cg_ref[0, rows, lanes].astype(F32))
            o_ref[0, rows, lanes] = o.astype(o_ref.dtype)
        return carry
    lax.fori_loop(0, ts // c_len, chunk_body, 0)


def hgrn_branch(u3, lb_logits, layer, norm_g, mix, *, side_casts=(), hb=16, ts=256):
    bsz, seq, _ = u3.shape
    heads = mix // HEAD_DIM
    wb = hb * HEAD_DIM
    nb = mix // wb
    grid = (bsz, heads // hb, seq // ts)
    s_in, s_out, s_shapes, side_blocks, s_args = _side_cast_specs(side_casts, grid)

    def col(k):
        return pl.BlockSpec((1, ts, wb), lambda b, g, s: (b, s, k * nb + g))

    vec = pl.BlockSpec((1, wb), lambda b, g, s: (0, g))
    res = pl.pallas_call(
        functools.partial(_hgrn_kernel, hb=hb, ts=ts, layer=layer, grid=grid, side_blocks=side_blocks),
        grid=grid,
        in_specs=[col(0), col(1), col(2), col(3),
                  pl.BlockSpec((lb_logits.shape[0], wb), lambda b, g, s: (0, g)), vec] + s_in,
        out_specs=[pl.BlockSpec((1, ts, wb), lambda b, g, s: (b, s, g))] + s_out,
        out_shape=[jax.ShapeDtypeStruct((bsz, seq, mix), BF16)] + s_shapes,
        scratch_shapes=[pltpu.VMEM((hb, HEAD_DIM, HEAD_DIM), F32)]
        + [pltpu.VMEM((hb, HGRN_CHUNK, HEAD_DIM), F32)] * 4,
        compiler_params=_cparams(("arbitrary",) * 3),
        name="hgrn",
    )(u3, u3, u3, u3, lb_logits.astype(F32), norm_g.reshape(1, mix), *s_args)
    return res[0], list(res[1:])


POOL_HALO = 16


def _pool_kernel(dp_ref, pw_ref, ps_ref, o_ref, x_ref, p_ref, *, ts, rc):
    s = pl.program_id(1)
    ch = o_ref.shape[2]
    grp = ch // len(POOL_WINDOWS)

    @pl.when(s == 0)
    def _():
        x_ref[:, 0:POOL_HALO, :] = jnp.zeros((ch // 128, POOL_HALO, 128), F32)

    @pl.when(s > 0)
    def _():
        x_ref[:, 0:POOL_HALO, :] = x_ref[:, ts:ts + POOL_HALO, :]

    def load_body(i, carry):
        r0 = pl.multiple_of(i * rc, rc)
        xv = dp_ref[0, pl.ds(r0, rc), :].astype(F32)
        for cc in range(ch // 128):
            x_ref[cc, pl.ds(POOL_HALO + r0, rc), :] = xv[:, cc * 128:(cc + 1) * 128]
        return carry
    lax.fori_loop(0, ts // rc, load_body, 0)

    def pool_body(i, carry):
        r0 = pl.multiple_of(i * rc, rc)
        pos = s * ts + r0 + lax.broadcasted_iota(jnp.int32, (rc, 1), 0)
        for cc in range(ch // 128):
            w = POOL_WINDOWS[(cc * 128) // grp]
            cur = x_ref[cc, pl.ds(POOL_HALO + r0, rc), :]
            tot = cur
            for j in range(1, w):
                tot = tot + x_ref[cc, pl.ds(POOL_HALO + r0 - j, rc), :]
            cnt = jnp.minimum(pos + 1, w).astype(F32)
            p_ref[pl.ds(r0, rc), cc * 128:(cc + 1) * 128] = (tot / cnt - cur).astype(BF16)
        return carry
    lax.fori_loop(0, ts // rc, pool_body, 0)

    for gi in range(len(POOL_WINDOWS)):
        lanes = slice(gi * grp, (gi + 1) * grp)
        y = jnp.dot(p_ref[:, lanes], pw_ref[gi], preferred_element_type=F32)
        o_ref[0, :, lanes] = (y * ps_ref[0:1, lanes]).astype(o_ref.dtype)


def pool_branch(u3, pool_w, pool_scale, mix, *, ts=512, rc=32):
    bsz, seq, _ = u3.shape
    ng, grp, _ = pool_w.shape
    return pl.pallas_call(
        functools.partial(_pool_kernel, ts=ts, rc=rc),
        grid=(bsz, seq // ts),
        in_specs=[
            pl.BlockSpec((1, ts, mix), lambda b, s: (b, s, 4)),
            pl.BlockSpec((ng, grp, grp), lambda b, s: (0, 0, 0)),
            pl.BlockSpec((1, mix), lambda b, s: (0, 0)),
        ],
        out_specs=pl.BlockSpec((1, ts, mix), lambda b, s: (b, s, 0)),
        out_shape=jax.ShapeDtypeStruct((bsz, seq, mix), BF16),
        scratch_shapes=[pltpu.VMEM((mix // 128, ts + POOL_HALO, 128), F32), pltpu.VMEM((ts, mix), BF16)],
        compiler_params=_cparams(("arbitrary", "arbitrary")),
        name="pool_branch",
    )(u3, pool_w.astype(BF16), pool_scale.reshape(1, mix))


def kernel(x, c, ada_w, ada_b, w_in, w_out, ln_g, ln_b, mlp_w1, mlp_w2, conv_w, conv_b, conv_ln_g, conv_ln_b,
           rel_bias, hgrn_lb_logits, hgrn_norm_g, pool_w, pool_scale):
    bsz, seq, d = x.shape
    depth = ada_w.shape[0]
    mix = d // 2
    alpha = (2.0 * depth) ** 0.25
    m = bsz * seq

    mod = adaln_mod(c, ada_w, ada_b)
    x2 = x.reshape(m, d)
    h = modulate(x2, mod[0], seq)
    w_in_b = cast_layer_bf16(w_in, 0)
    for l in range(depth):
        u = mm_in(h, w_in_b, perm_from=2 * mix if l % 2 == 0 else None)
        u3 = u.reshape(bsz, seq, 5 * mix)
        if l % 2 == 0:
            e = l // 2
            conv_sides = [(mlp_w2, l, 256)] + ([(w_in, l + 1, 64)] if l + 1 < depth else [])
            za, sides = conv_branch(u3, conv_w[e], conv_b[e], conv_ln_g[e], conv_ln_b[e], mix,
                                    side_casts=conv_sides)
            w2_b = sides[0]
            if l + 1 < depth:
                w_in_b = sides[1]
            zb, (w_out_b, w1_b) = dilated_attention(
                u3, rel_bias, mix, side_casts=[[], [(w_out, l, 64)], [(mlp_w1, l, 64)]])
        else:
            if l + 1 < depth:
                w_in_b = cast_layer_bf16(w_in, l + 1)
            o = l // 2
            za, (w_out_b, w1_b, w2_b) = hgrn_branch(
                u3, hgrn_lb_logits, l, hgrn_norm_g[o], mix,
                side_casts=[(w_out, l, 64), (mlp_w1, l, 64), (mlp_w2, l, 256)])
            zb = pool_branch(u3, pool_w[o], pool_scale[o], mix)
        x2, h2 = mm_out_ln(za.reshape(m, mix), zb.reshape(m, mix), w_out_b, x2, mod[l],
                           ln_g[l, 0], ln_b[l, 0], seq, alpha)
        x2, h = mlp_ln(h2, x2, mod[l], mod[l + 1] if l + 1 < depth else None,
                       w1_b, w2_b, ln_g[l, 1], ln_b[l, 1], seq, alpha)
    return x2.reshape(bsz, seq, d)
```
